```python
import math
import jax
import jax.numpy as jnp
from jax import lax
import numpy as np

D_MODEL = 4096
BATCH = 4
SEQ = 2048
DEPTH = 4
DEC_BATCH = 128
DEC_SEQ = 8
PAST_LEN = 16384
PAGE_SIZE = 128

F32 = jnp.float32
EPS = 1e-6
D_MIX = D_MODEL
N_MIXERS = 4
D_GROUP = D_MIX // N_MIXERS
CONV_W = 4
CHUNK = 64
SSD_HEAD_DIM = 64
SSD_HEADS = D_GROUP // SSD_HEAD_DIM
SSD_NGROUPS = 2
SSD_STATE = 128
SSD_CONV_DIM = D_GROUP + 2 * SSD_NGROUPS * SSD_STATE
GDN_HEAD_DIM = 128
GDN_HEADS = D_GROUP // GDN_HEAD_DIM
GDN_CONV_DIM = 3 * D_GROUP
MLSTM_HEAD_DIM = 128
MLSTM_HEADS = D_GROUP // MLSTM_HEAD_DIM
LRU_BLOCKS = 8
LRU_BLOCK = D_GROUP // LRU_BLOCKS
LRU_C = 8.0
N_EXPERT_GROUPS = 4
EXPERTS_PER_GROUP = 4
N_EXPERTS = N_EXPERT_GROUPS * EXPERTS_PER_GROUP
TOP_K_IN_GROUP = 2
D_FF_EXPERT = D_MODEL // 8
IN_SIZES = (D_GROUP, SSD_CONV_DIM, SSD_HEADS,
            GDN_CONV_DIM, GDN_HEADS, GDN_HEADS, D_GROUP,
            3 * D_GROUP, MLSTM_HEADS, MLSTM_HEADS, D_GROUP,
            D_GROUP, D_GROUP)
D_IN_PROJ = sum(IN_SIZES)

kernel_name = 'hymba_style_ssd_gdn_mlstm_rglru_hmoe_step'


def _state_shapes(n):
    return ((n, SSD_HEADS, SSD_HEAD_DIM, SSD_STATE),
            (n, CONV_W - 1, SSD_CONV_DIM),
            (n, GDN_HEADS, GDN_HEAD_DIM, GDN_HEAD_DIM),
            (n, CONV_W - 1, GDN_CONV_DIM),
            (n, MLSTM_HEADS, MLSTM_HEAD_DIM, MLSTM_HEAD_DIM),
            (n, MLSTM_HEADS, MLSTM_HEAD_DIM),
            (n, MLSTM_HEADS),
            (n, D_GROUP),
            (n, CONV_W - 1, D_GROUP))


def _zero_states(n, dtype):
    return tuple(jnp.zeros((DEPTH,) + s, dtype) for s in _state_shapes(n))


def _chunk_size(L):
    return L if L <= CHUNK else math.gcd(L, CHUNK)


def _rmsnorm(x, g):
    xf = x.astype(F32)
    y = xf * lax.rsqrt(jnp.mean(xf * xf, axis=-1, keepdims=True) + EPS)
    return (y * g.astype(F32)).astype(x.dtype)


def _group_rmsnorm(y, g, n_groups):
    shp = y.shape
    yg = y.astype(F32).reshape(shp[:-1] + (n_groups, shp[-1] // n_groups))
    yg = yg * lax.rsqrt(jnp.mean(yg * yg, axis=-1, keepdims=True) + EPS)
    return yg.reshape(shp) * g.astype(F32)


def _l2norm(x):
    return x * lax.rsqrt(jnp.sum(x * x, axis=-1, keepdims=True) + EPS)


def _causal_conv(u, buf, w, b):
    L = u.shape[1]
    full = jnp.concatenate([buf.astype(F32), u.astype(F32)], axis=1)
    out = b.astype(F32) + full[:, 0:L] * w[0].astype(F32)
    for j in range(1, CONV_W):
        out = out + full[:, j:j + L] * w[j].astype(F32)
    return out, full[:, L:]


def _to_chunks(t, cs):
    b, L = t.shape[:2]
    t = t.reshape((b, L // cs, cs) + t.shape[2:])
    return jnp.swapaxes(t, 2, 3)


def _from_chunks(t):
    t = jnp.swapaxes(t, 2, 3)
    return t.reshape((t.shape[0], t.shape[1] * t.shape[2]) + t.shape[3:])


def _ssd_chunked(x, dt, A, Bm, Cm, S0):
    Bsz, L, H, P = x.shape
    N = Bm.shape[-1]
    cs = _chunk_size(L)
    nc = L // cs
    xc = x.reshape(Bsz, nc, cs, H, P)
    dtc = dt.reshape(Bsz, nc, cs, H)
    Bc = Bm.reshape(Bsz, nc, cs, H, N)
    Cc = Cm.reshape(Bsz, nc, cs, H, N)
    cum = jnp.cumsum(dtc * A, axis=2)
    causal = jnp.tril(jnp.ones((cs, cs), dtype=bool))[:, :, None]
    seg = jnp.where(causal, cum[:, :, :, None, :] - cum[:, :, None, :, :], -jnp.inf)
    scores = jnp.einsum('bcthn,bcshn->bctsh', Cc, Bc) * jnp.exp(seg)
    xdt = xc * dtc[..., None]
    y_intra = jnp.einsum('bctsh,bcshp->bcthp', scores, xdt)
    w_end = jnp.exp(cum[:, :, -1:, :] - cum)
    chunk_S = jnp.einsum('bcsh,bcshp,bcshn->bchpn', w_end, xdt, Bc)
    chunk_decay = jnp.exp(cum[:, :, -1, :])

    def step(S, inp):
        dS, dec = inp
        return S * dec[..., None, None] + dS, S

    S_fin, S_start = lax.scan(step, S0, (jnp.swapaxes(chunk_S, 0, 1), jnp.swapaxes(chunk_decay, 0, 1)))
    y_inter = jnp.einsum('bcthn,cbhpn->bcthp', Cc * jnp.exp(cum)[..., None], S_start)
    return (y_intra + y_inter).reshape(Bsz, L, H, P), S_fin


def _ssd_mixer(z, xbc_raw, dt_raw, conv_buf, S0, conv_w, conv_b, dt_bias, a_log, d_skip, norm_g):
    Bsz, L, _ = z.shape
    xbc, buf = _causal_conv(xbc_raw, conv_buf, conv_w, conv_b)
    xbc = jax.nn.silu(xbc)
    xs, Bm, Cm = jnp.split(xbc, [D_GROUP, D_GROUP + SSD_NGROUPS * SSD_STATE], axis=-1)
    xs = xs.reshape(Bsz, L, SSD_HEADS, SSD_HEAD_DIM)
    rep = SSD_HEADS // SSD_NGROUPS
    Bm = jnp.repeat(Bm.reshape(Bsz, L, SSD_NGROUPS, SSD_STATE), rep, axis=2)
    Cm = jnp.repeat(Cm.reshape(Bsz, L, SSD_NGROUPS, SSD_STATE), rep, axis=2)
    dt = jax.nn.softplus(dt_raw.astype(F32) + dt_bias.astype(F32))
    A = -jnp.exp(a_log.astype(F32))
    y, S = _ssd_chunked(xs, dt, A, Bm, Cm, S0.astype(F32))
    y = (y + d_skip.astype(F32)[:, None] * xs).reshape(Bsz, L, D_GROUP)
    y = _group_rmsnorm(y * jax.nn.silu(z.astype(F32)), norm_g, SSD_NGROUPS)
    return y, buf, S


def _gdn_chunked(q, k, v, g, beta, S0):
    cs = _chunk_size(q.shape[1])
    qc, kc, vc = _to_chunks(q, cs), _to_chunks(k, cs), _to_chunks(v, cs)
    gc = jnp.cumsum(_to_chunks(g, cs), axis=-1)
    bc = _to_chunks(beta, cs)
    incl = jnp.tril(jnp.ones((cs, cs), dtype=bool))
    strict = jnp.tril(jnp.ones((cs, cs), dtype=bool), -1)
    decay = jnp.exp(jnp.where(incl, gc[..., :, None] - gc[..., None, :], -jnp.inf))
    kk = jnp.einsum('bnhtk,bnhsk->bnhts', kc, kc)
    a_mat = jnp.where(strict, bc[..., :, None] * kk * decay, 0.0) + jnp.eye(cs, dtype=F32)
    rhs = jnp.concatenate([vc * bc[..., None], kc * (bc * jnp.exp(gc))[..., None]], axis=-1)
    sol = lax.linalg.triangular_solve(a_mat, rhs, left_side=True, lower=True, unit_diagonal=True)
    dv = v.shape[-1]
    u, w = sol[..., :dv], sol[..., dv:]
    qk = jnp.einsum('bnhtk,bnhsk->bnhts', qc, kc) * decay
    q_dec = qc * jnp.exp(gc)[..., None]
    k_dec = kc * jnp.exp(gc[..., -1:] - gc)[..., None]
    g_last = jnp.exp(gc[..., -1])

    def step(S, inp):
        u_i, w_i, qk_i, qd_i, kd_i, gl_i = inp
        v_new = u_i - jnp.einsum('bhtk,bhkv->bhtv', w_i, S)
        o = jnp.einsum('bhtk,bhkv->bhtv', qd_i, S) + jnp.einsum('bhts,bhsv->bhtv', qk_i, v_new)
        S = S * gl_i[..., None, None] + jnp.einsum('bhsk,bhsv->bhkv', kd_i, v_new)
        return S, o

    xs = tuple(jnp.swapaxes(t, 0, 1) for t in (u, w, qk, q_dec, k_dec, g_last))
    S_fin, o = lax.scan(step, S0, xs)
    return _from_chunks(jnp.swapaxes(o, 0, 1)), S_fin


def _gdn_mixer(qkv_raw, a_raw, b_raw, z, conv_buf, S0, conv_w, conv_b, dt_bias, a_log, norm_g):
    Bsz, L, _ = z.shape
    qkv, buf = _causal_conv(qkv_raw, conv_buf, conv_w, conv_b)
    q, k, v = jnp.split(jax.nn.silu(qkv), 3, axis=-1)
    shp = (Bsz, L, GDN_HEADS, GDN_HEAD_DIM)
    q = _l2norm(q.reshape(shp)) * (GDN_HEAD_DIM ** -0.5)
    k = _l2norm(k.reshape(shp))
    v = v.reshape(shp)
    g = -jnp.exp(a_log.astype(F32)) * jax.nn.softplus(a_raw.astype(F32) + dt_bias.astype(F32))
    beta = jax.nn.sigmoid(b_raw.astype(F32))
    o, S = _gdn_chunked(q, k, v, g, beta, S0.astype(F32))
    o = _group_rmsnorm(o.reshape(Bsz, L, D_GROUP), norm_g, GDN_HEADS) * jax.nn.silu(z.astype(F32))
    return o, buf, S


def _mlstm_chunked(q, k, v, ig, lf, C0, n0, m0):
    cs = _chunk_size(q.shape[1])
    qc, kc, vc = _to_chunks(q, cs), _to_chunks(k, cs), _to_chunks(v, cs)
    igc = _to_chunks(ig, cs)
    F = jnp.cumsum(_to_chunks(lf, cs), axis=-1)
    incl = jnp.tril(jnp.ones((cs, cs), dtype=bool))
    d_log = jnp.where(incl, F[..., :, None] - F[..., None, :] + igc[..., None, :], -jnp.inf)
    d_max = jnp.max(d_log, axis=-1)
    qk = jnp.einsum('bnhtk,bnhsk->bnhts', qc, kc)
    d_end = F[..., -1:] - F + igc

    def step(carry, inp):
        C, n, m = carry
        q_i, k_i, v_i, F_i, dl_i, dm_i, qk_i, de_i = inp
        m_t = jnp.maximum(F_i + m[..., None], dm_i)
        w_carry = jnp.exp(F_i + m[..., None] - m_t)
        P = jnp.exp(dl_i - m_t[..., None]) * qk_i
        num = w_carry[..., None] * jnp.einsum('bhtk,bhkv->bhtv', q_i, C) + jnp.einsum('bhts,bhsv->bhtv', P, v_i)
        den = w_carry * jnp.einsum('bhtk,bhk->bht', q_i, n) + jnp.sum(P, axis=-1)
        h = num / jnp.maximum(jnp.abs(den), jnp.exp(-m_t))[..., None]
        m_end = m_t[..., -1]
        w_prev = jnp.exp(F_i[..., -1] + m - m_end)
        kw = k_i * jnp.exp(de_i - m_end[..., None])[..., None]
        C = w_prev[..., None, None] * C + jnp.einsum('bhsk,bhsv->bhkv', kw, v_i)
        n = w_prev[..., None] * n + jnp.sum(kw, axis=2)
        return (C, n, m_end), h

    xs = tuple(jnp.swapaxes(t, 0, 1) for t in (qc, kc, vc, F, d_log, d_max, qk, d_end))
    (C, n, m), h = lax.scan(step, (C0, n0, m0), xs)
    return _from_chunks(jnp.swapaxes(h, 0, 1)), C, n, m


def _mlstm_mixer(qkv, i_raw, f_raw, o_raw, C0, n0, m0, i_bias, f_bias, norm_g):
    Bsz, L, _ = qkv.shape
    shp = (Bsz, L, MLSTM_HEADS, MLSTM_HEAD_DIM)
    q, k, v = jnp.split(qkv.astype(F32), 3, axis=-1)
    q = q.reshape(shp)
    k = k.reshape(shp) * (MLSTM_HEAD_DIM ** -0.5)
    v = v.reshape(shp)
    ig = i_raw.astype(F32) + i_bias.astype(F32)
    lf = jax.nn.log_sigmoid(f_raw.astype(F32) + f_bias.astype(F32))
    h, C, n, m = _mlstm_chunked(q, k, v, ig, lf, C0.astype(F32), n0.astype(F32), m0.astype(F32))
    y = _group_rmsnorm(h.reshape(Bsz, L, D_GROUP), norm_g, MLSTM_HEADS) * jax.nn.sigmoid(o_raw.astype(F32))
    return y, C, n, m


def _lin_combine(left, right):
    a_l, b_l = left
    a_r, b_r = right
    return a_l * a_r, a_r * b_l + b_r


def _lru_mixer(x_raw, gate_raw, conv_buf, h0, conv_w, conv_b, w_a, b_a, w_x, b_x, lam):
    Bsz, L, _ = x_raw.shape
    xc, buf = _causal_conv(x_raw, conv_buf, conv_w, conv_b)
    xb = xc.reshape(Bsz, L, LRU_BLOCKS, LRU_BLOCK)
    r = jax.nn.sigmoid(jnp.einsum('blgi,gij->blgj', xb, w_a.astype(F32)) + b_a.astype(F32)).reshape(Bsz, L, D_GROUP)
    i = jax.nn.sigmoid(jnp.einsum('blgi,gij->blgj', xb, w_x.astype(F32)) + b_x.astype(F32)).reshape(Bsz, L, D_GROUP)
    log_a = -LRU_C * r * jax.nn.softplus(-lam.astype(F32))
    a = jnp.exp(log_a)
    u = jnp.sqrt(-jnp.expm1(2.0 * log_a)) * (i * xc)
    u = u.at[:, 0].add(a[:, 0] * h0.astype(F32))
    _, h = lax.associative_scan(_lin_combine, (a, u), axis=1)
    y = h * jax.nn.gelu(gate_raw.astype(F32))
    return y, buf, h[:, -1]


def _hier_moe(h, w_group, b_group, w_expert, b_expert, w_gate, w_up, w_down):
    Bsz, L, D = h.shape
    t = h.reshape(Bsz * L, D)
    g_logit = jnp.dot(t, w_group).astype(F32) + b_group.astype(F32)
    g_prob = jax.nn.softmax(g_logit, axis=-1)
    _, g_idx = lax.top_k(g_logit, 1)
    g_w = jnp.take_along_axis(g_prob, g_idx, axis=-1)
    e_logit = (jnp.dot(t, w_expert).astype(F32) + b_expert.astype(F32)).reshape(-1, N_EXPERT_GROUPS, EXPERTS_PER_GROUP)
    e_in_group = jnp.take_along_axis(e_logit, g_idx[:, :, None], axis=1)[:, 0]
    e_top, e_idx = lax.top_k(e_in_group, TOP_K_IN_GROUP)
    w_sel = jax.nn.softmax(e_top, axis=-1) * g_w
    expert_id = g_idx * EXPERTS_PER_GROUP + e_idx
    gates = jnp.sum(jax.nn.one_hot(expert_id, N_EXPERTS, dtype=F32) * w_sel[..., None], axis=1).astype(h.dtype)
    out = jnp.zeros_like(t)
    for e in range(N_EXPERTS):
        he = jax.nn.silu(jnp.dot(t, w_gate[e])) * jnp.dot(t, w_up[e])
        out = out + gates[:, e:e + 1] * jnp.dot(he, w_down[e])
    return out.reshape(Bsz, L, D)


def _mixer_block(h, st, p, l):
    s_ssd, s_ssd_conv, s_gdn, s_gdn_conv, s_mc, s_mn, s_mm, s_lru, s_lru_conv = st
    offsets = [int(o) for o in np.cumsum(IN_SIZES)[:-1]]
    (ssd_z, ssd_xbc, ssd_dt, gdn_qkv, gdn_a, gdn_b, gdn_z,
     ml_qkv, ml_i, ml_f, ml_o, lru_x, lru_gate) = jnp.split(jnp.dot(h, p['w_in'][l]), offsets, axis=-1)
    y_a, ssd_conv_new, ssd_new = _ssd_mixer(ssd_z, ssd_xbc, ssd_dt, s_ssd_conv, s_ssd,
                                            p['ssd_conv_w'][l], p['ssd_conv_b'][l], p['ssd_dt_bias'][l],
                                            p['ssd_a_log'][l], p['ssd_d'][l], p['ssd_norm_g'][l])
    y_b, gdn_conv_new, gdn_new = _gdn_mixer(gdn_qkv, gdn_a, gdn_b, gdn_z, s_gdn_conv, s_gdn,
                                            p['gdn_conv_w'][l], p['gdn_conv_b'][l], p['gdn_dt_bias'][l],
                                            p['gdn_a_log'][l], p['gdn_norm_g'][l])
    y_c, mc_new, mn_new, mm_new = _mlstm_mixer(ml_qkv, ml_i, ml_f, ml_o, s_mc, s_mn, s_mm,
                                               p['mlstm_i_bias'][l], p['mlstm_f_bias'][l], p['mlstm_norm_g'][l])
    y_d, lru_conv_new, lru_new = _lru_mixer(lru_x, lru_gate, s_lru_conv, s_lru,
                                            p['lru_conv_w'][l], p['lru_conv_b'][l], p['lru_w_a'][l],
                                            p['lru_b_a'][l], p['lru_w_x'][l], p['lru_b_x'][l], p['lru_lambda'][l])
    y = jnp.concatenate([y_a, y_b, y_c, y_d], axis=-1).astype(h.dtype)
    y = jnp.dot(y, p['w_out'][l])
    return y, (ssd_new, ssd_conv_new, gdn_new, gdn_conv_new, mc_new, mn_new, mm_new, lru_new, lru_conv_new)


def _trunk(x, c, states, p):
    sdtype = x.dtype
    mod_shared = jnp.dot(jax.nn.silu(c), p['w_ada'])
    new_states = [[] for _ in states]
    for l in range(DEPTH):
        mod = (mod_shared + p['ada_table'][l])[:, None, :]
        sh1, sc1, g1, sh2, sc2, g2 = jnp.split(mod, 6, axis=-1)
        h = _rmsnorm(x, p['norm1_g'][l]) * (1.0 + sc1) + sh1
        y, layer_states = _mixer_block(h, tuple(s[l] for s in states), p, l)
        x = x + g1 * y
        h = _rmsnorm(x, p['norm2_g'][l]) * (1.0 + sc2) + sh2
        x = x + g2 * _hier_moe(h, p['moe_w_group'][l], p['moe_b_group'][l], p['moe_w_expert'][l],
                               p['moe_b_expert'][l], p['moe_w_gate'][l], p['moe_w_up'][l], p['moe_w_down'][l])
        for acc, s in zip(new_states, layer_states):
            acc.append(s.astype(sdtype))
    return _rmsnorm(x, p['final_g']), tuple(jnp.stack(acc) for acc in new_states)


def setup_inputs(seed: int = 0) -> dict:
    key = jax.random.key(seed)
    ks = list(jax.random.split(key, 96))

    def nrm(shape, scale=1.0):
        return jax.random.normal(ks.pop(), shape, jnp.float32) * scale

    def unif(shape, lo, hi):
        return jax.random.uniform(ks.pop(), shape, jnp.float32, lo, hi)

    def dt_bias(shape):
        dt = jnp.exp(unif(shape, math.log(1e-3), math.log(1e-1)))
        return dt + jnp.log(-jnp.expm1(-dt))

    def gain(shape):
        return 1.0 + nrm(shape, 0.02)

    D = D_MODEL
    st = _state_shapes(DEC_BATCH)
    lam_s = unif((DEPTH, D_GROUP), 0.9, 0.999) ** (1.0 / LRU_C)
    return {
        'x_prompt': nrm((BATCH, SEQ, D)),
        'x_sample': nrm((DEC_BATCH, DEC_SEQ, D)),
        'state_ssd': nrm((DEPTH,) + st[0], 0.1),
        'state_ssd_conv': nrm((DEPTH,) + st[1]),
        'state_gdn': nrm((DEPTH,) + st[2], 0.1),
        'state_gdn_conv': nrm((DEPTH,) + st[3]),
        'state_mlstm_c': nrm((DEPTH,) + st[4], 0.1),
        'state_mlstm_n': nrm((DEPTH,) + st[5], 0.1),
        'state_mlstm_m': nrm((DEPTH,) + st[6]),
        'state_rglru': nrm((DEPTH,) + st[7], 0.5),
        'state_rglru_conv': nrm((DEPTH,) + st[8]),
        'c_prompt': nrm((BATCH, D)),
        'c_sample': nrm((DEC_BATCH, D)),
        'w_ada': nrm((D, 6 * D), 0.5 * D ** -0.5),
        'ada_table': nrm((DEPTH, 6 * D), 0.1),
        'norm1_g': gain((DEPTH, D)),
        'norm2_g': gain((DEPTH, D)),
        'final_g': gain((D,)),
        'w_in': nrm((DEPTH, D, D_IN_PROJ), D ** -0.5),
        'w_out': nrm((DEPTH, D_MIX, D), D_MIX ** -0.5),
        'ssd_conv_w': nrm((DEPTH, CONV_W, SSD_CONV_DIM), CONV_W ** -0.5),
        'ssd_conv_b': nrm((DEPTH, SSD_CONV_DIM), 0.02),
        'ssd_dt_bias': dt_bias((DEPTH, SSD_HEADS)),
        'ssd_a_log': jnp.log(unif((DEPTH, SSD_HEADS), 1.0, 16.0)),
        'ssd_d': 1.0 + nrm((DEPTH, SSD_HEADS), 0.1),
        'ssd_norm_g': gain((DEPTH, D_GROUP)),
        'gdn_conv_w': nrm((DEPTH, CONV_W, GDN_CONV_DIM), CONV_W ** -0.5),
        'gdn_conv_b': nrm((DEPTH, GDN_CONV_DIM), 0.02),
        'gdn_dt_bias': dt_bias((DEPTH, GDN_HEADS)),
        'gdn_a_log': jnp.log(unif((DEPTH, GDN_HEADS), 1.0, 16.0)),
        'gdn_norm_g': gain((DEPTH, D_GROUP)),
        'mlstm_i_bias': nrm((DEPTH, MLSTM_HEADS), 0.1),
        'mlstm_f_bias': 3.0 + nrm((DEPTH, MLSTM_HEADS), 0.1),
        'mlstm_norm_g': gain((DEPTH, D_GROUP)),
        'lru_conv_w': nrm((DEPTH, CONV_W, D_GROUP), CONV_W ** -0.5),
        'lru_conv_b': nrm((DEPTH, D_GROUP), 0.02),
        'lru_w_a': nrm((DEPTH, LRU_BLOCKS, LRU_BLOCK, LRU_BLOCK), LRU_BLOCK ** -0.5),
        'lru_b_a': nrm((DEPTH, LRU_BLOCKS, LRU_BLOCK), 0.02),
        'lru_w_x': nrm((DEPTH, LRU_BLOCKS, LRU_BLOCK, LRU_BLOCK), LRU_BLOCK ** -0.5),
        'lru_b_x': nrm((DEPTH, LRU_BLOCKS, LRU_BLOCK), 0.02),
        'lru_lambda': jnp.log(lam_s) - jnp.log1p(-lam_s),
        'moe_w_group': nrm((DEPTH, D, N_EXPERT_GROUPS), D ** -0.5),
        'moe_b_group': nrm((DEPTH, N_EXPERT_GROUPS), 0.01),
        'moe_w_expert': nrm((DEPTH, D, N_EXPERTS), D ** -0.5),
        'moe_b_expert': nrm((DEPTH, N_EXPERTS), 0.01),
        'moe_w_gate': nrm((DEPTH, N_EXPERTS, D, D_FF_EXPERT), D ** -0.5),
        'moe_w_up': nrm((DEPTH, N_EXPERTS, D, D_FF_EXPERT), D ** -0.5),
        'moe_w_down': nrm((DEPTH, N_EXPERTS, D_FF_EXPERT, D), D_FF_EXPERT ** -0.5),
    }


def reference(x_prompt, x_sample, state_ssd, state_ssd_conv, state_gdn, state_gdn_conv,
              state_mlstm_c, state_mlstm_n, state_mlstm_m, state_rglru, state_rglru_conv,
              c_prompt, c_sample, w_ada, ada_table, norm1_g, norm2_g, final_g, w_in, w_out,
              ssd_conv_w, ssd_conv_b, ssd_dt_bias, ssd_a_log, ssd_d, ssd_norm_g,
              gdn_conv_w, gdn_conv_b, gdn_dt_bias, gdn_a_log, gdn_norm_g,
              mlstm_i_bias, mlstm_f_bias, mlstm_norm_g,
              lru_conv_w, lru_conv_b, lru_w_a, lru_b_a, lru_w_x, lru_b_x, lru_lambda,
              moe_w_group, moe_b_group, moe_w_expert, moe_b_expert, moe_w_gate, moe_w_up, moe_w_down):
    p = dict(w_ada=w_ada, ada_table=ada_table, norm1_g=norm1_g, norm2_g=norm2_g, final_g=final_g,
             w_in=w_in, w_out=w_out,
             ssd_conv_w=ssd_conv_w, ssd_conv_b=ssd_conv_b, ssd_dt_bias=ssd_dt_bias, ssd_a_log=ssd_a_log,
             ssd_d=ssd_d, ssd_norm_g=ssd_norm_g,
             gdn_conv_w=gdn_conv_w, gdn_conv_b=gdn_conv_b, gdn_dt_bias=gdn_dt_bias, gdn_a_log=gdn_a_log,
             gdn_norm_g=gdn_norm_g,
             mlstm_i_bias=mlstm_i_bias, mlstm_f_bias=mlstm_f_bias, mlstm_norm_g=mlstm_norm_g,
             lru_conv_w=lru_conv_w, lru_conv_b=lru_conv_b, lru_w_a=lru_w_a, lru_b_a=lru_b_a,
             lru_w_x=lru_w_x, lru_b_x=lru_b_x, lru_lambda=lru_lambda,
             moe_w_group=moe_w_group, moe_b_group=moe_b_group, moe_w_expert=moe_w_expert,
             moe_b_expert=moe_b_expert, moe_w_gate=moe_w_gate, moe_w_up=moe_w_up, moe_w_down=moe_w_down)
    y_prompt, new_p = _trunk(x_prompt, c_prompt, _zero_states(x_prompt.shape[0], x_prompt.dtype), p)
    y_sample, new_s = _trunk(x_sample, c_sample,
                             (state_ssd, state_ssd_conv, state_gdn, state_gdn_conv, state_mlstm_c,
                              state_mlstm_n, state_mlstm_m, state_rglru, state_rglru_conv), p)
    (pr_ssd, pr_ssd_conv, pr_gdn, pr_gdn_conv, pr_mc, pr_mn, pr_mm, pr_lru, pr_lru_conv) = new_p
    (sa_ssd, sa_ssd_conv, sa_gdn, sa_gdn_conv, sa_mc, sa_mn, sa_mm, sa_lru, sa_lru_conv) = new_s
    return (y_prompt, y_sample,
            pr_ssd, sa_ssd, pr_ssd_conv, sa_ssd_conv, pr_gdn, sa_gdn, pr_gdn_conv, sa_gdn_conv,
            pr_mc, sa_mc, pr_mn, sa_mn, pr_mm, sa_mm, pr_lru, sa_lru, pr_lru_conv, sa_lru_conv)
```

```python
import functools
import math

import numpy as np
import jax
import jax.numpy as jnp
from jax import lax
from jax.experimental import pallas as pl
from jax.experimental.pallas import tpu as pltpu

F32 = jnp.float32
BF16 = jnp.bfloat16

D_MODEL = 4096
BATCH = 4
SEQ = 2048
DEPTH = 4
DEC_BATCH = 128
DEC_SEQ = 8
EPS = 1e-6
N_MIXERS = 4
D_GROUP = D_MODEL // N_MIXERS
CONV_W = 4
CHUNK = 64
SSD_HEAD_DIM = 64
SSD_HEADS = D_GROUP // SSD_HEAD_DIM
SSD_NGROUPS = 2
SSD_STATE = 128
SSD_CONV_DIM = D_GROUP + 2 * SSD_NGROUPS * SSD_STATE
GDN_HEAD_DIM = 128
GDN_HEADS = D_GROUP // GDN_HEAD_DIM
GDN_CONV_DIM = 3 * D_GROUP
MLSTM_HEAD_DIM = 128
MLSTM_HEADS = D_GROUP // MLSTM_HEAD_DIM
LRU_BLOCKS = 8
LRU_BLOCK = D_GROUP // LRU_BLOCKS
LRU_C = 8.0
N_EXPERT_GROUPS = 4
EXPERTS_PER_GROUP = 4
N_EXPERTS = N_EXPERT_GROUPS * EXPERTS_PER_GROUP
D_FF_EXPERT = D_MODEL // 8
IN_SIZES = (D_GROUP, SSD_CONV_DIM, SSD_HEADS,
            GDN_CONV_DIM, GDN_HEADS, GDN_HEADS, D_GROUP,
            3 * D_GROUP, MLSTM_HEADS, MLSTM_HEADS, D_GROUP,
            D_GROUP, D_GROUP)
IN_OFFSETS = tuple(int(o) for o in np.cumsum((0,) + IN_SIZES))

N_PROMPT_TOK = BATCH * SEQ
N_SAMPLE_TOK = DEC_SEQ * DEC_BATCH
N_TOK = N_PROMPT_TOK + N_SAMPLE_TOK

LANES = 128
VMEM_LIMIT_BYTES = 56 * 1024 * 1024

ROW_TILE = 128
N_ROW_TILES = N_TOK // ROW_TILE
PROMPT_TILES_PER_SEQ = SEQ // ROW_TILE
MM_TM = 1024
MM_TN = 512
ADA_ROWS = 144
ADA_TN = 512
MOE_TM = 512
MOE_FF_TILE = 256
MOE_FF_STEPS = D_FF_EXPERT // MOE_FF_TILE
MOE_STEPS_PER_TILE = EXPERTS_PER_GROUP * MOE_FF_STEPS
MOE_MAX_TILES = N_TOK // MOE_TM + N_EXPERT_GROUPS
MOE_SLOTS = MOE_MAX_TILES * MOE_TM
ROUTER_LANES = LANES
EXPERT_LANE0 = N_EXPERT_GROUPS

_WIDE_SEGS = (0, 1, 3, 6, 7, 10, 11, 12)
_NARROW_SEGS = (2, 4, 5, 8, 9)
WIDE_COLS = sum(IN_SIZES[s] for s in _WIDE_SEGS)
NARROW_COLS = LANES
_WIDE_OFF = {}
_o = 0
for _s in _WIDE_SEGS:
    _WIDE_OFF[_s] = _o
    _o += IN_SIZES[_s]
_NARROW_OFF = {}
_o = 0
for _s in _NARROW_SEGS:
    _NARROW_OFF[_s] = _o
    _o += IN_SIZES[_s]


def _cparams(*sem):
    return pltpu.CompilerParams(dimension_semantics=sem, vmem_limit_bytes=VMEM_LIMIT_BYTES)


def _mm_kernel(a_ref, w_ref, o_ref):
    o_ref[...] = jnp.dot(a_ref[...], w_ref[...], preferred_element_type=F32).astype(o_ref.dtype)


def _matmul(a, w, tn, out_dtype=F32, name="matmul"):
    m, k = a.shape
    n = w.shape[1]
    assert m % MM_TM == 0 and n % tn == 0
    return pl.pallas_call(
        _mm_kernel,
        grid=(m // MM_TM, n // tn),
        in_specs=[pl.BlockSpec((MM_TM, k), lambda i, j: (i, 0)),
                  pl.BlockSpec((k, tn), lambda i, j: (0, j))],
        out_specs=pl.BlockSpec((MM_TM, tn), lambda i, j: (i, j)),
        out_shape=jax.ShapeDtypeStruct((m, n), out_dtype),
        compiler_params=_cparams("parallel", "arbitrary"),
        name=name,
    )(a, w)


def _ada_kernel(c_ref, w_ref, o_ref):
    c = c_ref[...]
    a = (c * jax.nn.sigmoid(c)).astype(BF16)
    o_ref[...] = jnp.dot(a, w_ref[...].astype(BF16), preferred_element_type=F32)


def _ada_matmul(c_pad, w_ada):
    k, n = w_ada.shape
    return pl.pallas_call(
        _ada_kernel,
        grid=(n // ADA_TN,),
        in_specs=[pl.BlockSpec((ADA_ROWS, k), lambda j: (0, 0)),
                  pl.BlockSpec((k, ADA_TN), lambda j: (0, j))],
        out_specs=pl.BlockSpec((ADA_ROWS, ADA_TN), lambda j: (0, j)),
        out_shape=jax.ShapeDtypeStruct((ADA_ROWS, n), F32),
        compiler_params=_cparams("arbitrary"),
        name="adaln_matmul",
    )(c_pad, w_ada)


def _route(lg):
    lane = lax.broadcasted_iota(jnp.int32, lg.shape, 1).astype(F32)
    neg = -jnp.inf
    big = float(ROUTER_LANES)
    gl = jnp.where(lane < N_EXPERT_GROUPS, lg, neg)
    gmax = jnp.max(gl, axis=-1, keepdims=True)
    gidx = jnp.min(jnp.where(gl == gmax, lane, big), axis=-1, keepdims=True)
    g_w = 1.0 / jnp.sum(jnp.exp(gl - gmax), axis=-1, keepdims=True)
    lo = EXPERT_LANE0 + EXPERTS_PER_GROUP * gidx
    el = jnp.where((lane >= lo) & (lane < lo + EXPERTS_PER_GROUP), lg, neg)
    e1 = jnp.max(el, axis=-1, keepdims=True)
    i1 = jnp.min(jnp.where(el == e1, lane, big), axis=-1, keepdims=True)
    el2 = jnp.where(lane == i1, neg, el)
    e2 = jnp.max(el2, axis=-1, keepdims=True)
    i2 = jnp.min(jnp.where(el2 == e2, lane, big), axis=-1, keepdims=True)
    t = jnp.exp(e2 - e1)
    w1 = g_w / (1.0 + t)
    w2 = g_w * t / (1.0 + t)
    out = jnp.where(lane == i1, w1, 0.0) + jnp.where(lane == i2, w2, 0.0)
    return out + jnp.where(lane == 0.0, gidx, 0.0)


def _rowwise_kernel(*refs, has_resid, has_mod, has_router):
    it = iter(refs)
    x_ref = next(it)
    if has_resid:
        y_ref, gate_m, gate_t = next(it), next(it), next(it)
    g_ref = next(it)
    if has_mod:
        sc_m, sc_t, sh_m, sh_t = next(it), next(it), next(it), next(it)
    if has_router:
        wr_ref, br_ref = next(it), next(it)
    if has_resid:
        xo_ref = next(it)
    h_ref = next(it)
    if has_router:
        gt_ref = next(it)

    x = x_ref[...]
    if has_resid:
        x = x + (gate_m[...] + gate_t[...]) * y_ref[...]
        xo_ref[...] = x
    y = x * lax.rsqrt(jnp.mean(x * x, axis=-1, keepdims=True) + EPS)
    h = y * g_ref[...]
    if has_mod:
        h = h * (1.0 + (sc_m[...] + sc_t[...])) + (sh_m[...] + sh_t[...])
    h_ref[...] = h.astype(h_ref.dtype)
    if has_router:
        lg = jnp.dot(h, wr_ref[...], preferred_element_type=F32,
                     precision=lax.Precision.HIGHEST) + br_ref[...]
        gt_ref[...] = _route(lg)


def _mod_tile_index(i):
    return jnp.where(i < N_PROMPT_TOK // ROW_TILE, i // PROMPT_TILES_PER_SEQ, BATCH)


def _rowwise(x, g_row, *, y=None, mod_tiles=None, ada_rows=None, layer=0, gate_chunk=None, gate_layer=None,
             mod_chunks=None, router=None, h_dtype=BF16, name="rowwise"):
    has_resid = y is not None
    has_mod = mod_chunks is not None
    has_router = router is not None
    d = D_MODEL
    row_spec = pl.BlockSpec((ROW_TILE, d), lambda i: (i, 0))
    vec_spec = pl.BlockSpec((1, d), lambda i: (0, 0))

    def mod_specs(chunk, lyr):
        return [pl.BlockSpec((None, ROW_TILE, d), lambda i: (_mod_tile_index(i), 0, chunk)),
                pl.BlockSpec((None, 1, d), lambda i: (lyr * 6 + chunk, 0, 0))]

    args, specs = [x], [row_spec]
    if has_resid:
        args += [y, mod_tiles, ada_rows]
        specs += [row_spec] + mod_specs(gate_chunk, layer if gate_layer is None else gate_layer)
    args.append(g_row)
    specs.append(vec_spec)
    if has_mod:
        sc_chunk, sh_chunk = mod_chunks
        args += [mod_tiles, ada_rows, mod_tiles, ada_rows]
        specs += mod_specs(sc_chunk, layer) + mod_specs(sh_chunk, layer)
    if has_router:
        w_r, b_r = router
        args += [w_r, b_r]
        specs += [pl.BlockSpec((d, ROUTER_LANES), lambda i: (0, 0)),
                  pl.BlockSpec((1, ROUTER_LANES), lambda i: (0, 0))]
    out_shape, out_specs = [], []
    if has_resid:
        out_shape.append(jax.ShapeDtypeStruct((N_TOK, d), F32))
        out_specs.append(row_spec)
    out_shape.append(jax.ShapeDtypeStruct((N_TOK, d), h_dtype))
    out_specs.append(row_spec)
    if has_router:
        out_shape.append(jax.ShapeDtypeStruct((N_TOK, ROUTER_LANES), F32))
        out_specs.append(pl.BlockSpec((ROW_TILE, ROUTER_LANES), lambda i: (i, 0)))
    return pl.pallas_call(
        functools.partial(_rowwise_kernel, has_resid=has_resid, has_mod=has_mod, has_router=has_router),
        grid=(N_ROW_TILES,),
        in_specs=specs,
        out_specs=out_specs,
        out_shape=out_shape,
        compiler_params=_cparams("parallel"),
        name=name,
    )(*args)


def _moe_kernel(tg_ref, nu_ref, x_ref, gt_ref, wg_ref, wu_ref, wd_ref, o_ref):
    i = pl.program_id(0)
    j = pl.program_id(1)
    used = i < nu_ref[0]

    @pl.when(jnp.logical_and(j == 0, jnp.logical_not(used)))
    def _():
        o_ref[...] = jnp.zeros_like(o_ref)

    @pl.when(used)
    def _():
        x = x_ref[...]
        a = jnp.dot(x, wg_ref[...], preferred_element_type=F32)
        b = jnp.dot(x, wu_ref[...], preferred_element_type=F32)
        he = (a * jax.nn.sigmoid(a)) * b
        gates = gt_ref[...]
        lane = lax.broadcasted_iota(jnp.int32, gates.shape, 1)
        col = EXPERT_LANE0 + EXPERTS_PER_GROUP * tg_ref[i] + j // MOE_FF_STEPS
        gcol = jnp.sum(jnp.where(lane == col, gates, 0.0), axis=-1, keepdims=True)
        contrib = gcol * jnp.dot(he.astype(BF16), wd_ref[...], preferred_element_type=F32)

        @pl.when(j == 0)
        def _():
            o_ref[...] = contrib

        @pl.when(j > 0)
        def _():
            o_ref[...] += contrib


def _moe(x_sorted, gates_sorted, tile_group, n_used, w_gate, w_up, w_down, layer):
    d = D_MODEL

    def expert_of(i, j, tg, nu):
        jj = jnp.where(i < nu[0], j, MOE_STEPS_PER_TILE - 1)
        return tg[i] * EXPERTS_PER_GROUP + jj // MOE_FF_STEPS, jj % MOE_FF_STEPS

    def wgu_map(i, j, tg, nu):
        e, f = expert_of(i, j, tg, nu)
        return (layer, e, 0, f)

    def wd_map(i, j, tg, nu):
        e, f = expert_of(i, j, tg, nu)
        return (layer, e, f, 0)

    grid_spec = pltpu.PrefetchScalarGridSpec(
        num_scalar_prefetch=2,
        grid=(MOE_MAX_TILES, MOE_STEPS_PER_TILE),
        in_specs=[pl.BlockSpec((MOE_TM, d), lambda i, j, tg, nu: (i, 0)),
                  pl.BlockSpec((MOE_TM, ROUTER_LANES), lambda i, j, tg, nu: (i, 0)),
                  pl.BlockSpec((None, None, d, MOE_FF_TILE), wgu_map),
                  pl.BlockSpec((None, None, d, MOE_FF_TILE), wgu_map),
                  pl.BlockSpec((None, None, MOE_FF_TILE, d), wd_map)],
        out_specs=pl.BlockSpec((MOE_TM, d), lambda i, j, tg, nu: (i, 0)),
    )
    return pl.pallas_call(
        _moe_kernel,
        grid_spec=grid_spec,
        out_shape=jax.ShapeDtypeStruct((MOE_SLOTS, d), F32),
        compiler_params=_cparams("arbitrary", "arbitrary"),
        name="moe_experts",
    )(tile_group, n_used, x_sorted, gates_sorted, w_gate, w_up, w_down)


def _moe_plan(gates):
    gid = gates[:, 0].astype(jnp.int32)
    onehot = (gid[:, None] == jnp.arange(N_EXPERT_GROUPS, dtype=jnp.int32)[None, :]).astype(jnp.int32)
    rank = jnp.take_along_axis(jnp.cumsum(onehot, axis=0), gid[:, None], axis=1)[:, 0] - 1
    counts = jnp.sum(onehot, axis=0)
    tiles = (counts + MOE_TM - 1) // MOE_TM
    tile_end = jnp.cumsum(tiles)
    tile_start = tile_end - tiles
    dest = tile_start[gid] * MOE_TM + rank
    src = jnp.zeros((MOE_SLOTS,), jnp.int32).at[dest].set(jnp.arange(N_TOK, dtype=jnp.int32))
    valid = jnp.zeros((MOE_SLOTS,), F32).at[dest].set(1.0)
    n_used = tile_end[-1]
    t = jnp.arange(MOE_MAX_TILES, dtype=jnp.int32)
    tile_group = jnp.sum((t[:, None] >= tile_end[None, :]).astype(jnp.int32), axis=1)
    last_group = jnp.sum((n_used - 1 >= tile_end).astype(jnp.int32))
    tile_group = jnp.where(t < n_used, tile_group, last_group).astype(jnp.int32)
    return src, valid, dest, tile_group, n_used.reshape(1).astype(jnp.int32)


def _chunk_size(L):
    return L if L <= CHUNK else math.gcd(L, CHUNK)


def _group_rmsnorm(y, g, n_groups):
    shp = y.shape
    yg = y.reshape(shp[:-1] + (n_groups, shp[-1] // n_groups))
    yg = yg * lax.rsqrt(jnp.mean(yg * yg, axis=-1, keepdims=True) + EPS)
    return yg.reshape(shp) * g


def _l2norm(x):
    return x * lax.rsqrt(jnp.sum(x * x, axis=-1, keepdims=True) + EPS)


def _causal_conv(u, buf, w, b):
    L = u.shape[1]
    full = jnp.concatenate([buf, u], axis=1)
    out = b + full[:, 0:L] * w[0]
    for j in range(1, CONV_W):
        out = out + full[:, j:j + L] * w[j]
    return out, full[:, L:]


def _to_chunks(t, cs):
    b, L = t.shape[:2]
    t = t.reshape((b, L // cs, cs) + t.shape[2:])
    return jnp.swapaxes(t, 2, 3)


def _from_chunks(t):
    t = jnp.swapaxes(t, 2, 3)
    return t.reshape((t.shape[0], t.shape[1] * t.shape[2]) + t.shape[3:])


def _ssd_chunked(x, dt, A, Bm, Cm, S0):
    Bsz, L, H, P = x.shape
    N = Bm.shape[-1]
    cs = _chunk_size(L)
    nc = L // cs
    xc = x.reshape(Bsz, nc, cs, H, P)
    dtc = dt.reshape(Bsz, nc, cs, H)
    Bc = Bm.reshape(Bsz, nc, cs, H, N)
    Cc = Cm.reshape(Bsz, nc, cs, H, N)
    cum = jnp.cumsum(dtc * A, axis=2)
    causal = jnp.tril(jnp.ones((cs, cs), dtype=bool))[:, :, None]
    seg = jnp.where(causal, cum[:, :, :, None, :] - cum[:, :, None, :, :], -jnp.inf)
    scores = jnp.einsum('bcthn,bcshn->bctsh', Cc, Bc) * jnp.exp(seg)
    xdt = xc * dtc[..., None]
    y_intra = jnp.einsum('bctsh,bcshp->bcthp', scores, xdt)
    w_end = jnp.exp(cum[:, :, -1:, :] - cum)
    chunk_S = jnp.einsum('bcsh,bcshp,bcshn->bchpn', w_end, xdt, Bc)
    chunk_decay = jnp.exp(cum[:, :, -1, :])

    def step(S, inp):
        dS, dec = inp
        return S * dec[..., None, None] + dS, S

    S_fin, S_start = lax.scan(step, S0, (jnp.swapaxes(chunk_S, 0, 1), jnp.swapaxes(chunk_decay, 0, 1)))
    y_inter = jnp.einsum('bcthn,cbhpn->bcthp', Cc * jnp.exp(cum)[..., None], S_start)
    return (y_intra + y_inter).reshape(Bsz, L, H, P), S_fin


def _ssd_mixer(z, xbc_raw, dt_raw, conv_buf, S0, conv_w, conv_b, dt_bias, a_log, d_skip, norm_g):
    Bsz, L, _ = z.shape
    xbc, buf = _causal_conv(xbc_raw, conv_buf, conv_w, conv_b)
    xbc = jax.nn.silu(xbc)
    xs, Bm, Cm = jnp.split(xbc, [D_GROUP, D_GROUP + SSD_NGROUPS * SSD_STATE], axis=-1)
    xs = xs.reshape(Bsz, L, SSD_HEADS, SSD_HEAD_DIM)
    rep = SSD_HEADS // SSD_NGROUPS
    Bm = jnp.repeat(Bm.reshape(Bsz, L, SSD_NGROUPS, SSD_STATE), rep, axis=2)
    Cm = jnp.repeat(Cm.reshape(Bsz, L, SSD_NGROUPS, SSD_STATE), rep, axis=2)
    dt = jax.nn.softplus(dt_raw + dt_bias)
    A = -jnp.exp(a_log)
    y, S = _ssd_chunked(xs, dt, A, Bm, Cm, S0)
    y = (y + d_skip[:, None] * xs).reshape(Bsz, L, D_GROUP)
    y = _group_rmsnorm(y * jax.nn.silu(z), norm_g, SSD_NGROUPS)
    return y, buf, S


def _gdn_chunked(q, k, v, g, beta, S0):
    cs = _chunk_size(q.shape[1])
    qc, kc, vc = _to_chunks(q, cs), _to_chunks(k, cs), _to_chunks(v, cs)
    gc = jnp.cumsum(_to_chunks(g, cs), axis=-1)
    bc = _to_chunks(beta, cs)
    incl = jnp.tril(jnp.ones((cs, cs), dtype=bool))
    strict = jnp.tril(jnp.ones((cs, cs), dtype=bool), -1)
    decay = jnp.exp(jnp.where(incl, gc[..., :, None] - gc[..., None, :], -jnp.inf))
    kk = jnp.einsum('bnhtk,bnhsk->bnhts', kc, kc)
    a_mat = jnp.where(strict, bc[..., :, None] * kk * decay, 0.0) + jnp.eye(cs, dtype=F32)
    rhs = jnp.concatenate([vc * bc[..., None], kc * (bc * jnp.exp(gc))[..., None]], axis=-1)
    sol = lax.linalg.triangular_solve(a_mat, rhs, left_side=True, lower=True, unit_diagonal=True)
    dv = v.shape[-1]
    u, w = sol[..., :dv], sol[..., dv:]
    qk = jnp.einsum('bnhtk,bnhsk->bnhts', qc, kc) * decay
    q_dec = qc * jnp.exp(gc)[..., None]
    k_dec = kc * jnp.exp(gc[..., -1:] - gc)[..., None]
    g_last = jnp.exp(gc[..., -1])

    def step(S, inp):
        u_i, w_i, qk_i, qd_i, kd_i, gl_i = inp
        v_new = u_i - jnp.einsum('bhtk,bhkv->bhtv', w_i, S)
        o = jnp.einsum('bhtk,bhkv->bhtv', qd_i, S) + jnp.einsum('bhts,bhsv->bhtv', qk_i, v_new)
        S = S * gl_i[..., None, None] + jnp.einsum('bhsk,bhsv->bhkv', kd_i, v_new)
        return S, o

    xs = tuple(jnp.swapaxes(t, 0, 1) for t in (u, w, qk, q_dec, k_dec, g_last))
    S_fin, o = lax.scan(step, S0, xs)
    return _from_chunks(jnp.swapaxes(o, 0, 1)), S_fin


def _gdn_mixer(qkv_raw, a_raw, b_raw, z, conv_buf, S0, conv_w, conv_b, dt_bias, a_log, norm_g):
    Bsz, L, _ = z.shape
    qkv, buf = _causal_conv(qkv_raw, conv_buf, conv_w, conv_b)
    q, k, v = jnp.split(jax.nn.silu(qkv), 3, axis=-1)
    shp = (Bsz, L, GDN_HEADS, GDN_HEAD_DIM)
    q = _l2norm(q.reshape(shp)) * (GDN_HEAD_DIM ** -0.5)
    k = _l2norm(k.reshape(shp))
    v = v.reshape(shp)
    g = -jnp.exp(a_log) * jax.nn.softplus(a_raw + dt_bias)
    beta = jax.nn.sigmoid(b_raw)
    o, S = _gdn_chunked(q, k, v, g, beta, S0)
    o = _group_rmsnorm(o.reshape(Bsz, L, D_GROUP), norm_g, GDN_HEADS) * jax.nn.silu(z)
    return o, buf, S


def _mlstm_chunked(q, k, v, ig, lf, C0, n0, m0):
    cs = _chunk_size(q.shape[1])
    qc, kc, vc = _to_chunks(q, cs), _to_chunks(k, cs), _to_chunks(v, cs)
    igc = _to_chunks(ig, cs)
    F = jnp.cumsum(_to_chunks(lf, cs), axis=-1)
    incl = jnp.tril(jnp.ones((cs, cs), dtype=bool))
    d_log = jnp.where(incl, F[..., :, None] - F[..., None, :] + igc[..., None, :], -jnp.inf)
    d_max = jnp.max(d_log, axis=-1)
    qk = jnp.einsum('bnhtk,bnhsk->bnhts', qc, kc)
    d_end = F[..., -1:] - F + igc

    def step(carry, inp):
        C, n, m = carry
        q_i, k_i, v_i, F_i, dl_i, dm_i, qk_i, de_i = inp
        m_t = jnp.maximum(F_i + m[..., None], dm_i)
        w_carry = jnp.exp(F_i + m[..., None] - m_t)
        P = jnp.exp(dl_i - m_t[..., None]) * qk_i
        num = w_carry[..., None] * jnp.einsum('bhtk,bhkv->bhtv', q_i, C) + jnp.einsum('bhts,bhsv->bhtv', P, v_i)
        den = w_carry * jnp.einsum('bhtk,bhk->bht', q_i, n) + jnp.sum(P, axis=-1)
        h = num / jnp.maximum(jnp.abs(den), jnp.exp(-m_t))[..., None]
        m_end = m_t[..., -1]
        w_prev = jnp.exp(F_i[..., -1] + m - m_end)
        kw = k_i * jnp.exp(de_i - m_end[..., None])[..., None]
        C = w_prev[..., None, None] * C + jnp.einsum('bhsk,bhsv->bhkv', kw, v_i)
        n = w_prev[..., None] * n + jnp.sum(kw, axis=2)
        return (C, n, m_end), h

    xs = tuple(jnp.swapaxes(t, 0, 1) for t in (qc, kc, vc, F, d_log, d_max, qk, d_end))
    (C, n, m), h = lax.scan(step, (C0, n0, m0), xs)
    return _from_chunks(jnp.swapaxes(h, 0, 1)), C, n, m


def _mlstm_mixer(qkv, i_raw, f_raw, o_raw, C0, n0, m0, i_bias, f_bias, norm_g):
    Bsz, L, _ = qkv.shape
    shp = (Bsz, L, MLSTM_HEADS, MLSTM_HEAD_DIM)
    q, k, v = jnp.split(qkv, 3, axis=-1)
    q = q.reshape(shp)
    k = k.reshape(shp) * (MLSTM_HEAD_DIM ** -0.5)
    v = v.reshape(shp)
    ig = i_raw + i_bias
    lf = jax.nn.log_sigmoid(f_raw + f_bias)
    h, C, n, m = _mlstm_chunked(q, k, v, ig, lf, C0, n0, m0)
    y = _group_rmsnorm(h.reshape(Bsz, L, D_GROUP), norm_g, MLSTM_HEADS) * jax.nn.sigmoid(o_raw)
    return y, C, n, m


def _lin_combine(left, right):
    a_l, b_l = left
    a_r, b_r = right
    return a_l * a_r, a_r * b_l + b_r


def _lru_mixer(x_raw, gate_raw, conv_buf, h0, conv_w, conv_b, w_a, b_a, w_x, b_x, lam):
    Bsz, L, _ = x_raw.shape
    xc, buf = _causal_conv(x_raw, conv_buf, conv_w, conv_b)
    xb = xc.reshape(Bsz, L, LRU_BLOCKS, LRU_BLOCK)
    r = jax.nn.sigmoid(jnp.einsum('blgi,gij->blgj', xb, w_a) + b_a).reshape(Bsz, L, D_GROUP)
    i = jax.nn.sigmoid(jnp.einsum('blgi,gij->blgj', xb, w_x) + b_x).reshape(Bsz, L, D_GROUP)
    log_a = -LRU_C * r * jax.nn.softplus(-lam)
    a = jnp.exp(log_a)
    u = jnp.sqrt(-jnp.expm1(2.0 * log_a)) * (i * xc)
    u = u.at[:, 0].add(a[:, 0] * h0)
    _, h = lax.associative_scan(_lin_combine, (a, u), axis=1)
    y = h * jax.nn.gelu(gate_raw)
    return y, buf, h[:, -1]


def _mixers(wide, narrow, st, p, l):
    def seg(s):
        if s in _WIDE_OFF:
            return wide[..., _WIDE_OFF[s]:_WIDE_OFF[s] + IN_SIZES[s]]
        return narrow[..., _NARROW_OFF[s]:_NARROW_OFF[s] + IN_SIZES[s]]

    (ssd_z, ssd_xbc, ssd_dt, gdn_qkv, gdn_a, gdn_b, gdn_z,
     ml_qkv, ml_i, ml_f, ml_o, lru_x, lru_gate) = [seg(s) for s in range(len(IN_SIZES))]
    s_ssd, s_ssd_conv, s_gdn, s_gdn_conv, s_mc, s_mn, s_mm, s_lru, s_lru_conv = st
    y_a, ssd_conv_new, ssd_new = _ssd_mixer(ssd_z, ssd_xbc, ssd_dt, s_ssd_conv, s_ssd,
                                            p['ssd_conv_w'][l], p['ssd_conv_b'][l], p['ssd_dt_bias'][l],
                                            p['ssd_a_log'][l], p['ssd_d'][l], p['ssd_norm_g'][l])
    y_b, gdn_conv_new, gdn_new = _gdn_mixer(gdn_qkv, gdn_a, gdn_b, gdn_z, s_gdn_conv, s_gdn,
                                            p['gdn_conv_w'][l], p['gdn_conv_b'][l], p['gdn_dt_bias'][l],
                                            p['gdn_a_log'][l], p['gdn_norm_g'][l])
    y_c, mc_new, mn_new, mm_new = _mlstm_mixer(ml_qkv, ml_i, ml_f, ml_o, s_mc, s_mn, s_mm,
                                               p['mlstm_i_bias'][l], p['mlstm_f_bias'][l], p['mlstm_norm_g'][l])
    y_d, lru_conv_new, lru_new = _lru_mixer(lru_x, lru_gate, s_lru_conv, s_lru,
                                            p['lru_conv_w'][l], p['lru_conv_b'][l], p['lru_w_a'][l],
                                            p['lru_b_a'][l], p['lru_w_x'][l], p['lru_b_x'][l], p['lru_lambda'][l])
    y = jnp.concatenate([y_a, y_b, y_c, y_d], axis=-1)
    return y, (ssd_new, ssd_conv_new, gdn_new, gdn_conv_new, mc_new, mn_new, mm_new, lru_new, lru_conv_new)


def _state_shapes(n):
    return ((n, SSD_HEADS, SSD_HEAD_DIM, SSD_STATE),
            (n, CONV_W - 1, SSD_CONV_DIM),
            (n, GDN_HEADS, GDN_HEAD_DIM, GDN_HEAD_DIM),
            (n, CONV_W - 1, GDN_CONV_DIM),
            (n, MLSTM_HEADS, MLSTM_HEAD_DIM, MLSTM_HEAD_DIM),
            (n, MLSTM_HEADS, MLSTM_HEAD_DIM),
            (n, MLSTM_HEADS),
            (n, D_GROUP),
            (n, CONV_W - 1, D_GROUP))


def kernel(x_prompt, x_sample, state_ssd, state_ssd_conv, state_gdn, state_gdn_conv, state_mlstm_c, state_mlstm_n, state_mlstm_m, state_rglru, state_rglru_conv, c_prompt, c_sample, w_ada, ada_table, norm1_g, norm2_g, final_g, w_in, w_out, ssd_conv_w, ssd_conv_b, ssd_dt_bias, ssd_a_log, ssd_d, ssd_norm_g, gdn_conv_w, gdn_conv_b, gdn_dt_bias, gdn_a_log, gdn_norm_g, mlstm_i_bias, mlstm_f_bias, mlstm_norm_g, lru_conv_w, lru_conv_b, lru_w_a, lru_b_a, lru_w_x, lru_b_x, lru_lambda, moe_w_group, moe_b_group, moe_w_expert, moe_b_expert, moe_w_gate, moe_w_up, moe_w_down):
    p = dict(ssd_conv_w=ssd_conv_w, ssd_conv_b=ssd_conv_b, ssd_dt_bias=ssd_dt_bias, ssd_a_log=ssd_a_log,
             ssd_d=ssd_d, ssd_norm_g=ssd_norm_g,
             gdn_conv_w=gdn_conv_w, gdn_conv_b=gdn_conv_b, gdn_dt_bias=gdn_dt_bias, gdn_a_log=gdn_a_log,
             gdn_norm_g=gdn_norm_g,
             mlstm_i_bias=mlstm_i_bias, mlstm_f_bias=mlstm_f_bias, mlstm_norm_g=mlstm_norm_g,
             lru_conv_w=lru_conv_w, lru_conv_b=lru_conv_b, lru_w_a=lru_w_a, lru_b_a=lru_b_a,
             lru_w_x=lru_w_x, lru_b_x=lru_b_x, lru_lambda=lru_lambda)
    d = D_MODEL

    w_in_wide = jnp.concatenate(
        [w_in[:, :, IN_OFFSETS[s]:IN_OFFSETS[s + 1]] for s in _WIDE_SEGS], axis=-1).astype(BF16)
    w_in_narrow = jnp.concatenate(
        [w_in[:, :, IN_OFFSETS[s]:IN_OFFSETS[s + 1]] for s in _NARROW_SEGS]
        + [jnp.zeros((DEPTH, d, NARROW_COLS - sum(IN_SIZES[s] for s in _NARROW_SEGS)), w_in.dtype)],
        axis=-1).astype(BF16)
    w_out_b = w_out.astype(BF16)
    w_gate_b = moe_w_gate.astype(BF16)
    w_up_b = moe_w_up.astype(BF16)
    w_down_b = moe_w_down.astype(BF16)
    router_pad = jnp.zeros((DEPTH, d, ROUTER_LANES - N_EXPERT_GROUPS - N_EXPERTS), F32)
    w_router = jnp.concatenate([moe_w_group, moe_w_expert, router_pad], axis=-1)
    b_router = jnp.concatenate([moe_b_group, moe_b_expert, router_pad[:, 0, :]], axis=-1)[:, None, :]

    c_pad = jnp.concatenate([c_prompt, c_sample, jnp.zeros((ADA_ROWS - BATCH - DEC_BATCH, d), F32)], axis=0)
    mod_shared = _ada_matmul(c_pad, w_ada)
    mod_tiles = jnp.concatenate(
        [jnp.broadcast_to(mod_shared[:BATCH, None, :], (BATCH, ROW_TILE, 6 * d)),
         mod_shared[None, BATCH:BATCH + DEC_BATCH, :]], axis=0)
    ada_rows = ada_table.reshape(DEPTH * 6, 1, d)

    x = jnp.concatenate([x_prompt.reshape(N_PROMPT_TOK, d),
                         jnp.swapaxes(x_sample, 0, 1).reshape(N_SAMPLE_TOK, d)], axis=0)

    st_sample = (state_ssd, state_ssd_conv, state_gdn, state_gdn_conv, state_mlstm_c,
                 state_mlstm_n, state_mlstm_m, state_rglru, state_rglru_conv)
    new_p = [[] for _ in st_sample]
    new_s = [[] for _ in st_sample]

    y_prev, gate_prev = None, None
    for l in range(DEPTH):
        if l == 0:
            (h,) = _rowwise(x, norm1_g[l][None, :], mod_tiles=mod_tiles, ada_rows=ada_rows, layer=l,
                            mod_chunks=(1, 0), name="norm1")
        else:
            x, h = _rowwise(x, norm1_g[l][None, :], y=y_prev, mod_tiles=mod_tiles, ada_rows=ada_rows,
                            layer=l, gate_chunk=5, gate_layer=l - 1, mod_chunks=(1, 0), name="resid_norm1")
        wide = _matmul(h, w_in_wide[l], MM_TN, name="in_proj")
        narrow = _matmul(h, w_in_narrow[l], NARROW_COLS, name="in_proj_gates")

        zero_states = tuple(jnp.zeros(s, F32) for s in _state_shapes(BATCH))
        y_p, st_p = _mixers(wide[:N_PROMPT_TOK].reshape(BATCH, SEQ, WIDE_COLS),
                            narrow[:N_PROMPT_TOK].reshape(BATCH, SEQ, NARROW_COLS), zero_states, p, l)
        y_s, st_s = _mixers(jnp.swapaxes(wide[N_PROMPT_TOK:].reshape(DEC_SEQ, DEC_BATCH, WIDE_COLS), 0, 1),
                            jnp.swapaxes(narrow[N_PROMPT_TOK:].reshape(DEC_SEQ, DEC_BATCH, NARROW_COLS), 0, 1),
                            tuple(s[l] for s in st_sample), p, l)
        for acc, s in zip(new_p, st_p):
            acc.append(s)
        for acc, s in zip(new_s, st_s):
            acc.append(s)
        y_cat = jnp.concatenate([y_p.reshape(N_PROMPT_TOK, d),
                                 jnp.swapaxes(y_s, 0, 1).reshape(N_SAMPLE_TOK, d)], axis=0).astype(BF16)
        y_mix = _matmul(y_cat, w_out_b[l], MM_TN, name="out_proj")

        x, h2, gates = _rowwise(x, norm2_g[l][None, :], y=y_mix, mod_tiles=mod_tiles, ada_rows=ada_rows,
                                layer=l, gate_chunk=2, mod_chunks=(4, 3),
                                router=(w_router[l], b_router[l]), name="resid_norm2_router")
        src, valid, dest, tile_group, n_used = _moe_plan(gates)
        x_sorted = jnp.take(h2, src, axis=0)
        gates_sorted = jnp.take(gates, src, axis=0) * valid[:, None]
        y_sorted = _moe(x_sorted, gates_sorted, tile_group, n_used, w_gate_b, w_up_b, w_down_b, l)
        y_prev = jnp.take(y_sorted, dest, axis=0)

    x, y_fin = _rowwise(x, final_g[None, :], y=y_prev, mod_tiles=mod_tiles, ada_rows=ada_rows,
                        layer=DEPTH - 1, gate_chunk=5, h_dtype=F32, name="resid_final_norm")
    y_prompt = y_fin[:N_PROMPT_TOK].reshape(BATCH, SEQ, d)
    y_sample = jnp.swapaxes(y_fin[N_PROMPT_TOK:].reshape(DEC_SEQ, DEC_BATCH, d), 0, 1)

    outs = [y_prompt, y_sample]
    for acc_p, acc_s in zip(new_p, new_s):
        outs.append(jnp.stack(acc_p))
        outs.append(jnp.stack(acc_s))
    return tuple(outs)
```

```python
import functools

import numpy as np
import jax
import jax.numpy as jnp
from jax import lax
from jax.experimental import pallas as pl
from jax.experimental.pallas import tpu as pltpu

F32 = jnp.float32
BF16 = jnp.bfloat16

D_MODEL = 4096
BATCH = 4
SEQ = 2048
DEPTH = 4
DEC_BATCH = 128
DEC_SEQ = 8
EPS = 1e-6
N_MIXERS = 4
D_GROUP = D_MODEL // N_MIXERS
CONV_W = 4
CHUNK = 64
SSD_HEAD_DIM = 64
SSD_HEADS = D_GROUP // SSD_HEAD_DIM
SSD_NGROUPS = 2
SSD_STATE = 128
SSD_BC_DIM = 2 * SSD_NGROUPS * SSD_STATE
SSD_CONV_DIM = D_GROUP + SSD_BC_DIM
GDN_HEAD_DIM = 128
GDN_HEADS = D_GROUP // GDN_HEAD_DIM
GDN_CONV_DIM = 3 * D_GROUP
MLSTM_HEAD_DIM = 128
MLSTM_HEADS = D_GROUP // MLSTM_HEAD_DIM
LRU_BLOCKS = 8
LRU_BLOCK = D_GROUP // LRU_BLOCKS
LRU_C = 8.0
N_EXPERT_GROUPS = 4
EXPERTS_PER_GROUP = 4
N_EXPERTS = N_EXPERT_GROUPS * EXPERTS_PER_GROUP
D_FF_EXPERT = D_MODEL // 8
IN_SIZES = (D_GROUP, SSD_CONV_DIM, SSD_HEADS,
            GDN_CONV_DIM, GDN_HEADS, GDN_HEADS, D_GROUP,
            3 * D_GROUP, MLSTM_HEADS, MLSTM_HEADS, D_GROUP,
            D_GROUP, D_GROUP)
IN_OFFSETS = tuple(int(o) for o in np.cumsum((0,) + IN_SIZES))
(SEG_SSD_Z, SEG_SSD_XBC, SEG_SSD_DT, SEG_GDN_QKV, SEG_GDN_A, SEG_GDN_B, SEG_GDN_Z,
 SEG_ML_QKV, SEG_ML_I, SEG_ML_F, SEG_ML_O, SEG_LRU_X, SEG_LRU_GATE) = range(len(IN_SIZES))

N_PROMPT_TOK = BATCH * SEQ
N_SAMPLE_TOK = DEC_BATCH * DEC_SEQ
N_TOK = N_PROMPT_TOK + N_SAMPLE_TOK

LANES = 128
SUBLANES = 8
VMEM_LIMIT_BYTES = 56 * 1024 * 1024

ROW_TILE = 128
N_ROW_TILES = N_TOK // ROW_TILE
PROMPT_TILES_PER_SEQ = SEQ // ROW_TILE
PROMPT_ROW_TILES = N_PROMPT_TOK // ROW_TILE
SAMPLE_ROW_TILES = N_SAMPLE_TOK // ROW_TILE
MM_TM = 1024
MM_TN = 512
ADA_ROWS = 144
ADA_TN = 512
MOE_TM = 512
MOE_FF_TILE = 256
MOE_FF_STEPS = D_FF_EXPERT // MOE_FF_TILE
MOE_STEPS_PER_TILE = EXPERTS_PER_GROUP * MOE_FF_STEPS
MOE_MAX_TILES = N_TOK // MOE_TM + N_EXPERT_GROUPS
MOE_SLOTS = MOE_MAX_TILES * MOE_TM
ROUTER_LANES = LANES
EXPERT_LANE0 = N_EXPERT_GROUPS

_WIDE_SEGS = (SEG_GDN_QKV, SEG_ML_QKV, SEG_SSD_Z, SEG_GDN_Z, SEG_ML_O, SEG_LRU_X, SEG_LRU_GATE, SEG_SSD_XBC)
_NARROW_SEGS = (SEG_SSD_DT, SEG_GDN_A, SEG_GDN_B, SEG_ML_I, SEG_ML_F)
WIDE_OFF = {}
_o = 0
for _s in _WIDE_SEGS:
    WIDE_OFF[_s] = _o
    _o += IN_SIZES[_s]
WIDE_COLS = _o
NARROW_OFF = {}
_o = 0
for _s in _NARROW_SEGS:
    NARROW_OFF[_s] = _o
    _o += IN_SIZES[_s]
NARROW_COLS = LANES
LANE_SSD_DT = NARROW_OFF[SEG_SSD_DT]
LANE_GDN_A = NARROW_OFF[SEG_GDN_A]
LANE_GDN_B = NARROW_OFF[SEG_GDN_B]
LANE_ML_I = NARROW_OFF[SEG_ML_I]
LANE_ML_F = NARROW_OFF[SEG_ML_F]

CONV_PAD = SUBLANES
CONV_PREV = CONV_W - 1


def _cparams(*sem):
    return pltpu.CompilerParams(dimension_semantics=sem, vmem_limit_bytes=VMEM_LIMIT_BYTES)


def _dot(a, b):
    return jnp.dot(a.astype(BF16), b.astype(BF16), preferred_element_type=F32)


def _dot_nt(a, b):
    return lax.dot_general(a.astype(BF16), b.astype(BF16), (((1,), (1,)), ((), ())), preferred_element_type=F32)


def _dot_tn(a, b):
    return lax.dot_general(a.astype(BF16), b.astype(BF16), (((0,), (0,)), ((), ())), preferred_element_type=F32)


def _dot_f32(a, b):
    return jnp.dot(a, b, preferred_element_type=F32, precision=lax.Precision.HIGHEST)


def _silu(x):
    return x * jax.nn.sigmoid(x)


def _mm_kernel(a_ref, w_ref, o_ref):
    o_ref[...] = jnp.dot(a_ref[...], w_ref[...], preferred_element_type=F32).astype(o_ref.dtype)


def _matmul(a, w, tn, name):
    m, k = a.shape
    n = w.shape[1]
    assert m % MM_TM == 0 and n % tn == 0
    return pl.pallas_call(
        _mm_kernel,
        grid=(m // MM_TM, n // tn),
        in_specs=[pl.BlockSpec((MM_TM, k), lambda i, j: (i, 0)),
                  pl.BlockSpec((k, tn), lambda i, j: (0, j))],
        out_specs=pl.BlockSpec((MM_TM, tn), lambda i, j: (i, j)),
        out_shape=jax.ShapeDtypeStruct((m, n), F32),
        compiler_params=_cparams("parallel", "arbitrary"),
        name=name,
    )(a, w)


def _out_proj_kernel(ya_ref, yb_ref, yc_ref, yd_ref, wa_ref, wb_ref, wc_ref, wd_ref, o_ref):
    acc = jnp.dot(ya_ref[...], wa_ref[...], preferred_element_type=F32)
    acc += jnp.dot(yb_ref[...], wb_ref[...], preferred_element_type=F32)
    acc += jnp.dot(yc_ref[...], wc_ref[...], preferred_element_type=F32)
    acc += jnp.dot(yd_ref[...], wd_ref[...], preferred_element_type=F32)
    o_ref[...] = acc


def _out_proj(ys, w):
    n = w.shape[1]
    y_spec = pl.BlockSpec((MM_TM, D_GROUP), lambda i, j: (i, 0))
    w_specs = [pl.BlockSpec((D_GROUP, MM_TN), functools.partial(lambda i, j, m: (m, j), m=m))
               for m in range(N_MIXERS)]
    return pl.pallas_call(
        _out_proj_kernel,
        grid=(N_TOK // MM_TM, n // MM_TN),
        in_specs=[y_spec] * N_MIXERS + w_specs,
        out_specs=pl.BlockSpec((MM_TM, MM_TN), lambda i, j: (i, j)),
        out_shape=jax.ShapeDtypeStruct((N_TOK, n), F32),
        compiler_params=_cparams("parallel", "arbitrary"),
        name="out_proj",
    )(*ys, w, w, w, w)


def _ada_kernel(c_ref, w_ref, o_ref):
    o_ref[...] = _dot(_silu(c_ref[...]), w_ref[...])


def _ada_matmul(c_pad, w_ada):
    k, n = w_ada.shape
    return pl.pallas_call(
        _ada_kernel,
        grid=(n // ADA_TN,),
        in_specs=[pl.BlockSpec((ADA_ROWS, k), lambda j: (0, 0)),
                  pl.BlockSpec((k, ADA_TN), lambda j: (0, j))],
        out_specs=pl.BlockSpec((ADA_ROWS, ADA_TN), lambda j: (0, j)),
        out_shape=jax.ShapeDtypeStruct((ADA_ROWS, n), F32),
        compiler_params=_cparams("arbitrary"),
        name="adaln_matmul",
    )(c_pad, w_ada)


def _route(lg):
    lane = lax.broadcasted_iota(jnp.int32, lg.shape, 1).astype(F32)
    neg = -jnp.inf
    big = float(ROUTER_LANES)
    gl = jnp.where(lane < N_EXPERT_GROUPS, lg, neg)
    gmax = jnp.max(gl, axis=-1, keepdims=True)
    gidx = jnp.min(jnp.where(gl == gmax, lane, big), axis=-1, keepdims=True)
    g_w = 1.0 / jnp.sum(jnp.exp(gl - gmax), axis=-1, keepdims=True)
    lo = EXPERT_LANE0 + EXPERTS_PER_GROUP * gidx
    el = jnp.where((lane >= lo) & (lane < lo + EXPERTS_PER_GROUP), lg, neg)
    e1 = jnp.max(el, axis=-1, keepdims=True)
    i1 = jnp.min(jnp.where(el == e1, lane, big), axis=-1, keepdims=True)
    el2 = jnp.where(lane == i1, neg, el)
    e2 = jnp.max(el2, axis=-1, keepdims=True)
    i2 = jnp.min(jnp.where(el2 == e2, lane, big), axis=-1, keepdims=True)
    t = jnp.exp(e2 - e1)
    w1 = g_w / (1.0 + t)
    w2 = g_w * t / (1.0 + t)
    out = jnp.where(lane == i1, w1, 0.0) + jnp.where(lane == i2, w2, 0.0)
    return out + jnp.where(lane == 0.0, gidx, 0.0)


def _rowwise_kernel(*refs, has_resid, has_mod, has_router):
    it = iter(refs)
    x_ref = next(it)
    if has_resid:
        y_ref, gate_m, gate_t = next(it), next(it), next(it)
    g_ref = next(it)
    if has_mod:
        sc_m, sc_t, sh_m, sh_t = next(it), next(it), next(it), next(it)
    if has_router:
        wr_ref, br_ref = next(it), next(it)
    if has_resid:
        xo_ref = next(it)
    h_ref = next(it)
    if has_router:
        gt_ref = next(it)

    x = x_ref[...]
    if has_resid:
        x = x + (gate_m[...] + gate_t[...]) * y_ref[...]
        xo_ref[...] = x
    y = x * lax.rsqrt(jnp.mean(x * x, axis=-1, keepdims=True) + EPS)
    h = y * g_ref[...]
    if has_mod:
        h = h * (1.0 + (sc_m[...] + sc_t[...])) + (sh_m[...] + sh_t[...])
    h_ref[...] = h.astype(h_ref.dtype)
    if has_router:
        gt_ref[...] = _route(_dot_f32(h, wr_ref[...]) + br_ref[...])


def _mod_tile_index(i):
    return jnp.where(i < PROMPT_ROW_TILES, i // PROMPT_TILES_PER_SEQ, BATCH + i - PROMPT_ROW_TILES)


def _rowwise(x, g_row, *, y=None, mod_tiles=None, ada_rows=None, layer=0, gate_chunk=None, gate_layer=None,
             mod_chunks=None, router=None, h_dtype=BF16, name="rowwise"):
    has_resid = y is not None
    has_mod = mod_chunks is not None
    has_router = router is not None
    d = D_MODEL
    row_spec = pl.BlockSpec((ROW_TILE, d), lambda i: (i, 0))
    vec_spec = pl.BlockSpec((1, d), lambda i: (0, 0))

    def mod_specs(chunk, lyr):
        return [pl.BlockSpec((None, ROW_TILE, d), lambda i: (_mod_tile_index(i), 0, chunk)),
                pl.BlockSpec((None, 1, d), lambda i: (lyr * 6 + chunk, 0, 0))]

    args, specs = [x], [row_spec]
    if has_resid:
        args += [y, mod_tiles, ada_rows]
        specs += [row_spec] + mod_specs(gate_chunk, layer if gate_layer is None else gate_layer)
    args.append(g_row)
    specs.append(vec_spec)
    if has_mod:
        sc_chunk, sh_chunk = mod_chunks
        args += [mod_tiles, ada_rows, mod_tiles, ada_rows]
        specs += mod_specs(sc_chunk, layer) + mod_specs(sh_chunk, layer)
    if has_router:
        w_r, b_r = router
        args += [w_r, b_r]
        specs += [pl.BlockSpec((d, ROUTER_LANES), lambda i: (0, 0)),
                  pl.BlockSpec((1, ROUTER_LANES), lambda i: (0, 0))]
    out_shape, out_specs = [], []
    if has_resid:
        out_shape.append(jax.ShapeDtypeStruct((N_TOK, d), F32))
        out_specs.append(row_spec)
    out_shape.append(jax.ShapeDtypeStruct((N_TOK, d), h_dtype))
    out_specs.append(row_spec)
    if has_router:
        out_shape.append(jax.ShapeDtypeStruct((N_TOK, ROUTER_LANES), F32))
        out_specs.append(pl.BlockSpec((ROW_TILE, ROUTER_LANES), lambda i: (i, 0)))
    return pl.pallas_call(
        functools.partial(_rowwise_kernel, has_resid=has_resid, has_mod=has_mod, has_router=has_router),
        grid=(N_ROW_TILES,),
        in_specs=specs,
        out_specs=out_specs,
        out_shape=out_shape,
        compiler_params=_cparams("parallel"),
        name=name,
    )(*args)


def _moe_kernel(tg_ref, nu_ref, x_ref, gt_ref, wg_ref, wu_ref, wd_ref, o_ref):
    i = pl.program_id(0)
    j = pl.program_id(1)
    used = i < nu_ref[0]

    @pl.when(jnp.logical_and(j == 0, jnp.logical_not(used)))
    def _():
        o_ref[...] = jnp.zeros_like(o_ref)

    @pl.when(used)
    def _():
        x = x_ref[...]
        a = jnp.dot(x, wg_ref[...], preferred_element_type=F32)
        b = jnp.dot(x, wu_ref[...], preferred_element_type=F32)
        he = _silu(a) * b
        gates = gt_ref[...]
        lane = lax.broadcasted_iota(jnp.int32, gates.shape, 1)
        col = EXPERT_LANE0 + EXPERTS_PER_GROUP * tg_ref[i] + j // MOE_FF_STEPS
        gcol = jnp.sum(jnp.where(lane == col, gates, 0.0), axis=-1, keepdims=True)
        contrib = gcol * jnp.dot(he.astype(BF16), wd_ref[...], preferred_element_type=F32)

        @pl.when(j == 0)
        def _():
            o_ref[...] = contrib

        @pl.when(j > 0)
        def _():
            o_ref[...] += contrib


def _moe(x_sorted, gates_sorted, tile_group, n_used, w_gate, w_up, w_down, layer):
    d = D_MODEL

    def expert_of(i, j, tg, nu):
        jj = jnp.where(i < nu[0], j, MOE_STEPS_PER_TILE - 1)
        return tg[i] * EXPERTS_PER_GROUP + jj // MOE_FF_STEPS, jj % MOE_FF_STEPS

    def wgu_map(i, j, tg, nu):
        e, f = expert_of(i, j, tg, nu)
        return (layer, e, 0, f)

    def wd_map(i, j, tg, nu):
        e, f = expert_of(i, j, tg, nu)
        return (layer, e, f, 0)

    grid_spec = pltpu.PrefetchScalarGridSpec(
        num_scalar_prefetch=2,
        grid=(MOE_MAX_TILES, MOE_STEPS_PER_TILE),
        in_specs=[pl.BlockSpec((MOE_TM, d), lambda i, j, tg, nu: (i, 0)),
                  pl.BlockSpec((MOE_TM, ROUTER_LANES), lambda i, j, tg, nu: (i, 0)),
                  pl.BlockSpec((None, None, d, MOE_FF_TILE), wgu_map),
                  pl.BlockSpec((None, None, d, MOE_FF_TILE), wgu_map),
                  pl.BlockSpec((None, None, MOE_FF_TILE, d), wd_map)],
        out_specs=pl.BlockSpec((MOE_TM, d), lambda i, j, tg, nu: (i, 0)),
    )
    return pl.pallas_call(
        _moe_kernel,
        grid_spec=grid_spec,
        out_shape=jax.ShapeDtypeStruct((MOE_SLOTS, d), F32),
        compiler_params=_cparams("arbitrary", "arbitrary"),
        name="moe_experts",
    )(tile_group, n_used, x_sorted, gates_sorted, w_gate, w_up, w_down)


def _moe_plan(gates):
    gid = gates[:, 0].astype(jnp.int32)
    onehot = (gid[:, None] == jnp.arange(N_EXPERT_GROUPS, dtype=jnp.int32)[None, :]).astype(jnp.int32)
    rank = jnp.take_along_axis(jnp.cumsum(onehot, axis=0), gid[:, None], axis=1)[:, 0] - 1
    counts = jnp.sum(onehot, axis=0)
    tiles = (counts + MOE_TM - 1) // MOE_TM
    tile_end = jnp.cumsum(tiles)
    tile_start = tile_end - tiles
    dest = tile_start[gid] * MOE_TM + rank
    src = jnp.zeros((MOE_SLOTS,), jnp.int32).at[dest].set(jnp.arange(N_TOK, dtype=jnp.int32))
    valid = jnp.zeros((MOE_SLOTS,), F32).at[dest].set(1.0)
    n_used = tile_end[-1]
    t = jnp.arange(MOE_MAX_TILES, dtype=jnp.int32)
    tile_group = jnp.sum((t[:, None] >= tile_end[None, :]).astype(jnp.int32), axis=1)
    last_group = jnp.sum((n_used - 1 >= tile_end).astype(jnp.int32))
    tile_group = jnp.where(t < n_used, tile_group, last_group).astype(jnp.int32)
    return src, valid, dest, tile_group, n_used.reshape(1).astype(jnp.int32)


def _conv_step(u_ref, win_scr, w_ref, b_ref, cs):
    win_scr[CONV_PAD:CONV_PAD + cs, :] = u_ref[...]
    lo = CONV_PAD - CONV_PREV
    out = b_ref[...] + win_scr[lo:lo + cs, :] * w_ref[0:1, :]
    for j in range(1, CONV_W):
        out = out + win_scr[lo + j:lo + j + cs, :] * w_ref[j:j + 1, :]
    return out


def _conv_tail(win_scr, cs):
    return win_scr[CONV_PAD + cs - CONV_PREV:CONV_PAD + cs, :]


def _conv_advance(win_scr, cs):
    win_scr[CONV_PAD - CONV_PREV:CONV_PAD, :] = _conv_tail(win_scr, cs)


def _causal_masks(cs):
    row = lax.broadcasted_iota(jnp.int32, (cs, cs), 0)
    col = lax.broadcasted_iota(jnp.int32, (cs, cs), 1)
    return row >= col, row > col


def _cumsum_rows(incl, x):
    return _dot_f32(incl.astype(F32), x)


def _rms(x, width):
    return x * lax.rsqrt(jnp.sum(x * x, axis=-1, keepdims=True) * (1.0 / width) + EPS)


def _seq_specs(cs, nc, row0):
    rb0 = row0 // cs

    def rows(width, col):
        return pl.BlockSpec((cs, width), lambda b, c: (rb0 + b * nc + c, col))

    def const(shape):
        nd = len(shape)
        return pl.BlockSpec(shape, lambda b, c: (0,) * nd)

    def per_seq(shape, last=0):
        nd = len(shape)
        return pl.BlockSpec((None,) + shape, lambda b, c: (b,) + (0,) * (nd - 1) + (last,))

    return rows, const, per_seq


def _ssd_kernel(z_ref, x_ref, bc_ref, nar_ref, s0_ref, c0x_ref, c0bc_ref,
                cwx_ref, cbx_ref, cwbc_ref, cbbc_ref, bias_ref, alog_ref, dsk_ref, ng_ref,
                y_ref, s_out_ref, cx_out_ref, cbc_out_ref,
                st_scr, wx_scr, wbc_scr, ycat_scr, *, cs, nc):
    c = pl.program_id(1)
    lo = CONV_PAD - CONV_PREV

    @pl.when(c == 0)
    def _():
        st_scr[...] = s0_ref[...]
        wx_scr[lo:CONV_PAD, :] = c0x_ref[...]
        wbc_scr[lo:CONV_PAD, :] = c0bc_ref[...]

    xs = _silu(_conv_step(x_ref, wx_scr, cwx_ref, cbx_ref, cs))
    bcs = _silu(_conv_step(bc_ref, wbc_scr, cwbc_ref, cbbc_ref, cs))
    incl, _ = _causal_masks(cs)
    dt = jax.nn.softplus(nar_ref[...] + bias_ref[...])
    cum = _cumsum_rows(incl, dt * (-jnp.exp(alog_ref[...])))
    cum_t = cum.T
    exp_cum = jnp.exp(cum)
    cum_last = cum[cs - 1:cs, :]
    w_end = jnp.exp(cum_last - cum)
    chunk_decay = jnp.exp(cum_last)
    heads_per_group = SSD_HEADS // SSD_NGROUPS
    for g in range(SSD_NGROUPS):
        b_g = bcs[:, g * SSD_STATE:(g + 1) * SSD_STATE]
        c_g = bcs[:, (SSD_NGROUPS + g) * SSD_STATE:(SSD_NGROUPS + g + 1) * SSD_STATE]
        cb = _dot_nt(c_g, b_g)
        for hh in range(heads_per_group):
            h = g * heads_per_group + hh
            ln = LANE_SSD_DT + h
            seg = jnp.where(incl, cum[:, ln:ln + 1] - cum_t[ln:ln + 1, :], -jnp.inf)
            scores = cb * jnp.exp(seg)
            x_h = xs[:, h * SSD_HEAD_DIM:(h + 1) * SSD_HEAD_DIM]
            xdt = x_h * dt[:, ln:ln + 1]
            s_h = st_scr[h]
            y_h = _dot(scores, xdt) + _dot_nt(c_g * exp_cum[:, ln:ln + 1], s_h)
            st_scr[h] = s_h * chunk_decay[:, ln:ln + 1] + _dot_tn(xdt * w_end[:, ln:ln + 1], b_g)
            ycat_scr[:, h * SSD_HEAD_DIM:(h + 1) * SSD_HEAD_DIM] = y_h
    y = (ycat_scr[...] + dsk_ref[...] * xs) * _silu(z_ref[...])
    gw = D_GROUP // SSD_NGROUPS
    for g in range(SSD_NGROUPS):
        y_g = _rms(y[:, g * gw:(g + 1) * gw], gw) * ng_ref[:, g * gw:(g + 1) * gw]
        y_ref[:, g * gw:(g + 1) * gw] = y_g.astype(y_ref.dtype)

    @pl.when(c == nc - 1)
    def _():
        s_out_ref[...] = st_scr[...]
        cx_out_ref[...] = _conv_tail(wx_scr, cs)
        cbc_out_ref[...] = _conv_tail(wbc_scr, cs)

    _conv_advance(wx_scr, cs)
    _conv_advance(wbc_scr, cs)


def _ssd_call(wide, narrow, s0, conv0, prm, *, n_seq, seq_len, cs, row0):
    nc = seq_len // cs
    rows, const, per_seq = _seq_specs(cs, nc, row0)
    xw, bcw = D_GROUP, SSD_BC_DIM
    x_off = WIDE_OFF[SEG_SSD_XBC]
    in_specs = [rows(xw, WIDE_OFF[SEG_SSD_Z] // xw), rows(xw, x_off // xw), rows(bcw, (x_off + xw) // bcw),
                rows(NARROW_COLS, 0),
                per_seq((SSD_HEADS, SSD_HEAD_DIM, SSD_STATE)),
                per_seq((CONV_PREV, xw)), per_seq((CONV_PREV, bcw), last=xw // bcw),
                const((CONV_W, xw)), const((1, xw)), const((CONV_W, bcw)), const((1, bcw)),
                const((1, LANES)), const((1, LANES)), const((1, xw)), const((1, xw))]
    out_specs = [pl.BlockSpec((cs, xw), lambda b, c: (b * nc + c, 0)),
                 per_seq((SSD_HEADS, SSD_HEAD_DIM, SSD_STATE)),
                 per_seq((CONV_PREV, xw)), per_seq((CONV_PREV, bcw))]
    out_shape = [jax.ShapeDtypeStruct((n_seq * seq_len, xw), BF16),
                 jax.ShapeDtypeStruct((n_seq, SSD_HEADS, SSD_HEAD_DIM, SSD_STATE), F32),
                 jax.ShapeDtypeStruct((n_seq, CONV_PREV, xw), F32),
                 jax.ShapeDtypeStruct((n_seq, CONV_PREV, bcw), F32)]
    y, s_new, cx, cbc = pl.pallas_call(
        functools.partial(_ssd_kernel, cs=cs, nc=nc),
        grid=(n_seq, nc),
        in_specs=in_specs,
        out_specs=out_specs,
        out_shape=out_shape,
        scratch_shapes=[pltpu.VMEM((SSD_HEADS, SSD_HEAD_DIM, SSD_STATE), F32),
                        pltpu.VMEM((CONV_PAD + cs, xw), F32),
                        pltpu.VMEM((CONV_PAD + cs, bcw), F32),
                        pltpu.VMEM((cs, xw), F32)],
        compiler_params=_cparams("parallel", "arbitrary"),
        name="ssd_mixer",
    )(wide, wide, wide, narrow, s0, conv0, conv0,
      prm['cw'][:, :xw], prm['cb'][:, :xw], prm['cw'][:, xw:], prm['cb'][:, xw:],
      prm['bias'], prm['alog'], prm['dskip'], prm['ng'])
    return y, s_new, jnp.concatenate([cx, cbc], axis=-1)


def _ssd_params(conv_w, conv_b, dt_bias, a_log, d_skip, norm_g):
    pad = jnp.zeros((LANES - SSD_HEADS,), F32)
    return dict(cw=conv_w, cb=conv_b[None, :],
                bias=jnp.concatenate([dt_bias, pad])[None, :],
                alog=jnp.concatenate([a_log, pad])[None, :],
                dskip=jnp.repeat(d_skip, SSD_HEAD_DIM)[None, :], ng=norm_g[None, :])


def _inv_unit_lower(a, cs):
    row = lax.broadcasted_iota(jnp.int32, (cs, cs), 0)
    col = lax.broadcasted_iota(jnp.int32, (cs, cs), 1)
    p = -a
    t = jnp.where(row == col, 1.0, 0.0) + p
    n = 1
    while 2 * n < cs:
        p = _dot_f32(p, p)
        t = t + _dot_f32(t, p)
        n *= 2
    return t


def _gdn_kernel(qkv_ref, z_ref, nar_ref, s0_ref, c0_ref, cw_ref, cb_ref, bias_ref, alog_ref, ng_ref,
                y_ref, s_out_ref, c_out_ref, st_scr, win_scr, *, cs, nc):
    c = pl.program_id(1)

    @pl.when(c == 0)
    def _():
        st_scr[...] = s0_ref[...]
        win_scr[CONV_PAD - CONV_PREV:CONV_PAD, :] = c0_ref[...]

    qkv = _silu(_conv_step(qkv_ref, win_scr, cw_ref, cb_ref, cs))
    incl, strict = _causal_masks(cs)
    nar = nar_ref[...]
    g_log = -jnp.exp(alog_ref[...]) * jax.nn.softplus(nar + bias_ref[...])
    beta_all = jax.nn.sigmoid(nar)
    gc = _cumsum_rows(incl, g_log)
    gc_t = gc.T
    exp_gc = jnp.exp(gc)
    gc_last = gc[cs - 1:cs, :]
    exp_to_end = jnp.exp(gc_last - gc)
    g_last = jnp.exp(gc_last)
    hd = GDN_HEAD_DIM
    for h in range(GDN_HEADS):
        la, lb = LANE_GDN_A + h, LANE_GDN_B + h
        q = qkv[:, h * hd:(h + 1) * hd]
        k = qkv[:, D_GROUP + h * hd:D_GROUP + (h + 1) * hd]
        v = qkv[:, 2 * D_GROUP + h * hd:2 * D_GROUP + (h + 1) * hd]
        q = q * lax.rsqrt(jnp.sum(q * q, axis=-1, keepdims=True) + EPS) * (hd ** -0.5)
        k = k * lax.rsqrt(jnp.sum(k * k, axis=-1, keepdims=True) + EPS)
        decay = jnp.exp(jnp.where(incl, gc[:, la:la + 1] - gc_t[la:la + 1, :], -jnp.inf))
        beta = beta_all[:, lb:lb + 1]
        a_mat = jnp.where(strict, beta * _dot_nt(k, k) * decay, 0.0)
        t_inv = _inv_unit_lower(a_mat, cs)
        u = _dot_f32(t_inv, v * beta)
        w = _dot_f32(t_inv, k * (beta * exp_gc[:, la:la + 1]))
        qk = _dot_nt(q, k) * decay
        s_h = st_scr[h]
        v_new = u - _dot(w, s_h)
        o = _dot(q * exp_gc[:, la:la + 1], s_h) + _dot(qk, v_new)
        st_scr[h] = s_h * g_last[:, la:la + 1] + _dot_tn(k * exp_to_end[:, la:la + 1], v_new)
        o = _rms(o, hd) * ng_ref[:, h * hd:(h + 1) * hd] * _silu(z_ref[:, h * hd:(h + 1) * hd])
        y_ref[:, h * hd:(h + 1) * hd] = o.astype(y_ref.dtype)

    @pl.when(c == nc - 1)
    def _():
        s_out_ref[...] = st_scr[...]
        c_out_ref[...] = _conv_tail(win_scr, cs)

    _conv_advance(win_scr, cs)


def _gdn_call(wide, narrow, s0, conv0, prm, *, n_seq, seq_len, cs, row0):
    nc = seq_len // cs
    rows, const, per_seq = _seq_specs(cs, nc, row0)
    hd = GDN_HEAD_DIM
    in_specs = [rows(GDN_CONV_DIM, WIDE_OFF[SEG_GDN_QKV] // GDN_CONV_DIM),
                rows(D_GROUP, WIDE_OFF[SEG_GDN_Z] // D_GROUP),
                rows(NARROW_COLS, 0),
                per_seq((GDN_HEADS, hd, hd)), per_seq((CONV_PREV, GDN_CONV_DIM)),
                const((CONV_W, GDN_CONV_DIM)), const((1, GDN_CONV_DIM)),
                const((1, LANES)), const((1, LANES)), const((1, D_GROUP))]
    out_specs = [pl.BlockSpec((cs, D_GROUP), lambda b, c: (b * nc + c, 0)),
                 per_seq((GDN_HEADS, hd, hd)), per_seq((CONV_PREV, GDN_CONV_DIM))]
    out_shape = [jax.ShapeDtypeStruct((n_seq * seq_len, D_GROUP), BF16),
                 jax.ShapeDtypeStruct((n_seq, GDN_HEADS, hd, hd), F32),
                 jax.ShapeDtypeStruct((n_seq, CONV_PREV, GDN_CONV_DIM), F32)]
    return pl.pallas_call(
        functools.partial(_gdn_kernel, cs=cs, nc=nc),
        grid=(n_seq, nc),
        in_specs=in_specs,
        out_specs=out_specs,
        out_shape=out_shape,
        scratch_shapes=[pltpu.VMEM((GDN_HEADS, hd, hd), F32),
                        pltpu.VMEM((CONV_PAD + cs, GDN_CONV_DIM), F32)],
        compiler_params=_cparams("parallel", "arbitrary"),
        name="gdn_mixer",
    )(wide, wide, narrow, s0, conv0, prm['cw'], prm['cb'], prm['bias'], prm['alog'], prm['ng'])


def _lane_row(vec, lane0):
    return jnp.zeros((LANES,), F32).at[lane0:lane0 + vec.shape[0]].set(vec)[None, :]


def _gdn_params(conv_w, conv_b, dt_bias, a_log, norm_g):
    return dict(cw=conv_w, cb=conv_b[None, :], bias=_lane_row(dt_bias, LANE_GDN_A),
                alog=_lane_row(a_log, LANE_GDN_A), ng=norm_g[None, :])


def _mlstm_kernel(qkv_ref, o_ref, nar_ref, c0_ref, n0_ref, m0_ref, bias_ref, ng_ref,
                  y_ref, c_out_ref, n_out_ref, m_out_ref, c_scr, n_scr, m_scr, *, cs, nc):
    c = pl.program_id(1)

    @pl.when(c == 0)
    def _():
        c_scr[...] = c0_ref[...]
        n_scr[...] = n0_ref[...]
        m_scr[...] = m0_ref[...]

    incl, _ = _causal_masks(cs)
    pre = nar_ref[...] + bias_ref[...]
    f_cum = _cumsum_rows(incl, jax.nn.log_sigmoid(pre))
    f_cum_t = f_cum.T
    pre_t = pre.T
    hd = MLSTM_HEAD_DIM
    for h in range(MLSTM_HEADS):
        li, lf = LANE_ML_I + h, LANE_ML_F + h
        q = qkv_ref[:, h * hd:(h + 1) * hd]
        k = qkv_ref[:, D_GROUP + h * hd:D_GROUP + (h + 1) * hd] * (hd ** -0.5)
        v = qkv_ref[:, 2 * D_GROUP + h * hd:2 * D_GROUP + (h + 1) * hd]
        f_col = f_cum[:, lf:lf + 1]
        d_log = jnp.where(incl, f_col - f_cum_t[lf:lf + 1, :] + pre_t[li:li + 1, :], -jnp.inf)
        d_max = jnp.max(d_log, axis=-1, keepdims=True)
        m_prev = m_scr[:, li:li + 1]
        m_t = jnp.maximum(f_col + m_prev, d_max)
        w_carry = jnp.exp(f_col + m_prev - m_t)
        p = jnp.exp(d_log - m_t) * _dot_nt(q, k)
        c_h = c_scr[h]
        n_h = n_scr[h:h + 1, :]
        num = w_carry * _dot(q, c_h) + _dot(p, v)
        den = w_carry * jnp.sum(q * n_h, axis=-1, keepdims=True) + jnp.sum(p, axis=-1, keepdims=True)
        hh = num / jnp.maximum(jnp.abs(den), jnp.exp(-m_t))
        m_end = m_t[cs - 1:cs, :]
        f_last = f_col[cs - 1:cs, :]
        w_prev = jnp.exp(f_last + m_prev - m_end)
        kw = k * jnp.exp(f_last - f_col + pre[:, li:li + 1] - m_end)
        c_scr[h] = w_prev * c_h + _dot_tn(kw, v)
        n_scr[h:h + 1, :] = w_prev * n_h + jnp.sum(kw, axis=0, keepdims=True)
        m_scr[:, li:li + 1] = m_end
        y_h = _rms(hh, hd) * ng_ref[:, h * hd:(h + 1) * hd] * jax.nn.sigmoid(o_ref[:, h * hd:(h + 1) * hd])
        y_ref[:, h * hd:(h + 1) * hd] = y_h.astype(y_ref.dtype)

    @pl.when(c == nc - 1)
    def _():
        c_out_ref[...] = c_scr[...]
        n_out_ref[...] = n_scr[...]
        m_out_ref[...] = m_scr[...]


def _mlstm_call(wide, narrow, c0, n0, m0, prm, *, n_seq, seq_len, cs, row0):
    nc = seq_len // cs
    rows, const, per_seq = _seq_specs(cs, nc, row0)
    hd = MLSTM_HEAD_DIM
    qkv_w = 3 * D_GROUP
    in_specs = [rows(qkv_w, WIDE_OFF[SEG_ML_QKV] // qkv_w),
                rows(D_GROUP, WIDE_OFF[SEG_ML_O] // D_GROUP),
                rows(NARROW_COLS, 0),
                per_seq((MLSTM_HEADS, hd, hd)), per_seq((MLSTM_HEADS, hd)), per_seq((1, LANES)),
                const((1, LANES)), const((1, D_GROUP))]
    out_specs = [pl.BlockSpec((cs, D_GROUP), lambda b, c: (b * nc + c, 0)),
                 per_seq((MLSTM_HEADS, hd, hd)), per_seq((MLSTM_HEADS, hd)), per_seq((1, LANES))]
    out_shape = [jax.ShapeDtypeStruct((n_seq * seq_len, D_GROUP), BF16),
                 jax.ShapeDtypeStruct((n_seq, MLSTM_HEADS, hd, hd), F32),
                 jax.ShapeDtypeStruct((n_seq, MLSTM_HEADS, hd), F32),
                 jax.ShapeDtypeStruct((n_seq, 1, LANES), F32)]
    return pl.pallas_call(
        functools.partial(_mlstm_kernel, cs=cs, nc=nc),
        grid=(n_seq, nc),
        in_specs=in_specs,
        out_specs=out_specs,
        out_shape=out_shape,
        scratch_shapes=[pltpu.VMEM((MLSTM_HEADS, hd, hd), F32),
                        pltpu.VMEM((MLSTM_HEADS, hd), F32),
                        pltpu.VMEM((1, LANES), F32)],
        compiler_params=_cparams("parallel", "arbitrary"),
        name="mlstm_mixer",
    )(wide, wide, narrow, c0, n0, m0, prm['bias'], prm['ng'])


def _mlstm_params(i_bias, f_bias, norm_g):
    return dict(bias=_lane_row(i_bias, LANE_ML_I) + _lane_row(f_bias, LANE_ML_F), ng=norm_g[None, :])


def _lru_kernel(x_ref, gate_ref, h0_ref, c0_ref, cw_ref, cb_ref, wa_ref, ba_ref, wx_ref, bx_ref, lam_ref,
                y_ref, h_out_ref, c_out_ref, h_scr, win_scr, *, cs, nc):
    c = pl.program_id(1)

    @pl.when(c == 0)
    def _():
        h_scr[...] = h0_ref[...]
        win_scr[CONV_PAD - CONV_PREV:CONV_PAD, :] = c0_ref[...]

    xc = _conv_step(x_ref, win_scr, cw_ref, cb_ref, cs)
    r_parts, i_parts = [], []
    for g in range(LRU_BLOCKS):
        x_g = xc[:, g * LRU_BLOCK:(g + 1) * LRU_BLOCK].astype(BF16)
        r_parts.append(jnp.dot(x_g, wa_ref[g], preferred_element_type=F32))
        i_parts.append(jnp.dot(x_g, wx_ref[g], preferred_element_type=F32))
    r = jax.nn.sigmoid(jnp.concatenate(r_parts, axis=-1) + ba_ref[...])
    i = jax.nn.sigmoid(jnp.concatenate(i_parts, axis=-1) + bx_ref[...])
    log_a = -LRU_C * r * jax.nn.softplus(-lam_ref[...])
    a = jnp.exp(log_a)
    u = jnp.sqrt(jnp.tanh(-log_a) * (a * a + 1.0)) * (i * xc)
    row = lax.broadcasted_iota(jnp.int32, (cs, D_GROUP), 0)
    u = u + jnp.where(row == 0, a * h_scr[...], 0.0)
    shift = 1
    while shift < cs:
        a_sh = pltpu.roll(a, shift, 0)
        u_sh = pltpu.roll(u, shift, 0)
        live = row >= shift
        u = jnp.where(live, a * u_sh + u, u)
        a = jnp.where(live, a * a_sh, a)
        shift *= 2
    h_scr[...] = u[cs - 1:cs, :]
    y_ref[...] = (u * jax.nn.gelu(gate_ref[...])).astype(y_ref.dtype)

    @pl.when(c == nc - 1)
    def _():
        h_out_ref[...] = u[cs - 1:cs, :]
        c_out_ref[...] = _conv_tail(win_scr, cs)

    _conv_advance(win_scr, cs)


def _lru_call(wide, h0, conv0, prm, *, n_seq, seq_len, cs, row0):
    nc = seq_len // cs
    rows, const, per_seq = _seq_specs(cs, nc, row0)
    in_specs = [rows(D_GROUP, WIDE_OFF[SEG_LRU_X] // D_GROUP), rows(D_GROUP, WIDE_OFF[SEG_LRU_GATE] // D_GROUP),
                per_seq((1, D_GROUP)), per_seq((CONV_PREV, D_GROUP)),
                const((CONV_W, D_GROUP)), const((1, D_GROUP)),
                const((LRU_BLOCKS, LRU_BLOCK, LRU_BLOCK)), const((1, D_GROUP)),
                const((LRU_BLOCKS, LRU_BLOCK, LRU_BLOCK)), const((1, D_GROUP)), const((1, D_GROUP))]
    out_specs = [pl.BlockSpec((cs, D_GROUP), lambda b, c: (b * nc + c, 0)),
                 per_seq((1, D_GROUP)), per_seq((CONV_PREV, D_GROUP))]
    out_shape = [jax.ShapeDtypeStruct((n_seq * seq_len, D_GROUP), BF16),
                 jax.ShapeDtypeStruct((n_seq, 1, D_GROUP), F32),
                 jax.ShapeDtypeStruct((n_seq, CONV_PREV, D_GROUP), F32)]
    return pl.pallas_call(
        functools.partial(_lru_kernel, cs=cs, nc=nc),
        grid=(n_seq, nc),
        in_specs=in_specs,
        out_specs=out_specs,
        out_shape=out_shape,
        scratch_shapes=[pltpu.VMEM((1, D_GROUP), F32), pltpu.VMEM((CONV_PAD + cs, D_GROUP), F32)],
        compiler_params=_cparams("parallel", "arbitrary"),
        name="rglru_mixer",
    )(wide, wide, h0, conv0, prm['cw'], prm['cb'], prm['wa'], prm['ba'], prm['wx'], prm['bx'], prm['lam'])


def _lru_params(conv_w, conv_b, w_a, b_a, w_x, b_x, lam):
    return dict(cw=conv_w, cb=conv_b[None, :], wa=w_a.astype(BF16), ba=b_a.reshape(1, D_GROUP),
                wx=w_x.astype(BF16), bx=b_x.reshape(1, D_GROUP), lam=lam[None, :])


def _mixer_group(wide, narrow, st, prm, *, n_seq, seq_len, row0):
    s_ssd, s_ssd_conv, s_gdn, s_gdn_conv, s_mc, s_mn, s_mm, s_lru, s_lru_conv = st
    cs = min(seq_len, CHUNK)
    kw = dict(n_seq=n_seq, seq_len=seq_len, cs=cs, row0=row0)
    y_a, ssd_new, ssd_conv_new = _ssd_call(wide, narrow, s_ssd, s_ssd_conv, prm['ssd'], **kw)
    y_b, gdn_new, gdn_conv_new = _gdn_call(wide, narrow, s_gdn, s_gdn_conv, prm['gdn'], **kw)
    m0 = jnp.zeros((n_seq, 1, LANES), F32).at[:, 0, LANE_ML_I:LANE_ML_I + MLSTM_HEADS].set(s_mm)
    y_c, mc_new, mn_new, mm_new = _mlstm_call(wide, narrow, s_mc, s_mn, m0, prm['mlstm'], **kw)
    y_d, lru_new, lru_conv_new = _lru_call(wide, s_lru[:, None, :], s_lru_conv, prm['lru'], **kw)
    new = (ssd_new, ssd_conv_new, gdn_new, gdn_conv_new, mc_new, mn_new,
           mm_new[:, 0, LANE_ML_I:LANE_ML_I + MLSTM_HEADS], lru_new[:, 0, :], lru_conv_new)
    return (y_a, y_b, y_c, y_d), new


def _state_shapes(n):
    return ((n, SSD_HEADS, SSD_HEAD_DIM, SSD_STATE),
            (n, CONV_PREV, SSD_CONV_DIM),
            (n, GDN_HEADS, GDN_HEAD_DIM, GDN_HEAD_DIM),
            (n, CONV_PREV, GDN_CONV_DIM),
            (n, MLSTM_HEADS, MLSTM_HEAD_DIM, MLSTM_HEAD_DIM),
            (n, MLSTM_HEADS, MLSTM_HEAD_DIM),
            (n, MLSTM_HEADS),
            (n, D_GROUP),
            (n, CONV_PREV, D_GROUP))


def kernel(x_prompt, x_sample, state_ssd, state_ssd_conv, state_gdn, state_gdn_conv, state_mlstm_c, state_mlstm_n, state_mlstm_m, state_rglru, state_rglru_conv, c_prompt, c_sample, w_ada, ada_table, norm1_g, norm2_g, final_g, w_in, w_out, ssd_conv_w, ssd_conv_b, ssd_dt_bias, ssd_a_log, ssd_d, ssd_norm_g, gdn_conv_w, gdn_conv_b, gdn_dt_bias, gdn_a_log, gdn_norm_g, mlstm_i_bias, mlstm_f_bias, mlstm_norm_g, lru_conv_w, lru_conv_b, lru_w_a, lru_b_a, lru_w_x, lru_b_x, lru_lambda, moe_w_group, moe_b_group, moe_w_expert, moe_b_expert, moe_w_gate, moe_w_up, moe_w_down):
    d = D_MODEL

    w_in_wide = jnp.concatenate(
        [w_in[:, :, IN_OFFSETS[s]:IN_OFFSETS[s + 1]] for s in _WIDE_SEGS], axis=-1).astype(BF16)
    w_in_narrow = jnp.concatenate(
        [w_in[:, :, IN_OFFSETS[s]:IN_OFFSETS[s + 1]] for s in _NARROW_SEGS]
        + [jnp.zeros((DEPTH, d, NARROW_COLS - sum(IN_SIZES[s] for s in _NARROW_SEGS)), w_in.dtype)],
        axis=-1).astype(BF16)
    w_out_b = w_out.astype(BF16)
    w_gate_b = moe_w_gate.astype(BF16)
    w_up_b = moe_w_up.astype(BF16)
    w_down_b = moe_w_down.astype(BF16)
    router_pad = jnp.zeros((DEPTH, d, ROUTER_LANES - N_EXPERT_GROUPS - N_EXPERTS), F32)
    w_router = jnp.concatenate([moe_w_group, moe_w_expert, router_pad], axis=-1)
    b_router = jnp.concatenate([moe_b_group, moe_b_expert, router_pad[:, 0, :]], axis=-1)[:, None, :]

    c_pad = jnp.concatenate([c_prompt, c_sample, jnp.zeros((ADA_ROWS - BATCH - DEC_BATCH, d), F32)], axis=0)
    mod_shared = _ada_matmul(c_pad, w_ada)
    mod_tiles = jnp.concatenate(
        [jnp.broadcast_to(mod_shared[:BATCH, None, :], (BATCH, ROW_TILE, 6 * d)),
         jnp.repeat(mod_shared[BATCH:BATCH + DEC_BATCH], DEC_SEQ, axis=0).reshape(SAMPLE_ROW_TILES, ROW_TILE, 6 * d)],
        axis=0)
    ada_rows = ada_table.reshape(DEPTH * 6, 1, d)

    x = jnp.concatenate([x_prompt.reshape(N_PROMPT_TOK, d), x_sample.reshape(N_SAMPLE_TOK, d)], axis=0)

    st_sample = (state_ssd, state_ssd_conv, state_gdn, state_gdn_conv, state_mlstm_c,
                 state_mlstm_n, state_mlstm_m, state_rglru, state_rglru_conv)
    zero_states = tuple(jnp.zeros(s, F32) for s in _state_shapes(BATCH))
    new_p = [[] for _ in st_sample]
    new_s = [[] for _ in st_sample]

    y_prev = None
    for l in range(DEPTH):
        if l == 0:
            (h,) = _rowwise(x, norm1_g[l][None, :], mod_tiles=mod_tiles, ada_rows=ada_rows, layer=l,
                            mod_chunks=(1, 0), name="norm1")
        else:
            x, h = _rowwise(x, norm1_g[l][None, :], y=y_prev, mod_tiles=mod_tiles, ada_rows=ada_rows,
                            layer=l, gate_chunk=5, gate_layer=l - 1, mod_chunks=(1, 0), name="resid_norm1")
        wide = _matmul(h, w_in_wide[l], MM_TN, name="in_proj")
        narrow = _matmul(h, w_in_narrow[l], NARROW_COLS, name="in_proj_gates")

        prm = dict(ssd=_ssd_params(ssd_conv_w[l], ssd_conv_b[l], ssd_dt_bias[l], ssd_a_log[l], ssd_d[l],
                                   ssd_norm_g[l]),
                   gdn=_gdn_params(gdn_conv_w[l], gdn_conv_b[l], gdn_dt_bias[l], gdn_a_log[l], gdn_norm_g[l]),
                   mlstm=_mlstm_params(mlstm_i_bias[l], mlstm_f_bias[l], mlstm_norm_g[l]),
                   lru=_lru_params(lru_conv_w[l], lru_conv_b[l], lru_w_a[l], lru_b_a[l], lru_w_x[l], lru_b_x[l],
                                   lru_lambda[l]))
        ys_p, st_p = _mixer_group(wide, narrow, zero_states, prm, n_seq=BATCH, seq_len=SEQ, row0=0)
        ys_s, st_s = _mixer_group(wide, narrow, tuple(s[l] for s in st_sample), prm,
                                  n_seq=DEC_BATCH, seq_len=DEC_SEQ, row0=N_PROMPT_TOK)
        for acc, s in zip(new_p, st_p):
            acc.append(s)
        for acc, s in zip(new_s, st_s):
            acc.append(s)
        ys = [jnp.concatenate([a, b], axis=0) for a, b in zip(ys_p, ys_s)]
        y_mix = _out_proj(ys, w_out_b[l])

        x, h2, gates = _rowwise(x, norm2_g[l][None, :], y=y_mix, mod_tiles=mod_tiles, ada_rows=ada_rows,
                                layer=l, gate_chunk=2, mod_chunks=(4, 3),
                                router=(w_router[l], b_router[l]), name="resid_norm2_router")
        src, valid, dest, tile_group, n_used = _moe_plan(gates)
        x_sorted = jnp.take(h2, src, axis=0)
        gates_sorted = jnp.take(gates, src, axis=0) * valid[:, None]
        y_sorted = _moe(x_sorted, gates_sorted, tile_group, n_used, w_gate_b, w_up_b, w_down_b, l)
        y_prev = jnp.take(y_sorted, dest, axis=0)

    x, y_fin = _rowwise(x, final_g[None, :], y=y_prev, mod_tiles=mod_tiles, ada_rows=ada_rows,
                        layer=DEPTH - 1, gate_chunk=5, h_dtype=F32, name="resid_final_norm")
    outs = [y_fin[:N_PROMPT_TOK].reshape(BATCH, SEQ, d), y_fin[N_PROMPT_TOK:].reshape(DEC_BATCH, DEC_SEQ, d)]
    for acc_p, acc_s in zip(new_p, new_s):
        outs.append(jnp.stack(acc_p))
        outs.append(jnp.stack(acc_s))
    return tuple(outs)
```

```python
import functools

import numpy as np
import jax
import jax.numpy as jnp
from jax import lax
from jax.experimental import pallas as pl
from jax.experimental.pallas import tpu as pltpu

F32 = jnp.float32
BF16 = jnp.bfloat16

D_MODEL = 4096
BATCH = 4
SEQ = 2048
DEPTH = 4
DEC_BATCH = 128
DEC_SEQ = 8
EPS = 1e-6
N_MIXERS = 4
D_GROUP = D_MODEL // N_MIXERS
CONV_W = 4
CHUNK = 64
SSD_HEAD_DIM = 64
SSD_HEADS = D_GROUP // SSD_HEAD_DIM
SSD_NGROUPS = 2
SSD_STATE = 128
SSD_BC_DIM = 2 * SSD_NGROUPS * SSD_STATE
SSD_CONV_DIM = D_GROUP + SSD_BC_DIM
GDN_HEAD_DIM = 128
GDN_HEADS = D_GROUP // GDN_HEAD_DIM
GDN_CONV_DIM = 3 * D_GROUP
MLSTM_HEAD_DIM = 128
MLSTM_HEADS = D_GROUP // MLSTM_HEAD_DIM
LRU_BLOCKS = 8
LRU_BLOCK = D_GROUP // LRU_BLOCKS
LRU_C = 8.0
N_EXPERT_GROUPS = 4
EXPERTS_PER_GROUP = 4
N_EXPERTS = N_EXPERT_GROUPS * EXPERTS_PER_GROUP
D_FF_EXPERT = D_MODEL // 8
IN_SIZES = (D_GROUP, SSD_CONV_DIM, SSD_HEADS,
            GDN_CONV_DIM, GDN_HEADS, GDN_HEADS, D_GROUP,
            3 * D_GROUP, MLSTM_HEADS, MLSTM_HEADS, D_GROUP,
            D_GROUP, D_GROUP)
IN_OFFSETS = tuple(int(o) for o in np.cumsum((0,) + IN_SIZES))
(SEG_SSD_Z, SEG_SSD_XBC, SEG_SSD_DT, SEG_GDN_QKV, SEG_GDN_A, SEG_GDN_B, SEG_GDN_Z,
 SEG_ML_QKV, SEG_ML_I, SEG_ML_F, SEG_ML_O, SEG_LRU_X, SEG_LRU_GATE) = range(len(IN_SIZES))

N_PROMPT_TOK = BATCH * SEQ
N_SAMPLE_TOK = DEC_BATCH * DEC_SEQ
N_TOK = N_PROMPT_TOK + N_SAMPLE_TOK

LANES = 128
SUBLANES = 8
VMEM_LIMIT_BYTES = 56 * 1024 * 1024

ROW_TILE = 128
N_ROW_TILES = N_TOK // ROW_TILE
PROMPT_TILES_PER_SEQ = SEQ // ROW_TILE
PROMPT_ROW_TILES = N_PROMPT_TOK // ROW_TILE
SAMPLE_ROW_TILES = N_SAMPLE_TOK // ROW_TILE
MM_TM = 1024
MM_TN = 512
ADA_ROWS = 144
ADA_TN = 512
MOE_TM = 512
MOE_FF_TILE = 256
MOE_FF_STEPS = D_FF_EXPERT // MOE_FF_TILE
TOP_K_IN_GROUP = 2
PAIRS = tuple((a, b) for a in range(EXPERTS_PER_GROUP) for b in range(a + 1, EXPERTS_PER_GROUP))
N_PAIRS = len(PAIRS)
N_BUCKETS = N_EXPERT_GROUPS * N_PAIRS
MOE_STEPS_PER_TILE = TOP_K_IN_GROUP * MOE_FF_STEPS
MOE_MAX_TILES = N_TOK // MOE_TM + N_BUCKETS
MOE_SLOTS = MOE_MAX_TILES * MOE_TM
ROUTER_LANES = LANES
EXPERT_LANE0 = N_EXPERT_GROUPS
BUCKET_LANE = 0

_WIDE_SEGS = (SEG_GDN_QKV, SEG_ML_QKV, SEG_SSD_Z, SEG_GDN_Z, SEG_ML_O, SEG_LRU_X, SEG_LRU_GATE, SEG_SSD_XBC)
_NARROW_SEGS = (SEG_SSD_DT, SEG_GDN_A, SEG_GDN_B, SEG_ML_I, SEG_ML_F)
WIDE_OFF = {}
_o = 0
for _s in _WIDE_SEGS:
    WIDE_OFF[_s] = _o
    _o += IN_SIZES[_s]
WIDE_COLS = _o
NARROW_OFF = {}
_o = 0
for _s in _NARROW_SEGS:
    NARROW_OFF[_s] = _o
    _o += IN_SIZES[_s]
NARROW_COLS = LANES
LANE_SSD_DT = NARROW_OFF[SEG_SSD_DT]
LANE_GDN_A = NARROW_OFF[SEG_GDN_A]
LANE_GDN_B = NARROW_OFF[SEG_GDN_B]
LANE_ML_I = NARROW_OFF[SEG_ML_I]
LANE_ML_F = NARROW_OFF[SEG_ML_F]

CONV_PAD = SUBLANES
CONV_PREV = CONV_W - 1


def _cparams(*sem):
    return pltpu.CompilerParams(dimension_semantics=sem, vmem_limit_bytes=VMEM_LIMIT_BYTES)


def _dot(a, b):
    return jnp.dot(a.astype(BF16), b.astype(BF16), preferred_element_type=F32)


def _dot_nt(a, b):
    return lax.dot_general(a.astype(BF16), b.astype(BF16), (((1,), (1,)), ((), ())), preferred_element_type=F32)


def _dot_tn(a, b):
    return lax.dot_general(a.astype(BF16), b.astype(BF16), (((0,), (0,)), ((), ())), preferred_element_type=F32)


def _dot_f32(a, b):
    return jnp.dot(a, b, preferred_element_type=F32, precision=lax.Precision.HIGHEST)


def _silu(x):
    return x * jax.nn.sigmoid(x)


def _mm_kernel(a_ref, w_ref, o_ref):
    o_ref[...] = jnp.dot(a_ref[...], w_ref[...], preferred_element_type=F32).astype(o_ref.dtype)


def _matmul(a, w, tn, name):
    m, k = a.shape
    n = w.shape[1]
    assert m % MM_TM == 0 and n % tn == 0
    return pl.pallas_call(
        _mm_kernel,
        grid=(m // MM_TM, n // tn),
        in_specs=[pl.BlockSpec((MM_TM, k), lambda i, j: (i, 0)),
                  pl.BlockSpec((k, tn), lambda i, j: (0, j))],
        out_specs=pl.BlockSpec((MM_TM, tn), lambda i, j: (i, j)),
        out_shape=jax.ShapeDtypeStruct((m, n), F32),
        compiler_params=_cparams("parallel", "arbitrary"),
        name=name,
    )(a, w)


PROMPT_MM_TILES = N_PROMPT_TOK // MM_TM
assert N_SAMPLE_TOK == MM_TM


def _out_proj_kernel(*refs):
    yp_refs, ys_refs = refs[:N_MIXERS], refs[N_MIXERS:2 * N_MIXERS]
    w_refs, o_ref = refs[2 * N_MIXERS:3 * N_MIXERS], refs[3 * N_MIXERS]

    def project(y_refs):
        acc = jnp.dot(y_refs[0][...], w_refs[0][...], preferred_element_type=F32)
        for m in range(1, N_MIXERS):
            acc += jnp.dot(y_refs[m][...], w_refs[m][...], preferred_element_type=F32)
        o_ref[...] = acc

    is_prompt = pl.program_id(0) < PROMPT_MM_TILES
    pl.when(is_prompt)(lambda: project(yp_refs))
    pl.when(jnp.logical_not(is_prompt))(lambda: project(ys_refs))


def _out_proj(ys_prompt, ys_sample, w):
    n = w.shape[1]
    yp_spec = pl.BlockSpec((MM_TM, D_GROUP), lambda i, j: (jnp.minimum(i, PROMPT_MM_TILES - 1), 0))
    ys_spec = pl.BlockSpec((MM_TM, D_GROUP), lambda i, j: (0, 0))
    w_specs = [pl.BlockSpec((D_GROUP, MM_TN), functools.partial(lambda i, j, m: (m, j), m=m))
               for m in range(N_MIXERS)]
    return pl.pallas_call(
        _out_proj_kernel,
        grid=(N_TOK // MM_TM, n // MM_TN),
        in_specs=[yp_spec] * N_MIXERS + [ys_spec] * N_MIXERS + w_specs,
        out_specs=pl.BlockSpec((MM_TM, MM_TN), lambda i, j: (i, j)),
        out_shape=jax.ShapeDtypeStruct((N_TOK, n), F32),
        compiler_params=_cparams("parallel", "arbitrary"),
        name="out_proj",
    )(*ys_prompt, *ys_sample, w, w, w, w)


def _ada_kernel(c_ref, w_ref, o_ref):
    o_ref[...] = _dot(_silu(c_ref[...]), w_ref[...])


def _ada_matmul(c_pad, w_ada):
    k, n = w_ada.shape
    return pl.pallas_call(
        _ada_kernel,
        grid=(n // ADA_TN,),
        in_specs=[pl.BlockSpec((ADA_ROWS, k), lambda j: (0, 0)),
                  pl.BlockSpec((k, ADA_TN), lambda j: (0, j))],
        out_specs=pl.BlockSpec((ADA_ROWS, ADA_TN), lambda j: (0, j)),
        out_shape=jax.ShapeDtypeStruct((ADA_ROWS, n), F32),
        compiler_params=_cparams("arbitrary"),
        name="adaln_matmul",
    )(c_pad, w_ada)


def _route(lg):
    lane = lax.broadcasted_iota(jnp.int32, lg.shape, 1).astype(F32)
    neg = -jnp.inf
    big = float(ROUTER_LANES)
    gl = jnp.where(lane < N_EXPERT_GROUPS, lg, neg)
    gmax = jnp.max(gl, axis=-1, keepdims=True)
    gidx = jnp.min(jnp.where(gl == gmax, lane, big), axis=-1, keepdims=True)
    g_w = 1.0 / jnp.sum(jnp.exp(gl - gmax), axis=-1, keepdims=True)
    lo = EXPERT_LANE0 + EXPERTS_PER_GROUP * gidx
    el = jnp.where((lane >= lo) & (lane < lo + EXPERTS_PER_GROUP), lg, neg)
    e1 = jnp.max(el, axis=-1, keepdims=True)
    i1 = jnp.min(jnp.where(el == e1, lane, big), axis=-1, keepdims=True)
    el2 = jnp.where(lane == i1, neg, el)
    e2 = jnp.max(el2, axis=-1, keepdims=True)
    i2 = jnp.min(jnp.where(el2 == e2, lane, big), axis=-1, keepdims=True)
    t = jnp.exp(e2 - e1)
    w1 = g_w / (1.0 + t)
    w2 = g_w * t / (1.0 + t)
    out = jnp.where(lane == i1, w1, 0.0) + jnp.where(lane == i2, w2, 0.0)
    pa = jnp.minimum(i1, i2) - lo
    pb = jnp.maximum(i1, i2) - lo
    pair = pa * (2 * EXPERTS_PER_GROUP - 1 - pa) * 0.5 + (pb - pa - 1.0)
    return out + jnp.where(lane == float(BUCKET_LANE), gidx * N_PAIRS + pair, 0.0)


def _rowwise_kernel(*refs, has_resid, has_mod, has_router):
    it = iter(refs)
    x_ref = next(it)
    if has_resid:
        y_ref, gate_m, gate_t = next(it), next(it), next(it)
    g_ref = next(it)
    if has_mod:
        sc_m, sc_t, sh_m, sh_t = next(it), next(it), next(it), next(it)
    if has_router:
        wr_ref, br_ref = next(it), next(it)
    if has_resid:
        xo_ref = next(it)
    h_ref = next(it)
    if has_router:
        gt_ref = next(it)

    x = x_ref[...]
    if has_resid:
        x = x + (gate_m[...] + gate_t[...]) * y_ref[...]
        xo_ref[...] = x
    y = x * lax.rsqrt(jnp.mean(x * x, axis=-1, keepdims=True) + EPS)
    h = y * g_ref[...]
    if has_mod:
        h = h * (1.0 + (sc_m[...] + sc_t[...])) + (sh_m[...] + sh_t[...])
    h_ref[...] = h.astype(h_ref.dtype)
    if has_router:
        gt_ref[...] = _route(_dot_f32(h, wr_ref[...]) + br_ref[...])


def _mod_tile_index(i):
    return jnp.where(i < PROMPT_ROW_TILES, i // PROMPT_TILES_PER_SEQ, BATCH + i - PROMPT_ROW_TILES)


def _rowwise(x, g_row, *, y=None, mod_tiles=None, ada_rows=None, layer=0, gate_chunk=None, gate_layer=None,
             mod_chunks=None, router=None, h_dtype=BF16, name="rowwise"):
    has_resid = y is not None
    has_mod = mod_chunks is not None
    has_router = router is not None
    d = D_MODEL
    row_spec = pl.BlockSpec((ROW_TILE, d), lambda i: (i, 0))
    vec_spec = pl.BlockSpec((1, d), lambda i: (0, 0))

    def mod_specs(chunk, lyr):
        return [pl.BlockSpec((None, ROW_TILE, d), lambda i: (_mod_tile_index(i), 0, chunk)),
                pl.BlockSpec((None, 1, d), lambda i: (lyr * 6 + chunk, 0, 0))]

    args, specs = [x], [row_spec]
    if has_resid:
        args += [y, mod_tiles, ada_rows]
        specs += [row_spec] + mod_specs(gate_chunk, layer if gate_layer is None else gate_layer)
    args.append(g_row)
    specs.append(vec_spec)
    if has_mod:
        sc_chunk, sh_chunk = mod_chunks
        args += [mod_tiles, ada_rows, mod_tiles, ada_rows]
        specs += mod_specs(sc_chunk, layer) + mod_specs(sh_chunk, layer)
    if has_router:
        w_r, b_r = router
        args += [w_r, b_r]
        specs += [pl.BlockSpec((d, ROUTER_LANES), lambda i: (0, 0)),
                  pl.BlockSpec((1, ROUTER_LANES), lambda i: (0, 0))]
    out_shape, out_specs = [], []
    if has_resid:
        out_shape.append(jax.ShapeDtypeStruct((N_TOK, d), F32))
        out_specs.append(row_spec)
    out_shape.append(jax.ShapeDtypeStruct((N_TOK, d), h_dtype))
    out_specs.append(row_spec)
    if has_router:
        out_shape.append(jax.ShapeDtypeStruct((N_TOK, ROUTER_LANES), F32))
        out_specs.append(pl.BlockSpec((ROW_TILE, ROUTER_LANES), lambda i: (i, 0)))
    return pl.pallas_call(
        functools.partial(_rowwise_kernel, has_resid=has_resid, has_mod=has_mod, has_router=has_router),
        grid=(N_ROW_TILES,),
        in_specs=specs,
        out_specs=out_specs,
        out_shape=out_shape,
        compiler_params=_cparams("parallel"),
        name=name,
    )(*args)


def _moe_kernel(ea_ref, eb_ref, nu_ref, x_ref, gt_ref, wg_ref, wu_ref, wd_ref, o_ref):
    i = pl.program_id(0)
    j = pl.program_id(1)
    used = i < nu_ref[0]

    @pl.when(jnp.logical_and(j == 0, jnp.logical_not(used)))
    def _():
        o_ref[...] = jnp.zeros_like(o_ref)

    @pl.when(used)
    def _():
        x = x_ref[...]
        a = jnp.dot(x, wg_ref[...], preferred_element_type=F32)
        b = jnp.dot(x, wu_ref[...], preferred_element_type=F32)
        he = _silu(a) * b
        gates = gt_ref[...]
        lane = lax.broadcasted_iota(jnp.int32, gates.shape, 1)
        col = EXPERT_LANE0 + jnp.where(j < MOE_FF_STEPS, ea_ref[i], eb_ref[i])
        gcol = jnp.sum(jnp.where(lane == col, gates, 0.0), axis=-1, keepdims=True)
        contrib = gcol * jnp.dot(he.astype(BF16), wd_ref[...], preferred_element_type=F32)

        @pl.when(j == 0)
        def _():
            o_ref[...] = contrib

        @pl.when(j > 0)
        def _():
            o_ref[...] += contrib


def _moe(x_sorted, gates_sorted, tile_ea, tile_eb, n_used, w_gate, w_up, w_down, layer):
    d = D_MODEL

    def expert_of(i, j, ea, eb, nu):
        jj = jnp.where(i < nu[0], j, MOE_STEPS_PER_TILE - 1)
        return jnp.where(jj < MOE_FF_STEPS, ea[i], eb[i]), jj % MOE_FF_STEPS

    def in_row_map(i, j, ea, eb, nu):
        return (jnp.minimum(i, jnp.minimum(nu[0], MOE_MAX_TILES - 1)), 0)

    def wgu_map(i, j, ea, eb, nu):
        e, f = expert_of(i, j, ea, eb, nu)
        return (layer, e, 0, f)

    def wd_map(i, j, ea, eb, nu):
        e, f = expert_of(i, j, ea, eb, nu)
        return (layer, e, f, 0)

    grid_spec = pltpu.PrefetchScalarGridSpec(
        num_scalar_prefetch=3,
        grid=(MOE_MAX_TILES, MOE_STEPS_PER_TILE),
        in_specs=[pl.BlockSpec((MOE_TM, d), in_row_map),
                  pl.BlockSpec((MOE_TM, ROUTER_LANES), in_row_map),
                  pl.BlockSpec((None, None, d, MOE_FF_TILE), wgu_map),
                  pl.BlockSpec((None, None, d, MOE_FF_TILE), wgu_map),
                  pl.BlockSpec((None, None, MOE_FF_TILE, d), wd_map)],
        out_specs=pl.BlockSpec((MOE_TM, d), lambda i, j, ea, eb, nu: (i, 0)),
    )
    return pl.pallas_call(
        _moe_kernel,
        grid_spec=grid_spec,
        out_shape=jax.ShapeDtypeStruct((MOE_SLOTS, d), F32),
        compiler_params=_cparams("arbitrary", "arbitrary"),
        name="moe_experts",
    )(tile_ea, tile_eb, n_used, x_sorted, gates_sorted, w_gate, w_up, w_down)


def _take_rows(a, idx):
    return a.at[idx].get(mode="promise_in_bounds")


def _moe_plan(gates):
    bid = gates[:, BUCKET_LANE].astype(jnp.int32)
    onehot = (bid[:, None] == jnp.arange(N_BUCKETS, dtype=jnp.int32)[None, :]).astype(jnp.int32)
    rank = jnp.take_along_axis(jnp.cumsum(onehot, axis=0), bid[:, None], axis=1)[:, 0] - 1
    counts = jnp.sum(onehot, axis=0)
    tiles = (counts + MOE_TM - 1) // MOE_TM
    tile_end = jnp.cumsum(tiles)
    tile_start = tile_end - tiles
    dest = tile_start[bid] * MOE_TM + rank
    src = jnp.zeros((MOE_SLOTS,), jnp.int32).at[dest].set(jnp.arange(N_TOK, dtype=jnp.int32))
    valid = jnp.zeros((MOE_SLOTS,), F32).at[dest].set(1.0)
    n_used = tile_end[-1]
    t = jnp.arange(MOE_MAX_TILES, dtype=jnp.int32)
    tile_bucket = jnp.sum((jnp.minimum(t, n_used - 1)[:, None] >= tile_end[None, :]).astype(jnp.int32), axis=1)
    group0 = (tile_bucket // N_PAIRS) * EXPERTS_PER_GROUP
    pair = tile_bucket % N_PAIRS
    tile_ea = group0 + jnp.asarray([p[0] for p in PAIRS], jnp.int32)[pair]
    tile_eb = group0 + jnp.asarray([p[1] for p in PAIRS], jnp.int32)[pair]
    return src, valid, dest, tile_ea, tile_eb, n_used.reshape(1).astype(jnp.int32)


def _conv_step(u_ref, win_scr, w_ref, b_ref, cs):
    win_scr[CONV_PAD:CONV_PAD + cs, :] = u_ref[...]
    lo = CONV_PAD - CONV_PREV
    out = b_ref[...] + win_scr[lo:lo + cs, :] * w_ref[0:1, :]
    for j in range(1, CONV_W):
        out = out + win_scr[lo + j:lo + j + cs, :] * w_ref[j:j + 1, :]
    return out


def _conv_tail(win_scr, cs):
    return win_scr[CONV_PAD + cs - CONV_PREV:CONV_PAD + cs, :]


def _conv_advance(win_scr, cs):
    win_scr[CONV_PAD - CONV_PREV:CONV_PAD, :] = _conv_tail(win_scr, cs)


def _causal_masks(cs):
    row = lax.broadcasted_iota(jnp.int32, (cs, cs), 0)
    col = lax.broadcasted_iota(jnp.int32, (cs, cs), 1)
    return row >= col, row > col


def _cumsum_rows(incl, x):
    return _dot_f32(incl.astype(F32), x)


def _rms(x, width):
    return x * lax.rsqrt(jnp.sum(x * x, axis=-1, keepdims=True) * (1.0 / width) + EPS)


def _seq_specs(cs, nc, row0, layer):
    rb0 = row0 // cs

    def rows(width, col):
        return pl.BlockSpec((cs, width), lambda b, c: (rb0 + b * nc + c, col))

    def const(shape):
        nd = len(shape)
        return pl.BlockSpec(shape, lambda b, c: (0,) * nd)

    def per_seq(shape, last=0):
        nd = len(shape)
        return pl.BlockSpec((None,) + shape, lambda b, c: (b,) + (0,) * (nd - 1) + (last,))

    def state_in(shape, last=0):
        nd = len(shape)
        return pl.BlockSpec((None, None) + shape, lambda b, c: (layer, b) + (0,) * (nd - 1) + (last,))

    return rows, const, per_seq, state_in


def _ssd_kernel(z_ref, x_ref, bc_ref, nar_ref, s0_ref, c0x_ref, c0bc_ref,
                cwx_ref, cbx_ref, cwbc_ref, cbbc_ref, bias_ref, alog_ref, dsk_ref, ng_ref,
                y_ref, s_out_ref, cx_out_ref, cbc_out_ref,
                st_scr, wx_scr, wbc_scr, ycat_scr, *, cs, nc):
    c = pl.program_id(1)
    lo = CONV_PAD - CONV_PREV

    @pl.when(c == 0)
    def _():
        st_scr[...] = s0_ref[...]
        wx_scr[lo:CONV_PAD, :] = c0x_ref[...]
        wbc_scr[lo:CONV_PAD, :] = c0bc_ref[...]

    xs = _silu(_conv_step(x_ref, wx_scr, cwx_ref, cbx_ref, cs))
    bcs = _silu(_conv_step(bc_ref, wbc_scr, cwbc_ref, cbbc_ref, cs))
    incl, _ = _causal_masks(cs)
    dt = jax.nn.softplus(nar_ref[...] + bias_ref[...])
    cum = _cumsum_rows(incl, dt * (-jnp.exp(alog_ref[...])))
    cum_t = cum.T
    exp_cum = jnp.exp(cum)
    cum_last = cum[cs - 1:cs, :]
    w_end = jnp.exp(cum_last - cum)
    chunk_decay = jnp.exp(cum_last)
    heads_per_group = SSD_HEADS // SSD_NGROUPS
    heads = range(SSD_HEADS)
    hp = SSD_HEAD_DIM
    s_old = [st_scr[h] for h in heads]
    col = lambda arr, h: arr[:, LANE_SSD_DT + h:LANE_SSD_DT + h + 1]
    b_gs = [bcs[:, g * SSD_STATE:(g + 1) * SSD_STATE] for g in range(SSD_NGROUPS)]
    c_gs = [bcs[:, (SSD_NGROUPS + g) * SSD_STATE:(SSD_NGROUPS + g + 1) * SSD_STATE] for g in range(SSD_NGROUPS)]
    cbs = [_dot_nt(c_gs[g], b_gs[g]) for g in range(SSD_NGROUPS)]
    grp = [h // heads_per_group for h in heads]
    scores = [cbs[grp[h]] * jnp.exp(jnp.where(incl, col(cum, h) - cum_t[LANE_SSD_DT + h:LANE_SSD_DT + h + 1, :],
                                              -jnp.inf)) for h in heads]
    xdts = [xs[:, h * hp:(h + 1) * hp] * col(dt, h) for h in heads]
    ys = [_dot(scores[h], xdts[h]) + _dot_nt(c_gs[grp[h]] * col(exp_cum, h), s_old[h]) for h in heads]
    s_new = [s_old[h] * col(chunk_decay, h) + _dot_tn(xdts[h] * col(w_end, h), b_gs[grp[h]]) for h in heads]
    for h in heads:
        st_scr[h] = s_new[h]
        ycat_scr[:, h * hp:(h + 1) * hp] = ys[h]
    y = (ycat_scr[...] + dsk_ref[...] * xs) * _silu(z_ref[...])
    gw = D_GROUP // SSD_NGROUPS
    for g in range(SSD_NGROUPS):
        y_g = _rms(y[:, g * gw:(g + 1) * gw], gw) * ng_ref[:, g * gw:(g + 1) * gw]
        y_ref[:, g * gw:(g + 1) * gw] = y_g.astype(y_ref.dtype)

    @pl.when(c == nc - 1)
    def _():
        s_out_ref[...] = st_scr[...]
        cx_out_ref[...] = _conv_tail(wx_scr, cs)
        cbc_out_ref[...] = _conv_tail(wbc_scr, cs)

    _conv_advance(wx_scr, cs)
    _conv_advance(wbc_scr, cs)


def _ssd_call(wide, narrow, s0, conv0, prm, *, n_seq, seq_len, cs, row0, layer):
    nc = seq_len // cs
    rows, const, per_seq, state_in = _seq_specs(cs, nc, row0, layer)
    xw, bcw = D_GROUP, SSD_BC_DIM
    x_off = WIDE_OFF[SEG_SSD_XBC]
    in_specs = [rows(xw, WIDE_OFF[SEG_SSD_Z] // xw), rows(xw, x_off // xw), rows(bcw, (x_off + xw) // bcw),
                rows(NARROW_COLS, 0),
                state_in((SSD_HEADS, SSD_HEAD_DIM, SSD_STATE)),
                state_in((CONV_PREV, xw)), state_in((CONV_PREV, bcw), last=xw // bcw),
                const((CONV_W, xw)), const((1, xw)), const((CONV_W, bcw)), const((1, bcw)),
                const((1, LANES)), const((1, LANES)), const((1, xw)), const((1, xw))]
    out_specs = [pl.BlockSpec((cs, xw), lambda b, c: (b * nc + c, 0)),
                 per_seq((SSD_HEADS, SSD_HEAD_DIM, SSD_STATE)),
                 per_seq((CONV_PREV, xw)), per_seq((CONV_PREV, bcw))]
    out_shape = [jax.ShapeDtypeStruct((n_seq * seq_len, xw), BF16),
                 jax.ShapeDtypeStruct((n_seq, SSD_HEADS, SSD_HEAD_DIM, SSD_STATE), F32),
                 jax.ShapeDtypeStruct((n_seq, CONV_PREV, xw), F32),
                 jax.ShapeDtypeStruct((n_seq, CONV_PREV, bcw), F32)]
    y, s_new, cx, cbc = pl.pallas_call(
        functools.partial(_ssd_kernel, cs=cs, nc=nc),
        grid=(n_seq, nc),
        in_specs=in_specs,
        out_specs=out_specs,
        out_shape=out_shape,
        scratch_shapes=[pltpu.VMEM((SSD_HEADS, SSD_HEAD_DIM, SSD_STATE), F32),
                        pltpu.VMEM((CONV_PAD + cs, xw), F32),
                        pltpu.VMEM((CONV_PAD + cs, bcw), F32),
                        pltpu.VMEM((cs, xw), F32)],
        compiler_params=_cparams("parallel", "arbitrary"),
        name="ssd_mixer",
    )(wide, wide, wide, narrow, s0, conv0, conv0,
      prm['cw'][:, :xw], prm['cb'][:, :xw], prm['cw'][:, xw:], prm['cb'][:, xw:],
      prm['bias'], prm['alog'], prm['dskip'], prm['ng'])
    return y, s_new, jnp.concatenate([cx, cbc], axis=-1)


def _ssd_params(conv_w, conv_b, dt_bias, a_log, d_skip, norm_g):
    pad = jnp.zeros((LANES - SSD_HEADS,), F32)
    return dict(cw=conv_w, cb=conv_b[None, :],
                bias=jnp.concatenate([dt_bias, pad])[None, :],
                alog=jnp.concatenate([a_log, pad])[None, :],
                dskip=jnp.repeat(d_skip, SSD_HEAD_DIM)[None, :], ng=norm_g[None, :])


def _split_hi_lo(x):
    hi = x.astype(BF16).astype(F32)
    lo = (x - hi).astype(BF16).astype(F32)
    return hi, lo


def _lhs3(a):
    hi, lo = _split_hi_lo(a)
    return jnp.concatenate([hi, lo, hi], axis=1).astype(BF16)


def _rhs3(b):
    hi, lo = _split_hi_lo(b)
    return jnp.concatenate([hi, hi, lo], axis=0).astype(BF16)


def _dot3(lhs3, rhs3):
    return jnp.dot(lhs3, rhs3, preferred_element_type=F32)


def _inv_unit_lower(a_list, cs):
    row = lax.broadcasted_iota(jnp.int32, (cs, cs), 0)
    col = lax.broadcasted_iota(jnp.int32, (cs, cs), 1)
    eye = jnp.where(row == col, 1.0, 0.0)
    ps = [-a for a in a_list]
    ts = [eye + p for p in ps]
    forms = [(_lhs3(p), _rhs3(p)) for p in ps]
    n = 1
    while 2 * n < cs:
        ps = [_dot3(lhs, rhs) for lhs, rhs in forms]
        forms = [(_lhs3(p), _rhs3(p)) for p in ps]
        ts = [t + _dot3(_lhs3(t), rhs) for t, (_, rhs) in zip(ts, forms)]
        n *= 2
    return ts


def _gdn_kernel(qkv_ref, z_ref, nar_ref, s0_ref, c0_ref, cw_ref, cb_ref, bias_ref, alog_ref, ng_ref,
                y_ref, s_out_ref, c_out_ref, st_scr, win_scr, *, cs, nc):
    c = pl.program_id(1)

    @pl.when(c == 0)
    def _():
        st_scr[...] = s0_ref[...]
        win_scr[CONV_PAD - CONV_PREV:CONV_PAD, :] = c0_ref[...]

    qkv = _silu(_conv_step(qkv_ref, win_scr, cw_ref, cb_ref, cs))
    incl, strict = _causal_masks(cs)
    nar = nar_ref[...]
    g_log = -jnp.exp(alog_ref[...]) * jax.nn.softplus(nar + bias_ref[...])
    beta_all = jax.nn.sigmoid(nar)
    gc = _cumsum_rows(incl, g_log)
    gc_t = gc.T
    exp_gc = jnp.exp(gc)
    gc_last = gc[cs - 1:cs, :]
    exp_to_end = jnp.exp(gc_last - gc)
    g_last = jnp.exp(gc_last)
    hd = GDN_HEAD_DIM
    heads = range(GDN_HEADS)
    s_old = [st_scr[h] for h in heads]
    col = lambda arr, lane: arr[:, lane:lane + 1]
    qs = [qkv[:, h * hd:(h + 1) * hd] for h in heads]
    ks = [qkv[:, D_GROUP + h * hd:D_GROUP + (h + 1) * hd] for h in heads]
    vs = [qkv[:, 2 * D_GROUP + h * hd:2 * D_GROUP + (h + 1) * hd] for h in heads]
    qs = [q * lax.rsqrt(jnp.sum(q * q, axis=-1, keepdims=True) + EPS) * (hd ** -0.5) for q in qs]
    ks = [k * lax.rsqrt(jnp.sum(k * k, axis=-1, keepdims=True) + EPS) for k in ks]
    decays = [jnp.exp(jnp.where(incl, col(gc, LANE_GDN_A + h) - gc_t[LANE_GDN_A + h:LANE_GDN_A + h + 1, :],
                                -jnp.inf)) for h in heads]
    betas = [col(beta_all, LANE_GDN_B + h) for h in heads]
    a_mats = [jnp.where(strict, betas[h] * _dot_nt(ks[h], ks[h]) * decays[h], 0.0) for h in heads]
    t_invs = _inv_unit_lower(a_mats, cs)
    rhs = [jnp.concatenate([vs[h] * betas[h], ks[h] * (betas[h] * col(exp_gc, LANE_GDN_A + h))], axis=1)
           for h in heads]
    uw = [_dot3(_lhs3(t_invs[h]), _rhs3(rhs[h])) for h in heads]
    qks = [_dot_nt(qs[h], ks[h]) * decays[h] for h in heads]
    v_new = [uw[h][:, :hd] - _dot(uw[h][:, hd:], s_old[h]) for h in heads]
    outs = [_dot(qs[h] * col(exp_gc, LANE_GDN_A + h), s_old[h]) + _dot(qks[h], v_new[h]) for h in heads]
    s_new = [s_old[h] * col(g_last, LANE_GDN_A + h) + _dot_tn(ks[h] * col(exp_to_end, LANE_GDN_A + h), v_new[h])
             for h in heads]
    for h in heads:
        st_scr[h] = s_new[h]
    y = jnp.concatenate([_rms(o, hd) for o in outs], axis=1) * ng_ref[...] * _silu(z_ref[...])
    y_ref[...] = y.astype(y_ref.dtype)

    @pl.when(c == nc - 1)
    def _():
        s_out_ref[...] = st_scr[...]
        c_out_ref[...] = _conv_tail(win_scr, cs)

    _conv_advance(win_scr, cs)


def _gdn_call(wide, narrow, s0, conv0, prm, *, n_seq, seq_len, cs, row0, layer):
    nc = seq_len // cs
    rows, const, per_seq, state_in = _seq_specs(cs, nc, row0, layer)
    hd = GDN_HEAD_DIM
    in_specs = [rows(GDN_CONV_DIM, WIDE_OFF[SEG_GDN_QKV] // GDN_CONV_DIM),
                rows(D_GROUP, WIDE_OFF[SEG_GDN_Z] // D_GROUP),
                rows(NARROW_COLS, 0),
                state_in((GDN_HEADS, hd, hd)), state_in((CONV_PREV, GDN_CONV_DIM)),
                const((CONV_W, GDN_CONV_DIM)), const((1, GDN_CONV_DIM)),
                const((1, LANES)), const((1, LANES)), const((1, D_GROUP))]
    out_specs = [pl.BlockSpec((cs, D_GROUP), lambda b, c: (b * nc + c, 0)),
                 per_seq((GDN_HEADS, hd, hd)), per_seq((CONV_PREV, GDN_CONV_DIM))]
    out_shape = [jax.ShapeDtypeStruct((n_seq * seq_len, D_GROUP), BF16),
                 jax.ShapeDtypeStruct((n_seq, GDN_HEADS, hd, hd), F32),
                 jax.ShapeDtypeStruct((n_seq, CONV_PREV, GDN_CONV_DIM), F32)]
    return pl.pallas_call(
        functools.partial(_gdn_kernel, cs=cs, nc=nc),
        grid=(n_seq, nc),
        in_specs=in_specs,
        out_specs=out_specs,
        out_shape=out_shape,
        scratch_shapes=[pltpu.VMEM((GDN_HEADS, hd, hd), F32),
                        pltpu.VMEM((CONV_PAD + cs, GDN_CONV_DIM), F32)],
        compiler_params=_cparams("parallel", "arbitrary"),
        name="gdn_mixer",
    )(wide, wide, narrow, s0, conv0, prm['cw'], prm['cb'], prm['bias'], prm['alog'], prm['ng'])


def _lane_row(vec, lane0):
    return jnp.zeros((LANES,), F32).at[lane0:lane0 + vec.shape[0]].set(vec)[None, :]


def _gdn_params(conv_w, conv_b, dt_bias, a_log, norm_g):
    return dict(cw=conv_w, cb=conv_b[None, :], bias=_lane_row(dt_bias, LANE_GDN_A),
                alog=_lane_row(a_log, LANE_GDN_A), ng=norm_g[None, :])


def _mlstm_kernel(qkv_ref, o_ref, nar_ref, c0_ref, n0_ref, m0_ref, bias_ref, ng_ref,
                  y_ref, c_out_ref, n_out_ref, m_out_ref, c_scr, n_scr, m_scr, *, cs, nc):
    c = pl.program_id(1)

    @pl.when(c == 0)
    def _():
        c_scr[...] = c0_ref[...]
        n_scr[...] = n0_ref[...]
        m_scr[...] = m0_ref[...]

    incl, _ = _causal_masks(cs)
    pre = nar_ref[...] + bias_ref[...]
    f_cum = _cumsum_rows(incl, jax.nn.log_sigmoid(pre))
    f_cum_t = f_cum.T
    pre_t = pre.T
    hd = MLSTM_HEAD_DIM
    heads = range(MLSTM_HEADS)
    c_old = [c_scr[h] for h in heads]
    n_all = n_scr[...]
    m_all = m_scr[...]
    qkv = qkv_ref[...]
    col = lambda arr, lane: arr[:, lane:lane + 1]
    qs = [qkv[:, h * hd:(h + 1) * hd] for h in heads]
    ks = [qkv[:, D_GROUP + h * hd:D_GROUP + (h + 1) * hd] * (hd ** -0.5) for h in heads]
    vs = [qkv[:, 2 * D_GROUP + h * hd:2 * D_GROUP + (h + 1) * hd] for h in heads]
    f_cols = [col(f_cum, LANE_ML_F + h) for h in heads]
    d_logs = [jnp.where(incl, f_cols[h] - f_cum_t[LANE_ML_F + h:LANE_ML_F + h + 1, :]
                        + pre_t[LANE_ML_I + h:LANE_ML_I + h + 1, :], -jnp.inf) for h in heads]
    m_prev = [col(m_all, LANE_ML_I + h) for h in heads]
    m_t = [jnp.maximum(f_cols[h] + m_prev[h], jnp.max(d_logs[h], axis=-1, keepdims=True)) for h in heads]
    w_carry = [jnp.exp(f_cols[h] + m_prev[h] - m_t[h]) for h in heads]
    ps = [jnp.exp(d_logs[h] - m_t[h]) * _dot_nt(qs[h], ks[h]) for h in heads]
    num = [w_carry[h] * _dot(qs[h], c_old[h]) + _dot(ps[h], vs[h]) for h in heads]
    den = [w_carry[h] * jnp.sum(qs[h] * n_all[h:h + 1, :], axis=-1, keepdims=True)
           + jnp.sum(ps[h], axis=-1, keepdims=True) for h in heads]
    hs = [num[h] / jnp.maximum(jnp.abs(den[h]), jnp.exp(-m_t[h])) for h in heads]
    m_end = [m_t[h][cs - 1:cs, :] for h in heads]
    f_last = [f_cols[h][cs - 1:cs, :] for h in heads]
    w_prev = [jnp.exp(f_last[h] + m_prev[h] - m_end[h]) for h in heads]
    kws = [ks[h] * jnp.exp(f_last[h] - f_cols[h] + col(pre, LANE_ML_I + h) - m_end[h]) for h in heads]
    c_new = [w_prev[h] * c_old[h] + _dot_tn(kws[h], vs[h]) for h in heads]
    n_new = [w_prev[h] * n_all[h:h + 1, :] + jnp.sum(kws[h], axis=0, keepdims=True) for h in heads]
    lane = lax.broadcasted_iota(jnp.int32, m_all.shape, 1)
    m_new = m_all
    for h in heads:
        c_scr[h] = c_new[h]
        m_new = jnp.where(lane == LANE_ML_I + h, m_end[h], m_new)
    n_scr[...] = jnp.concatenate(n_new, axis=0)
    m_scr[...] = m_new
    y = jnp.concatenate([_rms(hh, hd) for hh in hs], axis=1) * ng_ref[...] * jax.nn.sigmoid(o_ref[...])
    y_ref[...] = y.astype(y_ref.dtype)

    @pl.when(c == nc - 1)
    def _():
        c_out_ref[...] = c_scr[...]
        n_out_ref[...] = n_scr[...]
        m_out_ref[...] = m_scr[...]


def _mlstm_call(wide, narrow, c0, n0, m0, prm, *, n_seq, seq_len, cs, row0, layer):
    nc = seq_len // cs
    rows, const, per_seq, state_in = _seq_specs(cs, nc, row0, layer)
    hd = MLSTM_HEAD_DIM
    qkv_w = 3 * D_GROUP
    in_specs = [rows(qkv_w, WIDE_OFF[SEG_ML_QKV] // qkv_w),
                rows(D_GROUP, WIDE_OFF[SEG_ML_O] // D_GROUP),
                rows(NARROW_COLS, 0),
                state_in((MLSTM_HEADS, hd, hd)), state_in((MLSTM_HEADS, hd)), state_in((1, LANES)),
                const((1, LANES)), const((1, D_GROUP))]
    out_specs = [pl.BlockSpec((cs, D_GROUP), lambda b, c: (b * nc + c, 0)),
                 per_seq((MLSTM_HEADS, hd, hd)), per_seq((MLSTM_HEADS, hd)), per_seq((1, LANES))]
    out_shape = [jax.ShapeDtypeStruct((n_seq * seq_len, D_GROUP), BF16),
                 jax.ShapeDtypeStruct((n_seq, MLSTM_HEADS, hd, hd), F32),
                 jax.ShapeDtypeStruct((n_seq, MLSTM_HEADS, hd), F32),
                 jax.ShapeDtypeStruct((n_seq, 1, LANES), F32)]
    return pl.pallas_call(
        functools.partial(_mlstm_kernel, cs=cs, nc=nc),
        grid=(n_seq, nc),
        in_specs=in_specs,
        out_specs=out_specs,
        out_shape=out_shape,
        scratch_shapes=[pltpu.VMEM((MLSTM_HEADS, hd, hd), F32),
                        pltpu.VMEM((MLSTM_HEADS, hd), F32),
                        pltpu.VMEM((1, LANES), F32)],
        compiler_params=_cparams("parallel", "arbitrary"),
        name="mlstm_mixer",
    )(wide, wide, narrow, c0, n0, m0, prm['bias'], prm['ng'])


def _mlstm_params(i_bias, f_bias, norm_g):
    return dict(bias=_lane_row(i_bias, LANE_ML_I) + _lane_row(f_bias, LANE_ML_F), ng=norm_g[None, :])


def _lru_kernel(x_ref, gate_ref, h0_ref, c0_ref, cw_ref, cb_ref, wa_ref, ba_ref, wx_ref, bx_ref, lam_ref,
                y_ref, h_out_ref, c_out_ref, h_scr, win_scr, *, cs, nc):
    c = pl.program_id(1)

    @pl.when(c == 0)
    def _():
        h_scr[...] = h0_ref[...]
        win_scr[CONV_PAD - CONV_PREV:CONV_PAD, :] = c0_ref[...]

    xc = _conv_step(x_ref, win_scr, cw_ref, cb_ref, cs)
    r_parts, i_parts = [], []
    for g in range(LRU_BLOCKS):
        x_g = xc[:, g * LRU_BLOCK:(g + 1) * LRU_BLOCK].astype(BF16)
        r_parts.append(jnp.dot(x_g, wa_ref[g], preferred_element_type=F32))
        i_parts.append(jnp.dot(x_g, wx_ref[g], preferred_element_type=F32))
    r = jax.nn.sigmoid(jnp.concatenate(r_parts, axis=-1) + ba_ref[...])
    i = jax.nn.sigmoid(jnp.concatenate(i_parts, axis=-1) + bx_ref[...])
    log_a = -LRU_C * r * jax.nn.softplus(-lam_ref[...])
    a = jnp.exp(log_a)
    u = jnp.sqrt(jnp.tanh(-log_a) * (a * a + 1.0)) * (i * xc)
    row = lax.broadcasted_iota(jnp.int32, (cs, D_GROUP), 0)
    u = u + jnp.where(row == 0, a * h_scr[...], 0.0)
    shift = 1
    while shift < cs:
        a_sh = pltpu.roll(a, shift, 0)
        u_sh = pltpu.roll(u, shift, 0)
        live = row >= shift
        u = jnp.where(live, a * u_sh + u, u)
        a = jnp.where(live, a * a_sh, a)
        shift *= 2
    h_scr[...] = u[cs - 1:cs, :]
    y_ref[...] = (u * jax.nn.gelu(gate_ref[...])).astype(y_ref.dtype)

    @pl.when(c == nc - 1)
    def _():
        h_out_ref[...] = u[cs - 1:cs, :]
        c_out_ref[...] = _conv_tail(win_scr, cs)

    _conv_advance(win_scr, cs)


def _lru_call(wide, h0, conv0, prm, *, n_seq, seq_len, cs, row0, layer):
    nc = seq_len // cs
    rows, const, per_seq, state_in = _seq_specs(cs, nc, row0, layer)
    in_specs = [rows(D_GROUP, WIDE_OFF[SEG_LRU_X] // D_GROUP), rows(D_GROUP, WIDE_OFF[SEG_LRU_GATE] // D_GROUP),
                state_in((1, D_GROUP)), state_in((CONV_PREV, D_GROUP)),
                const((CONV_W, D_GROUP)), const((1, D_GROUP)),
                const((LRU_BLOCKS, LRU_BLOCK, LRU_BLOCK)), const((1, D_GROUP)),
                const((LRU_BLOCKS, LRU_BLOCK, LRU_BLOCK)), const((1, D_GROUP)), const((1, D_GROUP))]
    out_specs = [pl.BlockSpec((cs, D_GROUP), lambda b, c: (b * nc + c, 0)),
                 per_seq((1, D_GROUP)), per_seq((CONV_PREV, D_GROUP))]
    out_shape = [jax.ShapeDtypeStruct((n_seq * seq_len, D_GROUP), BF16),
                 jax.ShapeDtypeStruct((n_seq, 1, D_GROUP), F32),
                 jax.ShapeDtypeStruct((n_seq, CONV_PREV, D_GROUP), F32)]
    return pl.pallas_call(
        functools.partial(_lru_kernel, cs=cs, nc=nc),
        grid=(n_seq, nc),
        in_specs=in_specs,
        out_specs=out_specs,
        out_shape=out_shape,
        scratch_shapes=[pltpu.VMEM((1, D_GROUP), F32), pltpu.VMEM((CONV_PAD + cs, D_GROUP), F32)],
        compiler_params=_cparams("parallel", "arbitrary"),
        name="rglru_mixer",
    )(wide, wide, h0, conv0, prm['cw'], prm['cb'], prm['wa'], prm['ba'], prm['wx'], prm['bx'], prm['lam'])


def _lru_params(conv_w, conv_b, w_a, b_a, w_x, b_x, lam):
    return dict(cw=conv_w, cb=conv_b[None, :], wa=w_a.astype(BF16), ba=b_a.reshape(1, D_GROUP),
                wx=w_x.astype(BF16), bx=b_x.reshape(1, D_GROUP), lam=lam[None, :])


def _kernel_states(st):
    s_ssd, s_ssd_conv, s_gdn, s_gdn_conv, s_mc, s_mn, s_mm, s_lru, s_lru_conv = st
    lead = s_mm.shape[:2]
    m_rows = jnp.zeros(lead + (1, LANES), F32).at[:, :, 0, LANE_ML_I:LANE_ML_I + MLSTM_HEADS].set(s_mm)
    return (s_ssd, s_ssd_conv, s_gdn, s_gdn_conv, s_mc, s_mn, m_rows, s_lru[:, :, None, :], s_lru_conv)


def _mixer_group(wide, narrow, kst, prm, *, n_seq, seq_len, row0, layer):
    s_ssd, s_ssd_conv, s_gdn, s_gdn_conv, s_mc, s_mn, m_rows, s_lru, s_lru_conv = kst
    cs = min(seq_len, CHUNK)
    kw = dict(n_seq=n_seq, seq_len=seq_len, cs=cs, row0=row0, layer=layer)
    y_a, ssd_new, ssd_conv_new = _ssd_call(wide, narrow, s_ssd, s_ssd_conv, prm['ssd'], **kw)
    y_b, gdn_new, gdn_conv_new = _gdn_call(wide, narrow, s_gdn, s_gdn_conv, prm['gdn'], **kw)
    y_c, mc_new, mn_new, mm_new = _mlstm_call(wide, narrow, s_mc, s_mn, m_rows, prm['mlstm'], **kw)
    y_d, lru_new, lru_conv_new = _lru_call(wide, s_lru, s_lru_conv, prm['lru'], **kw)
    new = (ssd_new, ssd_conv_new, gdn_new, gdn_conv_new, mc_new, mn_new,
           mm_new[:, 0, LANE_ML_I:LANE_ML_I + MLSTM_HEADS], lru_new[:, 0, :], lru_conv_new)
    return (y_a, y_b, y_c, y_d), new


def _state_shapes(n):
    return ((n, SSD_HEADS, SSD_HEAD_DIM, SSD_STATE),
            (n, CONV_PREV, SSD_CONV_DIM),
            (n, GDN_HEADS, GDN_HEAD_DIM, GDN_HEAD_DIM),
            (n, CONV_PREV, GDN_CONV_DIM),
            (n, MLSTM_HEADS, MLSTM_HEAD_DIM, MLSTM_HEAD_DIM),
            (n, MLSTM_HEADS, MLSTM_HEAD_DIM),
            (n, MLSTM_HEADS),
            (n, D_GROUP),
            (n, CONV_PREV, D_GROUP))


def kernel(x_prompt, x_sample, state_ssd, state_ssd_conv, state_gdn, state_gdn_conv, state_mlstm_c, state_mlstm_n, state_mlstm_m, state_rglru, state_rglru_conv, c_prompt, c_sample, w_ada, ada_table, norm1_g, norm2_g, final_g, w_in, w_out, ssd_conv_w, ssd_conv_b, ssd_dt_bias, ssd_a_log, ssd_d, ssd_norm_g, gdn_conv_w, gdn_conv_b, gdn_dt_bias, gdn_a_log, gdn_norm_g, mlstm_i_bias, mlstm_f_bias, mlstm_norm_g, lru_conv_w, lru_conv_b, lru_w_a, lru_b_a, lru_w_x, lru_b_x, lru_lambda, moe_w_group, moe_b_group, moe_w_expert, moe_b_expert, moe_w_gate, moe_w_up, moe_w_down):
    d = D_MODEL

    w_in_wide = jnp.concatenate(
        [w_in[:, :, IN_OFFSETS[s]:IN_OFFSETS[s + 1]].astype(BF16) for s in _WIDE_SEGS], axis=-1)
    w_in_narrow = jnp.concatenate(
        [w_in[:, :, IN_OFFSETS[s]:IN_OFFSETS[s + 1]].astype(BF16) for s in _NARROW_SEGS]
        + [jnp.zeros((DEPTH, d, NARROW_COLS - sum(IN_SIZES[s] for s in _NARROW_SEGS)), BF16)], axis=-1)
    w_out_b = w_out.astype(BF16)
    w_gate_b = moe_w_gate.astype(BF16)
    w_up_b = moe_w_up.astype(BF16)
    w_down_b = moe_w_down.astype(BF16)
    router_pad = jnp.zeros((DEPTH, d, ROUTER_LANES - N_EXPERT_GROUPS - N_EXPERTS), F32)
    w_router = jnp.concatenate([moe_w_group, moe_w_expert, router_pad], axis=-1)
    b_router = jnp.concatenate([moe_b_group, moe_b_expert, router_pad[:, 0, :]], axis=-1)[:, None, :]

    c_pad = jnp.concatenate([c_prompt, c_sample, jnp.zeros((ADA_ROWS - BATCH - DEC_BATCH, d), F32)], axis=0)
    mod_shared = _ada_matmul(c_pad, w_ada)
    mod_tiles = jnp.concatenate(
        [jnp.broadcast_to(mod_shared[:BATCH, None, :], (BATCH, ROW_TILE, 6 * d)),
         jnp.repeat(mod_shared[BATCH:BATCH + DEC_BATCH], DEC_SEQ, axis=0).reshape(SAMPLE_ROW_TILES, ROW_TILE, 6 * d)],
        axis=0)
    ada_rows = ada_table.reshape(DEPTH * 6, 1, d)

    x = jnp.concatenate([x_prompt.reshape(N_PROMPT_TOK, d), x_sample.reshape(N_SAMPLE_TOK, d)], axis=0)

    st_sample = (state_ssd, state_ssd_conv, state_gdn, state_gdn_conv, state_mlstm_c,
                 state_mlstm_n, state_mlstm_m, state_rglru, state_rglru_conv)
    kst_sample = _kernel_states(st_sample)
    kst_prompt = _kernel_states(tuple(jnp.zeros((1,) + s, F32) for s in _state_shapes(BATCH)))
    new_p = [[] for _ in st_sample]
    new_s = [[] for _ in st_sample]

    y_prev = None
    for l in range(DEPTH):
        if l == 0:
            (h,) = _rowwise(x, norm1_g[l][None, :], mod_tiles=mod_tiles, ada_rows=ada_rows, layer=l,
                            mod_chunks=(1, 0), name="norm1")
        else:
            x, h = _rowwise(x, norm1_g[l][None, :], y=y_prev, mod_tiles=mod_tiles, ada_rows=ada_rows,
                            layer=l, gate_chunk=5, gate_layer=l - 1, mod_chunks=(1, 0), name="resid_norm1")
        wide = _matmul(h, w_in_wide[l], MM_TN, name="in_proj")
        narrow = _matmul(h, w_in_narrow[l], NARROW_COLS, name="in_proj_gates")

        prm = dict(ssd=_ssd_params(ssd_conv_w[l], ssd_conv_b[l], ssd_dt_bias[l], ssd_a_log[l], ssd_d[l],
                                   ssd_norm_g[l]),
                   gdn=_gdn_params(gdn_conv_w[l], gdn_conv_b[l], gdn_dt_bias[l], gdn_a_log[l], gdn_norm_g[l]),
                   mlstm=_mlstm_params(mlstm_i_bias[l], mlstm_f_bias[l], mlstm_norm_g[l]),
                   lru=_lru_params(lru_conv_w[l], lru_conv_b[l], lru_w_a[l], lru_b_a[l], lru_w_x[l], lru_b_x[l],
                                   lru_lambda[l]))
        ys_p, st_p = _mixer_group(wide, narrow, kst_prompt, prm, n_seq=BATCH, seq_len=SEQ, row0=0, layer=0)
        ys_s, st_s = _mixer_group(wide, narrow, kst_sample, prm,
                                  n_seq=DEC_BATCH, seq_len=DEC_SEQ, row0=N_PROMPT_TOK, layer=l)
        for acc, s in zip(new_p, st_p):
            acc.append(s)
        for acc, s in zip(new_s, st_s):
            acc.append(s)
        y_mix = _out_proj(ys_p, ys_s, w_out_b[l])

        x, h2, gates = _rowwise(x, norm2_g[l][None, :], y=y_mix, mod_tiles=mod_tiles, ada_rows=ada_rows,
                                layer=l, gate_chunk=2, mod_chunks=(4, 3),
                                router=(w_router[l], b_router[l]), name="resid_norm2_router")
        src, valid, dest, tile_ea, tile_eb, n_used = _moe_plan(gates)
        x_sorted = _take_rows(h2, src)
        gates_sorted = _take_rows(gates, src) * valid[:, None]
        y_sorted = _moe(x_sorted, gates_sorted, tile_ea, tile_eb, n_used, w_gate_b, w_up_b, w_down_b, l)
        y_prev = _take_rows(y_sorted, dest)

    x, y_fin = _rowwise(x, final_g[None, :], y=y_prev, mod_tiles=mod_tiles, ada_rows=ada_rows,
                        layer=DEPTH - 1, gate_chunk=5, h_dtype=F32, name="resid_final_norm")
    outs = [y_fin[:N_PROMPT_TOK].reshape(BATCH, SEQ, d), y_fin[N_PROMPT_TOK:].reshape(DEC_BATCH, DEC_SEQ, d)]
    for acc_p, acc_s in zip(new_p, new_s):
        outs.append(jnp.stack(acc_p))
        outs.append(jnp.stack(acc_s))
    return tuple(outs)
```

```python
import functools

import numpy as np
import jax
import jax.numpy as jnp
from jax import lax
from jax.experimental import pallas as pl
from jax.experimental.pallas import tpu as pltpu

F32 = jnp.float32
BF16 = jnp.bfloat16

D_MODEL = 4096
BATCH = 4
SEQ = 2048
DEPTH = 4
DEC_BATCH = 128
DEC_SEQ = 8
EPS = 1e-6
N_MIXERS = 4
D_GROUP = D_MODEL // N_MIXERS
CONV_W = 4
CHUNK = 64
SSD_HEAD_DIM = 64
SSD_HEADS = D_GROUP // SSD_HEAD_DIM
SSD_NGROUPS = 2
SSD_STATE = 128
SSD_BC_DIM = 2 * SSD_NGROUPS * SSD_STATE
SSD_CONV_DIM = D_GROUP + SSD_BC_DIM
GDN_HEAD_DIM = 128
GDN_HEADS = D_GROUP // GDN_HEAD_DIM
GDN_CONV_DIM = 3 * D_GROUP
MLSTM_HEAD_DIM = 128
MLSTM_HEADS = D_GROUP // MLSTM_HEAD_DIM
LRU_BLOCKS = 8
LRU_BLOCK = D_GROUP // LRU_BLOCKS
LRU_C = 8.0
N_EXPERT_GROUPS = 4
EXPERTS_PER_GROUP = 4
N_EXPERTS = N_EXPERT_GROUPS * EXPERTS_PER_GROUP
D_FF_EXPERT = D_MODEL // 8
IN_SIZES = (D_GROUP, SSD_CONV_DIM, SSD_HEADS,
            GDN_CONV_DIM, GDN_HEADS, GDN_HEADS, D_GROUP,
            3 * D_GROUP, MLSTM_HEADS, MLSTM_HEADS, D_GROUP,
            D_GROUP, D_GROUP)
IN_OFFSETS = tuple(int(o) for o in np.cumsum((0,) + IN_SIZES))
(SEG_SSD_Z, SEG_SSD_XBC, SEG_SSD_DT, SEG_GDN_QKV, SEG_GDN_A, SEG_GDN_B, SEG_GDN_Z,
 SEG_ML_QKV, SEG_ML_I, SEG_ML_F, SEG_ML_O, SEG_LRU_X, SEG_LRU_GATE) = range(len(IN_SIZES))

N_PROMPT_TOK = BATCH * SEQ
N_SAMPLE_TOK = DEC_BATCH * DEC_SEQ
N_TOK = N_PROMPT_TOK + N_SAMPLE_TOK

LANES = 128
SUBLANES = 8
VMEM_LIMIT_BYTES = 56 * 1024 * 1024

ROW_TILE = 256
MOD_REPEAT = DEC_SEQ
MOD_ROWS = ROW_TILE // MOD_REPEAT
N_ROW_TILES = N_TOK // ROW_TILE
PROMPT_TILES_PER_SEQ = SEQ // ROW_TILE
PROMPT_ROW_TILES = N_PROMPT_TOK // ROW_TILE
SAMPLE_ROW_TILES = N_SAMPLE_TOK // ROW_TILE
MM_TM = 1024
MM_TN = 512
ADA_ROWS = 144
ADA_TN = 512
MOE_TM = 512
MOE_FF_TILE = 256
MOE_FF_STEPS = D_FF_EXPERT // MOE_FF_TILE
TOP_K_IN_GROUP = 2
PAIRS = tuple((a, b) for a in range(EXPERTS_PER_GROUP) for b in range(a + 1, EXPERTS_PER_GROUP))
N_PAIRS = len(PAIRS)
N_BUCKETS = N_EXPERT_GROUPS * N_PAIRS
MOE_STEPS_PER_TILE = TOP_K_IN_GROUP * MOE_FF_STEPS
MOE_MAX_TILES = N_TOK // MOE_TM + N_BUCKETS
MOE_SLOTS = MOE_MAX_TILES * MOE_TM
ROUTER_LANES = LANES
EXPERT_LANE0 = N_EXPERT_GROUPS
BUCKET_LANE = 0

_WIDE_SEGS = (SEG_GDN_QKV, SEG_ML_QKV, SEG_SSD_Z, SEG_GDN_Z, SEG_ML_O, SEG_LRU_X, SEG_LRU_GATE, SEG_SSD_XBC)
_NARROW_SEGS = (SEG_SSD_DT, SEG_GDN_A, SEG_GDN_B, SEG_ML_I, SEG_ML_F)
WIDE_OFF = {}
_o = 0
for _s in _WIDE_SEGS:
    WIDE_OFF[_s] = _o
    _o += IN_SIZES[_s]
WIDE_COLS = _o
NARROW_OFF = {}
_o = 0
for _s in _NARROW_SEGS:
    NARROW_OFF[_s] = _o
    _o += IN_SIZES[_s]
NARROW_COLS = LANES
LANE_SSD_DT = NARROW_OFF[SEG_SSD_DT]
LANE_GDN_A = NARROW_OFF[SEG_GDN_A]
LANE_GDN_B = NARROW_OFF[SEG_GDN_B]
LANE_ML_I = NARROW_OFF[SEG_ML_I]
LANE_ML_F = NARROW_OFF[SEG_ML_F]

CONV_PAD = SUBLANES
CONV_PREV = CONV_W - 1


def _cparams(*sem):
    return pltpu.CompilerParams(dimension_semantics=sem, vmem_limit_bytes=VMEM_LIMIT_BYTES)


def _dot(a, b):
    return jnp.dot(a.astype(BF16), b.astype(BF16), preferred_element_type=F32)


def _dot_nt(a, b):
    return lax.dot_general(a.astype(BF16), b.astype(BF16), (((1,), (1,)), ((), ())), preferred_element_type=F32)


def _dot_tn(a, b):
    return lax.dot_general(a.astype(BF16), b.astype(BF16), (((0,), (0,)), ((), ())), preferred_element_type=F32)


def _dot_f32(a, b):
    return jnp.dot(a, b, preferred_element_type=F32, precision=lax.Precision.HIGHEST)


def _silu(x):
    return x * jax.nn.sigmoid(x)


def _mm_kernel(a_ref, w_ref, o_ref):
    o_ref[...] = jnp.dot(a_ref[...], w_ref[...], preferred_element_type=F32).astype(o_ref.dtype)


def _matmul(a, w, tn, name):
    m, k = a.shape
    n = w.shape[1]
    assert m % MM_TM == 0 and n % tn == 0
    return pl.pallas_call(
        _mm_kernel,
        grid=(m // MM_TM, n // tn),
        in_specs=[pl.BlockSpec((MM_TM, k), lambda i, j: (i, 0)),
                  pl.BlockSpec((k, tn), lambda i, j: (0, j))],
        out_specs=pl.BlockSpec((MM_TM, tn), lambda i, j: (i, j)),
        out_shape=jax.ShapeDtypeStruct((m, n), F32),
        compiler_params=_cparams("parallel", "arbitrary"),
        name=name,
    )(a, w)


PROMPT_MM_TILES = N_PROMPT_TOK // MM_TM
assert N_SAMPLE_TOK == MM_TM


def _out_proj_kernel(*refs):
    yp_refs, ys_refs = refs[:N_MIXERS], refs[N_MIXERS:2 * N_MIXERS]
    w_refs, o_ref = refs[2 * N_MIXERS:3 * N_MIXERS], refs[3 * N_MIXERS]

    def project(y_refs):
        acc = jnp.dot(y_refs[0][...], w_refs[0][...], preferred_element_type=F32)
        for m in range(1, N_MIXERS):
            acc += jnp.dot(y_refs[m][...], w_refs[m][...], preferred_element_type=F32)
        o_ref[...] = acc

    is_prompt = pl.program_id(0) < PROMPT_MM_TILES
    pl.when(is_prompt)(lambda: project(yp_refs))
    pl.when(jnp.logical_not(is_prompt))(lambda: project(ys_refs))


def _out_proj(ys_prompt, ys_sample, w):
    n = w.shape[1]
    yp_spec = pl.BlockSpec((MM_TM, D_GROUP), lambda i, j: (jnp.minimum(i, PROMPT_MM_TILES - 1), 0))
    ys_spec = pl.BlockSpec((MM_TM, D_GROUP), lambda i, j: (0, 0))
    w_specs = [pl.BlockSpec((D_GROUP, MM_TN), functools.partial(lambda i, j, m: (m, j), m=m))
               for m in range(N_MIXERS)]
    return pl.pallas_call(
        _out_proj_kernel,
        grid=(N_TOK // MM_TM, n // MM_TN),
        in_specs=[yp_spec] * N_MIXERS + [ys_spec] * N_MIXERS + w_specs,
        out_specs=pl.BlockSpec((MM_TM, MM_TN), lambda i, j: (i, j)),
        out_shape=jax.ShapeDtypeStruct((N_TOK, n), F32),
        compiler_params=_cparams("parallel", "arbitrary"),
        name="out_proj",
    )(*ys_prompt, *ys_sample, w, w, w, w)


def _ada_kernel(c_ref, w_ref, o_ref):
    o_ref[...] = _dot(_silu(c_ref[...]), w_ref[...])


def _ada_matmul(c_pad, w_ada):
    k, n = w_ada.shape
    return pl.pallas_call(
        _ada_kernel,
        grid=(n // ADA_TN,),
        in_specs=[pl.BlockSpec((ADA_ROWS, k), lambda j: (0, 0)),
                  pl.BlockSpec((k, ADA_TN), lambda j: (0, j))],
        out_specs=pl.BlockSpec((ADA_ROWS, ADA_TN), lambda j: (0, j)),
        out_shape=jax.ShapeDtypeStruct((ADA_ROWS, n), F32),
        compiler_params=_cparams("arbitrary"),
        name="adaln_matmul",
    )(c_pad, w_ada)


def _route(lg):
    lane = lax.broadcasted_iota(jnp.int32, lg.shape, 1).astype(F32)
    neg = -jnp.inf
    big = float(ROUTER_LANES)
    gl = jnp.where(lane < N_EXPERT_GROUPS, lg, neg)
    gmax = jnp.max(gl, axis=-1, keepdims=True)
    gidx = jnp.min(jnp.where(gl == gmax, lane, big), axis=-1, keepdims=True)
    g_w = 1.0 / jnp.sum(jnp.exp(gl - gmax), axis=-1, keepdims=True)
    lo = EXPERT_LANE0 + EXPERTS_PER_GROUP * gidx
    el = jnp.where((lane >= lo) & (lane < lo + EXPERTS_PER_GROUP), lg, neg)
    e1 = jnp.max(el, axis=-1, keepdims=True)
    i1 = jnp.min(jnp.where(el == e1, lane, big), axis=-1, keepdims=True)
    el2 = jnp.where(lane == i1, neg, el)
    e2 = jnp.max(el2, axis=-1, keepdims=True)
    i2 = jnp.min(jnp.where(el2 == e2, lane, big), axis=-1, keepdims=True)
    t = jnp.exp(e2 - e1)
    w1 = g_w / (1.0 + t)
    w2 = g_w * t / (1.0 + t)
    out = jnp.where(lane == i1, w1, 0.0) + jnp.where(lane == i2, w2, 0.0)
    pa = jnp.minimum(i1, i2) - lo
    pb = jnp.maximum(i1, i2) - lo
    pair = pa * (2 * EXPERTS_PER_GROUP - 1 - pa) * 0.5 + (pb - pa - 1.0)
    return out + jnp.where(lane == float(BUCKET_LANE), gidx * N_PAIRS + pair, 0.0)


def _rowwise_kernel(*refs, has_resid, has_mod, has_router):
    it = iter(refs)
    x_ref = next(it)
    if has_resid:
        y_ref, gate_m, gate_t = next(it), next(it), next(it)
    g_ref = next(it)
    if has_mod:
        sc_m, sc_t, sh_m, sh_t = next(it), next(it), next(it), next(it)
    if has_router:
        wr_ref, br_ref = next(it), next(it)
    if has_resid:
        xo_ref = next(it)
    h_ref = next(it)
    if has_router:
        gt_ref = next(it)

    def mod(m_ref, t_ref):
        m = m_ref[...] + t_ref[...]
        return jnp.broadcast_to(m[:, None, :], (MOD_ROWS, MOD_REPEAT, m.shape[-1])).reshape(ROW_TILE, m.shape[-1])

    x = x_ref[...]
    if has_resid:
        x = x + mod(gate_m, gate_t) * y_ref[...]
        xo_ref[...] = x
    y = x * lax.rsqrt(jnp.mean(x * x, axis=-1, keepdims=True) + EPS)
    h = y * g_ref[...]
    if has_mod:
        h = h * (1.0 + mod(sc_m, sc_t)) + mod(sh_m, sh_t)
    h_ref[...] = h.astype(h_ref.dtype)
    if has_router:
        h_hi = h.astype(BF16)
        h_lo = (h - h_hi.astype(F32)).astype(BF16)
        hi_terms = jnp.dot(h_hi, wr_ref[...], preferred_element_type=F32)
        lg = (hi_terms[:, :ROUTER_LANES] + hi_terms[:, ROUTER_LANES:]
              + jnp.dot(h_lo, wr_ref[:, :ROUTER_LANES], preferred_element_type=F32))
        gt_ref[...] = _route(lg + br_ref[...])


def _mod_tile_index(i):
    return jnp.where(i < PROMPT_ROW_TILES, i // PROMPT_TILES_PER_SEQ, BATCH + i - PROMPT_ROW_TILES)


def _rowwise(x, g_row, *, y=None, mod_tiles=None, ada_rows=None, layer=0, gate_chunk=None, gate_layer=None,
             mod_chunks=None, router=None, h_dtype=BF16, name="rowwise"):
    has_resid = y is not None
    has_mod = mod_chunks is not None
    has_router = router is not None
    d = D_MODEL
    row_spec = pl.BlockSpec((ROW_TILE, d), lambda i: (i, 0))
    vec_spec = pl.BlockSpec((1, d), lambda i: (0, 0))

    def mod_specs(chunk, lyr):
        return [pl.BlockSpec((None, MOD_ROWS, d), lambda i: (_mod_tile_index(i), 0, chunk)),
                pl.BlockSpec((None, 1, d), lambda i: (lyr * 6 + chunk, 0, 0))]

    args, specs = [x], [row_spec]
    if has_resid:
        args += [y, mod_tiles, ada_rows]
        specs += [row_spec] + mod_specs(gate_chunk, layer if gate_layer is None else gate_layer)
    args.append(g_row)
    specs.append(vec_spec)
    if has_mod:
        sc_chunk, sh_chunk = mod_chunks
        args += [mod_tiles, ada_rows, mod_tiles, ada_rows]
        specs += mod_specs(sc_chunk, layer) + mod_specs(sh_chunk, layer)
    if has_router:
        w_r, b_r = router
        args += [w_r, b_r]
        specs += [pl.BlockSpec((d, 2 * ROUTER_LANES), lambda i: (0, 0)),
                  pl.BlockSpec((1, ROUTER_LANES), lambda i: (0, 0))]
    out_shape, out_specs = [], []
    if has_resid:
        out_shape.append(jax.ShapeDtypeStruct((N_TOK, d), F32))
        out_specs.append(row_spec)
    out_shape.append(jax.ShapeDtypeStruct((N_TOK, d), h_dtype))
    out_specs.append(row_spec)
    if has_router:
        out_shape.append(jax.ShapeDtypeStruct((N_TOK, ROUTER_LANES), F32))
        out_specs.append(pl.BlockSpec((ROW_TILE, ROUTER_LANES), lambda i: (i, 0)))
    return pl.pallas_call(
        functools.partial(_rowwise_kernel, has_resid=has_resid, has_mod=has_mod, has_router=has_router),
        grid=(N_ROW_TILES,),
        in_specs=specs,
        out_specs=out_specs,
        out_shape=out_shape,
        compiler_params=_cparams("parallel"),
        name=name,
    )(*args)


def _moe_kernel(ea_ref, eb_ref, nu_ref, x_ref, gt_ref, wg_ref, wu_ref, wd_ref, o_ref):
    i = pl.program_id(0)
    j = pl.program_id(1)
    used = i < nu_ref[0]

    @pl.when(jnp.logical_and(j == 0, jnp.logical_not(used)))
    def _():
        o_ref[...] = jnp.zeros_like(o_ref)

    @pl.when(used)
    def _():
        x = x_ref[...]
        a = jnp.dot(x, wg_ref[...], preferred_element_type=F32)
        b = jnp.dot(x, wu_ref[...], preferred_element_type=F32)
        he = _silu(a) * b
        gates = gt_ref[...]
        lane = lax.broadcasted_iota(jnp.int32, gates.shape, 1)
        col = EXPERT_LANE0 + jnp.where(j < MOE_FF_STEPS, ea_ref[i], eb_ref[i])
        gcol = jnp.sum(jnp.where(lane == col, gates, 0.0), axis=-1, keepdims=True)
        contrib = gcol * jnp.dot(he.astype(BF16), wd_ref[...], preferred_element_type=F32)

        @pl.when(j == 0)
        def _():
            o_ref[...] = contrib

        @pl.when(j > 0)
        def _():
            o_ref[...] += contrib


def _moe(x_sorted, gates_sorted, tile_ea, tile_eb, n_used, w_gate, w_up, w_down, layer):
    d = D_MODEL

    def expert_of(i, j, ea, eb, nu):
        jj = jnp.where(i < nu[0], j, MOE_STEPS_PER_TILE - 1)
        return jnp.where(jj < MOE_FF_STEPS, ea[i], eb[i]), jj % MOE_FF_STEPS

    def in_row_map(i, j, ea, eb, nu):
        return (jnp.minimum(i, jnp.minimum(nu[0], MOE_MAX_TILES - 1)), 0)

    def wgu_map(i, j, ea, eb, nu):
        e, f = expert_of(i, j, ea, eb, nu)
        return (layer, e, 0, f)

    def wd_map(i, j, ea, eb, nu):
        e, f = expert_of(i, j, ea, eb, nu)
        return (layer, e, f, 0)

    grid_spec = pltpu.PrefetchScalarGridSpec(
        num_scalar_prefetch=3,
        grid=(MOE_MAX_TILES, MOE_STEPS_PER_TILE),
        in_specs=[pl.BlockSpec((MOE_TM, d), in_row_map),
                  pl.BlockSpec((MOE_TM, ROUTER_LANES), in_row_map),
                  pl.BlockSpec((None, None, d, MOE_FF_TILE), wgu_map),
                  pl.BlockSpec((None, None, d, MOE_FF_TILE), wgu_map),
                  pl.BlockSpec((None, None, MOE_FF_TILE, d), wd_map)],
        out_specs=pl.BlockSpec((MOE_TM, d), lambda i, j, ea, eb, nu: (i, 0)),
    )
    return pl.pallas_call(
        _moe_kernel,
        grid_spec=grid_spec,
        out_shape=jax.ShapeDtypeStruct((MOE_SLOTS, d), F32),
        compiler_params=_cparams("arbitrary", "arbitrary"),
        name="moe_experts",
    )(tile_ea, tile_eb, n_used, x_sorted, gates_sorted, w_gate, w_up, w_down)


def _take_rows(a, idx):
    return a.at[idx].get(mode="promise_in_bounds")


def _moe_plan(gates):
    bid = gates[:, BUCKET_LANE].astype(jnp.int32)
    onehot = (bid[:, None] == jnp.arange(N_BUCKETS, dtype=jnp.int32)[None, :]).astype(jnp.int32)
    rank = jnp.take_along_axis(jnp.cumsum(onehot, axis=0), bid[:, None], axis=1)[:, 0] - 1
    counts = jnp.sum(onehot, axis=0)
    tiles = (counts + MOE_TM - 1) // MOE_TM
    tile_end = jnp.cumsum(tiles)
    tile_start = tile_end - tiles
    dest = tile_start[bid] * MOE_TM + rank
    src = (jnp.arange(MOE_SLOTS, dtype=jnp.int32) % N_TOK).at[dest].set(jnp.arange(N_TOK, dtype=jnp.int32))
    valid = jnp.zeros((MOE_SLOTS,), F32).at[dest].set(1.0)
    n_used = tile_end[-1]
    t = jnp.arange(MOE_MAX_TILES, dtype=jnp.int32)
    tile_bucket = jnp.sum((jnp.minimum(t, n_used - 1)[:, None] >= tile_end[None, :]).astype(jnp.int32), axis=1)
    group0 = (tile_bucket // N_PAIRS) * EXPERTS_PER_GROUP
    pair = tile_bucket % N_PAIRS
    tile_ea = group0 + jnp.asarray([p[0] for p in PAIRS], jnp.int32)[pair]
    tile_eb = group0 + jnp.asarray([p[1] for p in PAIRS], jnp.int32)[pair]
    return src, valid, dest, tile_ea, tile_eb, n_used.reshape(1).astype(jnp.int32)


def _conv_step(u_ref, win_scr, w_ref, b_ref, cs):
    win_scr[CONV_PAD:CONV_PAD + cs, :] = u_ref[...]
    lo = CONV_PAD - CONV_PREV
    out = b_ref[...] + win_scr[lo:lo + cs, :] * w_ref[0:1, :]
    for j in range(1, CONV_W):
        out = out + win_scr[lo + j:lo + j + cs, :] * w_ref[j:j + 1, :]
    return out


def _conv_tail(win_scr, cs):
    return win_scr[CONV_PAD + cs - CONV_PREV:CONV_PAD + cs, :]


def _conv_advance(win_scr, cs):
    win_scr[CONV_PAD - CONV_PREV:CONV_PAD, :] = _conv_tail(win_scr, cs)


def _causal_masks(cs):
    row = lax.broadcasted_iota(jnp.int32, (cs, cs), 0)
    col = lax.broadcasted_iota(jnp.int32, (cs, cs), 1)
    return row >= col, row > col


def _cumsum_rows(x):
    cs = x.shape[0]
    row = lax.broadcasted_iota(jnp.int32, x.shape, 0)
    shift = 1
    while shift < cs:
        x = x + jnp.where(row >= shift, pltpu.roll(x, shift, 0), 0.0)
        shift *= 2
    return x


def _rms(x, width):
    return x * lax.rsqrt(jnp.sum(x * x, axis=-1, keepdims=True) * (1.0 / width) + EPS)


def _seq_specs(cs, nc, row0, layer):
    rb0 = row0 // cs

    def rows(width, col):
        return pl.BlockSpec((cs, width), lambda b, c: (rb0 + b * nc + c, col))

    def const(shape):
        nd = len(shape)
        return pl.BlockSpec(shape, lambda b, c: (0,) * nd)

    def per_seq(shape, last=0):
        nd = len(shape)
        return pl.BlockSpec((None,) + shape, lambda b, c: (b,) + (0,) * (nd - 1) + (last,))

    def state_in(shape, last=0):
        nd = len(shape)
        return pl.BlockSpec((None, None) + shape, lambda b, c: (layer, b) + (0,) * (nd - 1) + (last,))

    return rows, const, per_seq, state_in


def _ssd_kernel(z_ref, x_ref, bc_ref, nar_ref, s0_ref, c0x_ref, c0bc_ref,
                cwx_ref, cbx_ref, cwbc_ref, cbbc_ref, bias_ref, alog_ref, dsk_ref, ng_ref,
                y_ref, s_out_ref, cx_out_ref, cbc_out_ref,
                st_scr, wx_scr, wbc_scr, ycat_scr, *, cs, nc):
    c = pl.program_id(1)
    lo = CONV_PAD - CONV_PREV

    @pl.when(c == 0)
    def _():
        st_scr[...] = s0_ref[...]
        wx_scr[lo:CONV_PAD, :] = c0x_ref[...]
        wbc_scr[lo:CONV_PAD, :] = c0bc_ref[...]

    xs = _silu(_conv_step(x_ref, wx_scr, cwx_ref, cbx_ref, cs))
    bcs = _silu(_conv_step(bc_ref, wbc_scr, cwbc_ref, cbbc_ref, cs))
    incl, _ = _causal_masks(cs)
    dt = jax.nn.softplus(nar_ref[...] + bias_ref[...])
    cum = _cumsum_rows(dt * (-jnp.exp(alog_ref[...])))
    cum_t = cum.T
    exp_cum = jnp.exp(cum)
    cum_last = cum[cs - 1:cs, :]
    w_end = jnp.exp(cum_last - cum)
    chunk_decay = jnp.exp(cum_last)
    heads_per_group = SSD_HEADS // SSD_NGROUPS
    heads = range(SSD_HEADS)
    hp = SSD_HEAD_DIM
    s_old = [st_scr[h] for h in heads]
    col = lambda arr, h: arr[:, LANE_SSD_DT + h:LANE_SSD_DT + h + 1]
    b_gs = [bcs[:, g * SSD_STATE:(g + 1) * SSD_STATE] for g in range(SSD_NGROUPS)]
    c_gs = [bcs[:, (SSD_NGROUPS + g) * SSD_STATE:(SSD_NGROUPS + g + 1) * SSD_STATE] for g in range(SSD_NGROUPS)]
    cbs = [_dot_nt(c_gs[g], b_gs[g]) for g in range(SSD_NGROUPS)]
    grp = [h // heads_per_group for h in heads]
    scores = [cbs[grp[h]] * jnp.exp(jnp.where(incl, col(cum, h) - cum_t[LANE_SSD_DT + h:LANE_SSD_DT + h + 1, :],
                                              -jnp.inf)) for h in heads]
    xdts = [xs[:, h * hp:(h + 1) * hp] * col(dt, h) for h in heads]
    ys = [_dot(scores[h], xdts[h]) + _dot_nt(c_gs[grp[h]] * col(exp_cum, h), s_old[h]) for h in heads]
    s_new = [s_old[h] * col(chunk_decay, h) + _dot_tn(xdts[h] * col(w_end, h), b_gs[grp[h]]) for h in heads]
    for h in heads:
        st_scr[h] = s_new[h]
        ycat_scr[:, h * hp:(h + 1) * hp] = ys[h]
    y = (ycat_scr[...] + dsk_ref[...] * xs) * _silu(z_ref[...])
    gw = D_GROUP // SSD_NGROUPS
    for g in range(SSD_NGROUPS):
        y_g = _rms(y[:, g * gw:(g + 1) * gw], gw) * ng_ref[:, g * gw:(g + 1) * gw]
        y_ref[:, g * gw:(g + 1) * gw] = y_g.astype(y_ref.dtype)

    @pl.when(c == nc - 1)
    def _():
        s_out_ref[...] = st_scr[...]
        cx_out_ref[...] = _conv_tail(wx_scr, cs)
        cbc_out_ref[...] = _conv_tail(wbc_scr, cs)

    _conv_advance(wx_scr, cs)
    _conv_advance(wbc_scr, cs)


def _ssd_call(wide, narrow, s0, conv0, prm, *, n_seq, seq_len, cs, row0, layer):
    nc = seq_len // cs
    rows, const, per_seq, state_in = _seq_specs(cs, nc, row0, layer)
    xw, bcw = D_GROUP, SSD_BC_DIM
    x_off = WIDE_OFF[SEG_SSD_XBC]
    in_specs = [rows(xw, WIDE_OFF[SEG_SSD_Z] // xw), rows(xw, x_off // xw), rows(bcw, (x_off + xw) // bcw),
                rows(NARROW_COLS, 0),
                state_in((SSD_HEADS, SSD_HEAD_DIM, SSD_STATE)),
                state_in((CONV_PREV, xw)), state_in((CONV_PREV, bcw), last=xw // bcw),
                const((CONV_W, xw)), const((1, xw)), const((CONV_W, bcw)), const((1, bcw)),
                const((1, LANES)), const((1, LANES)), const((1, xw)), const((1, xw))]
    out_specs = [pl.BlockSpec((cs, xw), lambda b, c: (b * nc + c, 0)),
                 per_seq((SSD_HEADS, SSD_HEAD_DIM, SSD_STATE)),
                 per_seq((CONV_PREV, xw)), per_seq((CONV_PREV, bcw))]
    out_shape = [jax.ShapeDtypeStruct((n_seq * seq_len, xw), BF16),
                 jax.ShapeDtypeStruct((n_seq, SSD_HEADS, SSD_HEAD_DIM, SSD_STATE), F32),
                 jax.ShapeDtypeStruct((n_seq, CONV_PREV, xw), F32),
                 jax.ShapeDtypeStruct((n_seq, CONV_PREV, bcw), F32)]
    y, s_new, cx, cbc = pl.pallas_call(
        functools.partial(_ssd_kernel, cs=cs, nc=nc),
        grid=(n_seq, nc),
        in_specs=in_specs,
        out_specs=out_specs,
        out_shape=out_shape,
        scratch_shapes=[pltpu.VMEM((SSD_HEADS, SSD_HEAD_DIM, SSD_STATE), F32),
                        pltpu.VMEM((CONV_PAD + cs, xw), F32),
                        pltpu.VMEM((CONV_PAD + cs, bcw), F32),
                        pltpu.VMEM((cs, xw), F32)],
        compiler_params=_cparams("parallel", "arbitrary"),
        name="ssd_mixer",
    )(wide, wide, wide, narrow, s0, conv0, conv0,
      prm['cw'][:, :xw], prm['cb'][:, :xw], prm['cw'][:, xw:], prm['cb'][:, xw:],
      prm['bias'], prm['alog'], prm['dskip'], prm['ng'])
    return y, s_new, jnp.concatenate([cx, cbc], axis=-1)


def _ssd_params(conv_w, conv_b, dt_bias, a_log, d_skip, norm_g):
    pad = jnp.zeros((LANES - SSD_HEADS,), F32)
    return dict(cw=conv_w, cb=conv_b[None, :],
                bias=jnp.concatenate([dt_bias, pad])[None, :],
                alog=jnp.concatenate([a_log, pad])[None, :],
                dskip=jnp.repeat(d_skip, SSD_HEAD_DIM)[None, :], ng=norm_g[None, :])


def _split_hi_lo(x):
    hi = x.astype(BF16).astype(F32)
    lo = (x - hi).astype(BF16).astype(F32)
    return hi, lo


def _lhs3(a):
    hi, lo = _split_hi_lo(a)
    return jnp.concatenate([hi, lo, hi], axis=1).astype(BF16)


def _rhs3(b):
    hi, lo = _split_hi_lo(b)
    return jnp.concatenate([hi, hi, lo], axis=0).astype(BF16)


def _dot3(lhs3, rhs3):
    return jnp.dot(lhs3, rhs3, preferred_element_type=F32)


def _inv_unit_lower(a_list, cs):
    row = lax.broadcasted_iota(jnp.int32, (cs, cs), 0)
    col = lax.broadcasted_iota(jnp.int32, (cs, cs), 1)
    eye = jnp.where(row == col, 1.0, 0.0)
    ps = [-a for a in a_list]
    ts = [eye + p for p in ps]
    forms = [(_lhs3(p), _rhs3(p)) for p in ps]
    n = 1
    while 2 * n < cs:
        ps = [_dot3(lhs, rhs) for lhs, rhs in forms]
        forms = [(_lhs3(p), _rhs3(p)) for p in ps]
        ts = [t + _dot3(_lhs3(t), rhs) for t, (_, rhs) in zip(ts, forms)]
        n *= 2
    return ts


def _gdn_kernel(qkv_ref, z_ref, nar_ref, s0_ref, c0_ref, cw_ref, cb_ref, bias_ref, alog_ref, ng_ref,
                y_ref, s_out_ref, c_out_ref, st_scr, win_scr, *, cs, nc):
    c = pl.program_id(1)

    @pl.when(c == 0)
    def _():
        st_scr[...] = s0_ref[...]
        win_scr[CONV_PAD - CONV_PREV:CONV_PAD, :] = c0_ref[...]

    qkv = _silu(_conv_step(qkv_ref, win_scr, cw_ref, cb_ref, cs))
    incl, strict = _causal_masks(cs)
    nar = nar_ref[...]
    g_log = -jnp.exp(alog_ref[...]) * jax.nn.softplus(nar + bias_ref[...])
    beta_all = jax.nn.sigmoid(nar)
    gc = _cumsum_rows(g_log)
    gc_t = gc.T
    exp_gc = jnp.exp(gc)
    gc_last = gc[cs - 1:cs, :]
    exp_to_end = jnp.exp(gc_last - gc)
    g_last = jnp.exp(gc_last)
    hd = GDN_HEAD_DIM
    heads = range(GDN_HEADS)
    s_old = [st_scr[h] for h in heads]
    col = lambda arr, lane: arr[:, lane:lane + 1]
    qs = [qkv[:, h * hd:(h + 1) * hd] for h in heads]
    ks = [qkv[:, D_GROUP + h * hd:D_GROUP + (h + 1) * hd] for h in heads]
    vs = [qkv[:, 2 * D_GROUP + h * hd:2 * D_GROUP + (h + 1) * hd] for h in heads]
    qs = [q * lax.rsqrt(jnp.sum(q * q, axis=-1, keepdims=True) + EPS) * (hd ** -0.5) for q in qs]
    ks = [k * lax.rsqrt(jnp.sum(k * k, axis=-1, keepdims=True) + EPS) for k in ks]
    decays = [jnp.exp(jnp.where(incl, col(gc, LANE_GDN_A + h) - gc_t[LANE_GDN_A + h:LANE_GDN_A + h + 1, :],
                                -jnp.inf)) for h in heads]
    betas = [col(beta_all, LANE_GDN_B + h) for h in heads]
    a_mats = [jnp.where(strict, betas[h] * _dot_nt(ks[h], ks[h]) * decays[h], 0.0) for h in heads]
    t_invs = _inv_unit_lower(a_mats, cs)
    rhs = [jnp.concatenate([vs[h] * betas[h], ks[h] * (betas[h] * col(exp_gc, LANE_GDN_A + h))], axis=1)
           for h in heads]
    uw = [_dot3(_lhs3(t_invs[h]), _rhs3(rhs[h])) for h in heads]
    qks = [_dot_nt(qs[h], ks[h]) * decays[h] for h in heads]
    v_new = [uw[h][:, :hd] - _dot(uw[h][:, hd:], s_old[h]) for h in heads]
    outs = [_dot(qs[h] * col(exp_gc, LANE_GDN_A + h), s_old[h]) + _dot(qks[h], v_new[h]) for h in heads]
    s_new = [s_old[h] * col(g_last, LANE_GDN_A + h) + _dot_tn(ks[h] * col(exp_to_end, LANE_GDN_A + h), v_new[h])
             for h in heads]
    for h in heads:
        st_scr[h] = s_new[h]
    y = jnp.concatenate([_rms(o, hd) for o in outs], axis=1) * ng_ref[...] * _silu(z_ref[...])
    y_ref[...] = y.astype(y_ref.dtype)

    @pl.when(c == nc - 1)
    def _():
        s_out_ref[...] = st_scr[...]
        c_out_ref[...] = _conv_tail(win_scr, cs)

    _conv_advance(win_scr, cs)


def _gdn_call(wide, narrow, s0, conv0, prm, *, n_seq, seq_len, cs, row0, layer):
    nc = seq_len // cs
    rows, const, per_seq, state_in = _seq_specs(cs, nc, row0, layer)
    hd = GDN_HEAD_DIM
    in_specs = [rows(GDN_CONV_DIM, WIDE_OFF[SEG_GDN_QKV] // GDN_CONV_DIM),
                rows(D_GROUP, WIDE_OFF[SEG_GDN_Z] // D_GROUP),
                rows(NARROW_COLS, 0),
                state_in((GDN_HEADS, hd, hd)), state_in((CONV_PREV, GDN_CONV_DIM)),
                const((CONV_W, GDN_CONV_DIM)), const((1, GDN_CONV_DIM)),
                const((1, LANES)), const((1, LANES)), const((1, D_GROUP))]
    out_specs = [pl.BlockSpec((cs, D_GROUP), lambda b, c: (b * nc + c, 0)),
                 per_seq((GDN_HEADS, hd, hd)), per_seq((CONV_PREV, GDN_CONV_DIM))]
    out_shape = [jax.ShapeDtypeStruct((n_seq * seq_len, D_GROUP), BF16),
                 jax.ShapeDtypeStruct((n_seq, GDN_HEADS, hd, hd), F32),
                 jax.ShapeDtypeStruct((n_seq, CONV_PREV, GDN_CONV_DIM), F32)]
    return pl.pallas_call(
        functools.partial(_gdn_kernel, cs=cs, nc=nc),
        grid=(n_seq, nc),
        in_specs=in_specs,
        out_specs=out_specs,
        out_shape=out_shape,
        scratch_shapes=[pltpu.VMEM((GDN_HEADS, hd, hd), F32),
                        pltpu.VMEM((CONV_PAD + cs, GDN_CONV_DIM), F32)],
        compiler_params=_cparams("parallel", "arbitrary"),
        name="gdn_mixer",
    )(wide, wide, narrow, s0, conv0, prm['cw'], prm['cb'], prm['bias'], prm['alog'], prm['ng'])


def _lane_row(vec, lane0):
    return jnp.zeros((LANES,), F32).at[lane0:lane0 + vec.shape[0]].set(vec)[None, :]


def _gdn_params(conv_w, conv_b, dt_bias, a_log, norm_g):
    return dict(cw=conv_w, cb=conv_b[None, :], bias=_lane_row(dt_bias, LANE_GDN_A),
                alog=_lane_row(a_log, LANE_GDN_A), ng=norm_g[None, :])


def _mlstm_kernel(qkv_ref, o_ref, nar_ref, c0_ref, n0_ref, m0_ref, bias_ref, ng_ref,
                  y_ref, c_out_ref, n_out_ref, m_out_ref, c_scr, n_scr, m_scr, *, cs, nc):
    c = pl.program_id(1)

    @pl.when(c == 0)
    def _():
        c_scr[...] = c0_ref[...]
        n_scr[...] = n0_ref[...]
        m_scr[...] = m0_ref[...]

    incl, _ = _causal_masks(cs)
    pre = nar_ref[...] + bias_ref[...]
    f_cum = _cumsum_rows(jax.nn.log_sigmoid(pre))
    f_cum_t = f_cum.T
    pre_t = pre.T
    hd = MLSTM_HEAD_DIM
    heads = range(MLSTM_HEADS)
    c_old = [c_scr[h] for h in heads]
    n_all = n_scr[...]
    m_all = m_scr[...]
    qkv = qkv_ref[...]
    col = lambda arr, lane: arr[:, lane:lane + 1]
    qs = [qkv[:, h * hd:(h + 1) * hd] for h in heads]
    ks = [qkv[:, D_GROUP + h * hd:D_GROUP + (h + 1) * hd] * (hd ** -0.5) for h in heads]
    vs = [qkv[:, 2 * D_GROUP + h * hd:2 * D_GROUP + (h + 1) * hd] for h in heads]
    f_cols = [col(f_cum, LANE_ML_F + h) for h in heads]
    d_logs = [jnp.where(incl, f_cols[h] - f_cum_t[LANE_ML_F + h:LANE_ML_F + h + 1, :]
                        + pre_t[LANE_ML_I + h:LANE_ML_I + h + 1, :], -jnp.inf) for h in heads]
    m_prev = [col(m_all, LANE_ML_I + h) for h in heads]
    m_t = [jnp.maximum(f_cols[h] + m_prev[h], jnp.max(d_logs[h], axis=-1, keepdims=True)) for h in heads]
    w_carry = [jnp.exp(f_cols[h] + m_prev[h] - m_t[h]) for h in heads]
    ps = [jnp.exp(d_logs[h] - m_t[h]) * _dot_nt(qs[h], ks[h]) for h in heads]
    num = [w_carry[h] * _dot(qs[h], c_old[h]) + _dot(ps[h], vs[h]) for h in heads]
    den = [w_carry[h] * jnp.sum(qs[h] * n_all[h:h + 1, :], axis=-1, keepdims=True)
           + jnp.sum(ps[h], axis=-1, keepdims=True) for h in heads]
    hs = [num[h] / jnp.maximum(jnp.abs(den[h]), jnp.exp(-m_t[h])) for h in heads]
    m_end = [m_t[h][cs - 1:cs, :] for h in heads]
    f_last = [f_cols[h][cs - 1:cs, :] for h in heads]
    w_prev = [jnp.exp(f_last[h] + m_prev[h] - m_end[h]) for h in heads]
    kws = [ks[h] * jnp.exp(f_last[h] - f_cols[h] + col(pre, LANE_ML_I + h) - m_end[h]) for h in heads]
    c_new = [w_prev[h] * c_old[h] + _dot_tn(kws[h], vs[h]) for h in heads]
    n_new = [w_prev[h] * n_all[h:h + 1, :] + jnp.sum(kws[h], axis=0, keepdims=True) for h in heads]
    lane = lax.broadcasted_iota(jnp.int32, m_all.shape, 1)
    m_new = m_all
    for h in heads:
        c_scr[h] = c_new[h]
        m_new = jnp.where(lane == LANE_ML_I + h, m_end[h], m_new)
    n_scr[...] = jnp.concatenate(n_new, axis=0)
    m_scr[...] = m_new
    y = jnp.concatenate([_rms(hh, hd) for hh in hs], axis=1) * ng_ref[...] * jax.nn.sigmoid(o_ref[...])
    y_ref[...] = y.astype(y_ref.dtype)

    @pl.when(c == nc - 1)
    def _():
        c_out_ref[...] = c_scr[...]
        n_out_ref[...] = n_scr[...]
        m_out_ref[...] = m_scr[...]


def _mlstm_call(wide, narrow, c0, n0, m0, prm, *, n_seq, seq_len, cs, row0, layer):
    nc = seq_len // cs
    rows, const, per_seq, state_in = _seq_specs(cs, nc, row0, layer)
    hd = MLSTM_HEAD_DIM
    qkv_w = 3 * D_GROUP
    in_specs = [rows(qkv_w, WIDE_OFF[SEG_ML_QKV] // qkv_w),
                rows(D_GROUP, WIDE_OFF[SEG_ML_O] // D_GROUP),
                rows(NARROW_COLS, 0),
                state_in((MLSTM_HEADS, hd, hd)), state_in((MLSTM_HEADS, hd)), state_in((1, LANES)),
                const((1, LANES)), const((1, D_GROUP))]
    out_specs = [pl.BlockSpec((cs, D_GROUP), lambda b, c: (b * nc + c, 0)),
                 per_seq((MLSTM_HEADS, hd, hd)), per_seq((MLSTM_HEADS, hd)), per_seq((1, LANES))]
    out_shape = [jax.ShapeDtypeStruct((n_seq * seq_len, D_GROUP), BF16),
                 jax.ShapeDtypeStruct((n_seq, MLSTM_HEADS, hd, hd), F32),
                 jax.ShapeDtypeStruct((n_seq, MLSTM_HEADS, hd), F32),
                 jax.ShapeDtypeStruct((n_seq, 1, LANES), F32)]
    return pl.pallas_call(
        functools.partial(_mlstm_kernel, cs=cs, nc=nc),
        grid=(n_seq, nc),
        in_specs=in_specs,
        out_specs=out_specs,
        out_shape=out_shape,
        scratch_shapes=[pltpu.VMEM((MLSTM_HEADS, hd, hd), F32),
                        pltpu.VMEM((MLSTM_HEADS, hd), F32),
                        pltpu.VMEM((1, LANES), F32)],
        compiler_params=_cparams("parallel", "arbitrary"),
        name="mlstm_mixer",
    )(wide, wide, narrow, c0, n0, m0, prm['bias'], prm['ng'])


def _mlstm_params(i_bias, f_bias, norm_g):
    return dict(bias=_lane_row(i_bias, LANE_ML_I) + _lane_row(f_bias, LANE_ML_F), ng=norm_g[None, :])


def _lru_kernel(x_ref, gate_ref, h0_ref, c0_ref, cw_ref, cb_ref, wa_ref, ba_ref, wx_ref, bx_ref, lam_ref,
                y_ref, h_out_ref, c_out_ref, h_scr, win_scr, *, cs, nc):
    c = pl.program_id(1)

    @pl.when(c == 0)
    def _():
        h_scr[...] = h0_ref[...]
        win_scr[CONV_PAD - CONV_PREV:CONV_PAD, :] = c0_ref[...]

    xc = _conv_step(x_ref, win_scr, cw_ref, cb_ref, cs)
    r_parts, i_parts = [], []
    for g in range(LRU_BLOCKS):
        x_g = xc[:, g * LRU_BLOCK:(g + 1) * LRU_BLOCK].astype(BF16)
        r_parts.append(jnp.dot(x_g, wa_ref[g], preferred_element_type=F32))
        i_parts.append(jnp.dot(x_g, wx_ref[g], preferred_element_type=F32))
    r = jax.nn.sigmoid(jnp.concatenate(r_parts, axis=-1) + ba_ref[...])
    i = jax.nn.sigmoid(jnp.concatenate(i_parts, axis=-1) + bx_ref[...])
    log_a = -LRU_C * r * jax.nn.softplus(-lam_ref[...])
    a = jnp.exp(log_a)
    u = jnp.sqrt(jnp.tanh(-log_a) * (a * a + 1.0)) * (i * xc)
    row = lax.broadcasted_iota(jnp.int32, (cs, D_GROUP), 0)
    u = u + jnp.where(row == 0, a * h_scr[...], 0.0)
    shift = 1
    while shift < cs:
        a_sh = pltpu.roll(a, shift, 0)
        u_sh = pltpu.roll(u, shift, 0)
        live = row >= shift
        u = jnp.where(live, a * u_sh + u, u)
        a = jnp.where(live, a * a_sh, a)
        shift *= 2
    h_scr[...] = u[cs - 1:cs, :]
    y_ref[...] = (u * jax.nn.gelu(gate_ref[...])).astype(y_ref.dtype)

    @pl.when(c == nc - 1)
    def _():
        h_out_ref[...] = u[cs - 1:cs, :]
        c_out_ref[...] = _conv_tail(win_scr, cs)

    _conv_advance(win_scr, cs)


def _lru_call(wide, h0, conv0, prm, *, n_seq, seq_len, cs, row0, layer):
    nc = seq_len // cs
    rows, const, per_seq, state_in = _seq_specs(cs, nc, row0, layer)
    in_specs = [rows(D_GROUP, WIDE_OFF[SEG_LRU_X] // D_GROUP), rows(D_GROUP, WIDE_OFF[SEG_LRU_GATE] // D_GROUP),
                state_in((1, D_GROUP)), state_in((CONV_PREV, D_GROUP)),
                const((CONV_W, D_GROUP)), const((1, D_GROUP)),
                const((LRU_BLOCKS, LRU_BLOCK, LRU_BLOCK)), const((1, D_GROUP)),
                const((LRU_BLOCKS, LRU_BLOCK, LRU_BLOCK)), const((1, D_GROUP)), const((1, D_GROUP))]
    out_specs = [pl.BlockSpec((cs, D_GROUP), lambda b, c: (b * nc + c, 0)),
                 per_seq((1, D_GROUP)), per_seq((CONV_PREV, D_GROUP))]
    out_shape = [jax.ShapeDtypeStruct((n_seq * seq_len, D_GROUP), BF16),
                 jax.ShapeDtypeStruct((n_seq, 1, D_GROUP), F32),
                 jax.ShapeDtypeStruct((n_seq, CONV_PREV, D_GROUP), F32)]
    return pl.pallas_call(
        functools.partial(_lru_kernel, cs=cs, nc=nc),
        grid=(n_seq, nc),
        in_specs=in_specs,
        out_specs=out_specs,
        out_shape=out_shape,
        scratch_shapes=[pltpu.VMEM((1, D_GROUP), F32), pltpu.VMEM((CONV_PAD + cs, D_GROUP), F32)],
        compiler_params=_cparams("parallel", "arbitrary"),
        name="rglru_mixer",
    )(wide, wide, h0, conv0, prm['cw'], prm['cb'], prm['wa'], prm['ba'], prm['wx'], prm['bx'], prm['lam'])


def _lru_params(conv_w, conv_b, w_a, b_a, w_x, b_x, lam):
    return dict(cw=conv_w, cb=conv_b[None, :], wa=w_a.astype(BF16), ba=b_a.reshape(1, D_GROUP),
                wx=w_x.astype(BF16), bx=b_x.reshape(1, D_GROUP), lam=lam[None, :])


def _kernel_states(st):
    s_ssd, s_ssd_conv, s_gdn, s_gdn_conv, s_mc, s_mn, s_mm, s_lru, s_lru_conv = st
    lead = s_mm.shape[:2]
    m_rows = jnp.zeros(lead + (1, LANES), F32).at[:, :, 0, LANE_ML_I:LANE_ML_I + MLSTM_HEADS].set(s_mm)
    return (s_ssd, s_ssd_conv, s_gdn, s_gdn_conv, s_mc, s_mn, m_rows, s_lru[:, :, None, :], s_lru_conv)


def _mixer_group(wide, narrow, kst, prm, *, n_seq, seq_len, row0, layer):
    s_ssd, s_ssd_conv, s_gdn, s_gdn_conv, s_mc, s_mn, m_rows, s_lru, s_lru_conv = kst
    cs = min(seq_len, CHUNK)
    kw = dict(n_seq=n_seq, seq_len=seq_len, cs=cs, row0=row0, layer=layer)
    y_a, ssd_new, ssd_conv_new = _ssd_call(wide, narrow, s_ssd, s_ssd_conv, prm['ssd'], **kw)
    y_b, gdn_new, gdn_conv_new = _gdn_call(wide, narrow, s_gdn, s_gdn_conv, prm['gdn'], **kw)
    y_c, mc_new, mn_new, mm_new = _mlstm_call(wide, narrow, s_mc, s_mn, m_rows, prm['mlstm'], **kw)
    y_d, lru_new, lru_conv_new = _lru_call(wide, s_lru, s_lru_conv, prm['lru'], **kw)
    new = (ssd_new, ssd_conv_new, gdn_new, gdn_conv_new, mc_new, mn_new,
           mm_new[:, 0, LANE_ML_I:LANE_ML_I + MLSTM_HEADS], lru_new[:, 0, :], lru_conv_new)
    return (y_a, y_b, y_c, y_d), new


def _state_shapes(n):
    return ((n, SSD_HEADS, SSD_HEAD_DIM, SSD_STATE),
            (n, CONV_PREV, SSD_CONV_DIM),
            (n, GDN_HEADS, GDN_HEAD_DIM, GDN_HEAD_DIM),
            (n, CONV_PREV, GDN_CONV_DIM),
            (n, MLSTM_HEADS, MLSTM_HEAD_DIM, MLSTM_HEAD_DIM),
            (n, MLSTM_HEADS, MLSTM_HEAD_DIM),
            (n, MLSTM_HEADS),
            (n, D_GROUP),
            (n, CONV_PREV, D_GROUP))


def kernel(x_prompt, x_sample, state_ssd, state_ssd_conv, state_gdn, state_gdn_conv, state_mlstm_c, state_mlstm_n, state_mlstm_m, state_rglru, state_rglru_conv, c_prompt, c_sample, w_ada, ada_table, norm1_g, norm2_g, final_g, w_in, w_out, ssd_conv_w, ssd_conv_b, ssd_dt_bias, ssd_a_log, ssd_d, ssd_norm_g, gdn_conv_w, gdn_conv_b, gdn_dt_bias, gdn_a_log, gdn_norm_g, mlstm_i_bias, mlstm_f_bias, mlstm_norm_g, lru_conv_w, lru_conv_b, lru_w_a, lru_b_a, lru_w_x, lru_b_x, lru_lambda, moe_w_group, moe_b_group, moe_w_expert, moe_b_expert, moe_w_gate, moe_w_up, moe_w_down):
    d = D_MODEL

    w_in_wide = jnp.concatenate(
        [w_in[:, :, IN_OFFSETS[s]:IN_OFFSETS[s + 1]].astype(BF16) for s in _WIDE_SEGS], axis=-1)
    w_in_narrow = jnp.concatenate(
        [w_in[:, :, IN_OFFSETS[s]:IN_OFFSETS[s + 1]].astype(BF16) for s in _NARROW_SEGS]
        + [jnp.zeros((DEPTH, d, NARROW_COLS - sum(IN_SIZES[s] for s in _NARROW_SEGS)), BF16)], axis=-1)
    w_out_b = w_out.astype(BF16)
    w_gate_b = moe_w_gate.astype(BF16)
    w_up_b = moe_w_up.astype(BF16)
    w_down_b = moe_w_down.astype(BF16)
    router_pad = jnp.zeros((DEPTH, d, ROUTER_LANES - N_EXPERT_GROUPS - N_EXPERTS), F32)
    w_router = jnp.concatenate([moe_w_group, moe_w_expert, router_pad], axis=-1)
    w_router_hi = w_router.astype(BF16)
    w_router_lo = (w_router - w_router_hi.astype(F32)).astype(BF16)
    w_router = jnp.concatenate([w_router_hi, w_router_lo], axis=-1)
    b_router = jnp.concatenate([moe_b_group, moe_b_expert, router_pad[:, 0, :]], axis=-1)[:, None, :]

    c_pad = jnp.concatenate([c_prompt, c_sample, jnp.zeros((ADA_ROWS - BATCH - DEC_BATCH, d), F32)], axis=0)
    mod_shared = _ada_matmul(c_pad, w_ada)
    mod_tiles = jnp.concatenate(
        [jnp.broadcast_to(mod_shared[:BATCH, None, :], (BATCH, MOD_ROWS, 6 * d)),
         mod_shared[BATCH:BATCH + DEC_BATCH].reshape(SAMPLE_ROW_TILES, MOD_ROWS, 6 * d)], axis=0)
    ada_rows = ada_table.reshape(DEPTH * 6, 1, d)

    x = jnp.concatenate([x_prompt.reshape(N_PROMPT_TOK, d), x_sample.reshape(N_SAMPLE_TOK, d)], axis=0)

    st_sample = (state_ssd, state_ssd_conv, state_gdn, state_gdn_conv, state_mlstm_c,
                 state_mlstm_n, state_mlstm_m, state_rglru, state_rglru_conv)
    kst_sample = _kernel_states(st_sample)
    kst_prompt = _kernel_states(tuple(jnp.zeros((1,) + s, F32) for s in _state_shapes(BATCH)))
    new_p = [[] for _ in st_sample]
    new_s = [[] for _ in st_sample]

    y_prev = None
    for l in range(DEPTH):
        if l == 0:
            (h,) = _rowwise(x, norm1_g[l][None, :], mod_tiles=mod_tiles, ada_rows=ada_rows, layer=l,
                            mod_chunks=(1, 0), name="norm1")
        else:
            x, h = _rowwise(x, norm1_g[l][None, :], y=y_prev, mod_tiles=mod_tiles, ada_rows=ada_rows,
                            layer=l, gate_chunk=5, gate_layer=l - 1, mod_chunks=(1, 0), name="resid_norm1")
        wide = _matmul(h, w_in_wide[l], MM_TN, name="in_proj")
        narrow = _matmul(h, w_in_narrow[l], NARROW_COLS, name="in_proj_gates")

        prm = dict(ssd=_ssd_params(ssd_conv_w[l], ssd_conv_b[l], ssd_dt_bias[l], ssd_a_log[l], ssd_d[l],
                                   ssd_norm_g[l]),
                   gdn=_gdn_params(gdn_conv_w[l], gdn_conv_b[l], gdn_dt_bias[l], gdn_a_log[l], gdn_norm_g[l]),
                   mlstm=_mlstm_params(mlstm_i_bias[l], mlstm_f_bias[l], mlstm_norm_g[l]),
                   lru=_lru_params(lru_conv_w[l], lru_conv_b[l], lru_w_a[l], lru_b_a[l], lru_w_x[l], lru_b_x[l],
                                   lru_lambda[l]))
        ys_p, st_p = _mixer_group(wide, narrow, kst_prompt, prm, n_seq=BATCH, seq_len=SEQ, row0=0, layer=0)
        ys_s, st_s = _mixer_group(wide, narrow, kst_sample, prm,
                                  n_seq=DEC_BATCH, seq_len=DEC_SEQ, row0=N_PROMPT_TOK, layer=l)
        for acc, s in zip(new_p, st_p):
            acc.append(s)
        for acc, s in zip(new_s, st_s):
            acc.append(s)
        y_mix = _out_proj(ys_p, ys_s, w_out_b[l])

        x, h2, gates = _rowwise(x, norm2_g[l][None, :], y=y_mix, mod_tiles=mod_tiles, ada_rows=ada_rows,
                                layer=l, gate_chunk=2, mod_chunks=(4, 3),
                                router=(w_router[l], b_router[l]), name="resid_norm2_router")
        src, valid, dest, tile_ea, tile_eb, n_used = _moe_plan(gates)
        x_sorted = _take_rows(h2, src)
        gates_sorted = _take_rows(gates, src) * valid[:, None]
        y_sorted = _moe(x_sorted, gates_sorted, tile_ea, tile_eb, n_used, w_gate_b, w_up_b, w_down_b, l)
        y_prev = _take_rows(y_sorted, dest)

    x, y_fin = _rowwise(x, final_g[None, :], y=y_prev, mod_tiles=mod_tiles, ada_rows=ada_rows,
                        layer=DEPTH - 1, gate_chunk=5, h_dtype=F32, name="resid_final_norm")
    outs = [y_fin[:N_PROMPT_TOK].reshape(BATCH, SEQ, d), y_fin[N_PROMPT_TOK:].reshape(DEC_BATCH, DEC_SEQ, d)]
    for acc_p, acc_s in zip(new_p, new_s):
        outs.append(jnp.stack(acc_p))
        outs.append(jnp.stack(acc_s))
    return tuple(outs)
```

```python
import functools

import numpy as np
import jax
import jax.numpy as jnp
from jax import lax
from jax.experimental import pallas as pl
from jax.experimental.pallas import tpu as pltpu

F32 = jnp.float32
BF16 = jnp.bfloat16

D_MODEL = 4096
BATCH = 4
SEQ = 2048
DEPTH = 4
DEC_BATCH = 128
DEC_SEQ = 8
EPS = 1e-6
N_MIXERS = 4
D_GROUP = D_MODEL // N_MIXERS
CONV_W = 4
CHUNK = 64
SSD_HEAD_DIM = 64
SSD_HEADS = D_GROUP // SSD_HEAD_DIM
SSD_NGROUPS = 2
SSD_STATE = 128
SSD_BC_DIM = 2 * SSD_NGROUPS * SSD_STATE
SSD_CONV_DIM = D_GROUP + SSD_BC_DIM
GDN_HEAD_DIM = 128
GDN_HEADS = D_GROUP // GDN_HEAD_DIM
GDN_CONV_DIM = 3 * D_GROUP
MLSTM_HEAD_DIM = 128
MLSTM_HEADS = D_GROUP // MLSTM_HEAD_DIM
LRU_BLOCKS = 8
LRU_BLOCK = D_GROUP // LRU_BLOCKS
LRU_C = 8.0
N_EXPERT_GROUPS = 4
EXPERTS_PER_GROUP = 4
N_EXPERTS = N_EXPERT_GROUPS * EXPERTS_PER_GROUP
D_FF_EXPERT = D_MODEL // 8
IN_SIZES = (D_GROUP, SSD_CONV_DIM, SSD_HEADS,
            GDN_CONV_DIM, GDN_HEADS, GDN_HEADS, D_GROUP,
            3 * D_GROUP, MLSTM_HEADS, MLSTM_HEADS, D_GROUP,
            D_GROUP, D_GROUP)
IN_OFFSETS = tuple(int(o) for o in np.cumsum((0,) + IN_SIZES))
(SEG_SSD_Z, SEG_SSD_XBC, SEG_SSD_DT, SEG_GDN_QKV, SEG_GDN_A, SEG_GDN_B, SEG_GDN_Z,
 SEG_ML_QKV, SEG_ML_I, SEG_ML_F, SEG_ML_O, SEG_LRU_X, SEG_LRU_GATE) = range(len(IN_SIZES))

N_PROMPT_TOK = BATCH * SEQ
N_SAMPLE_TOK = DEC_BATCH * DEC_SEQ
N_TOK = N_PROMPT_TOK + N_SAMPLE_TOK

LANES = 128
SUBLANES = 8
VMEM_LIMIT_BYTES = 56 * 1024 * 1024

ROW_TILE = 256
MOD_REPEAT = DEC_SEQ
MOD_ROWS = ROW_TILE // MOD_REPEAT
N_ROW_TILES = N_TOK // ROW_TILE
PROMPT_TILES_PER_SEQ = SEQ // ROW_TILE
PROMPT_ROW_TILES = N_PROMPT_TOK // ROW_TILE
SAMPLE_ROW_TILES = N_SAMPLE_TOK // ROW_TILE
MM_TM = 1024
MM_TN = 512
ADA_ROWS = 144
ADA_TN = 512
MOE_TM = 512
MOE_FF_TILE = 256
MOE_FF_STEPS = D_FF_EXPERT // MOE_FF_TILE
TOP_K_IN_GROUP = 2
PAIRS = tuple((a, b) for a in range(EXPERTS_PER_GROUP) for b in range(a + 1, EXPERTS_PER_GROUP))
N_PAIRS = len(PAIRS)
N_BUCKETS = N_EXPERT_GROUPS * N_PAIRS
MOE_STEPS_PER_TILE = TOP_K_IN_GROUP * MOE_FF_STEPS
MOE_MAX_TILES = N_TOK // MOE_TM + N_BUCKETS
MOE_SLOTS = MOE_MAX_TILES * MOE_TM
ROUTER_LANES = LANES
EXPERT_LANE0 = N_EXPERT_GROUPS
BUCKET_LANE = 0

_WIDE_SEGS = (SEG_GDN_QKV, SEG_ML_QKV, SEG_SSD_Z, SEG_GDN_Z, SEG_ML_O, SEG_LRU_X, SEG_LRU_GATE, SEG_SSD_XBC)
_NARROW_SEGS = (SEG_SSD_DT, SEG_GDN_A, SEG_GDN_B, SEG_ML_I, SEG_ML_F)
WIDE_OFF = {}
_o = 0
for _s in _WIDE_SEGS:
    WIDE_OFF[_s] = _o
    _o += IN_SIZES[_s]
WIDE_COLS = _o
NARROW_OFF = {}
_o = 0
for _s in _NARROW_SEGS:
    NARROW_OFF[_s] = _o
    _o += IN_SIZES[_s]
NARROW_COLS = LANES
LANE_SSD_DT = NARROW_OFF[SEG_SSD_DT]
LANE_GDN_A = NARROW_OFF[SEG_GDN_A]
LANE_GDN_B = NARROW_OFF[SEG_GDN_B]
LANE_ML_I = NARROW_OFF[SEG_ML_I]
LANE_ML_F = NARROW_OFF[SEG_ML_F]

PROMPT_SEQS_PER_STEP = 1
SAMPLE_SEQS_PER_STEP = 8
CONV_PAD = SUBLANES
CONV_PREV = CONV_W - 1


def _cparams(*sem):
    return pltpu.CompilerParams(dimension_semantics=sem, vmem_limit_bytes=VMEM_LIMIT_BYTES)


def _dot(a, b):
    return jnp.dot(a.astype(BF16), b.astype(BF16), preferred_element_type=F32)


def _dot_nt(a, b):
    return lax.dot_general(a.astype(BF16), b.astype(BF16), (((1,), (1,)), ((), ())), preferred_element_type=F32)


def _dot_tn(a, b):
    return lax.dot_general(a.astype(BF16), b.astype(BF16), (((0,), (0,)), ((), ())), preferred_element_type=F32)


def _dot_f32(a, b):
    return jnp.dot(a, b, preferred_element_type=F32, precision=lax.Precision.HIGHEST)


def _silu(x):
    return x * jax.nn.sigmoid(x)


def _mm_kernel(a_ref, w_ref, o_ref):
    o_ref[...] = jnp.dot(a_ref[...], w_ref[...], preferred_element_type=F32).astype(o_ref.dtype)


def _matmul(a, w, tn, name, row0, m):
    k = a.shape[1]
    n = w.shape[1]
    assert m % MM_TM == 0 and row0 % MM_TM == 0 and n % tn == 0
    tile0 = row0 // MM_TM
    return pl.pallas_call(
        _mm_kernel,
        grid=(m // MM_TM, n // tn),
        in_specs=[pl.BlockSpec((MM_TM, k), lambda i, j: (tile0 + i, 0)),
                  pl.BlockSpec((k, tn), lambda i, j: (0, j))],
        out_specs=pl.BlockSpec((MM_TM, tn), lambda i, j: (i, j)),
        out_shape=jax.ShapeDtypeStruct((m, n), F32),
        compiler_params=_cparams("parallel", "arbitrary"),
        name=name,
    )(a, w)


PROMPT_MM_TILES = N_PROMPT_TOK // MM_TM
assert N_SAMPLE_TOK == MM_TM


def _out_proj_kernel(*refs):
    yp_refs, ys_refs = refs[:N_MIXERS], refs[N_MIXERS:2 * N_MIXERS]
    w_refs, o_ref = refs[2 * N_MIXERS:3 * N_MIXERS], refs[3 * N_MIXERS]

    def project(y_refs):
        acc = jnp.dot(y_refs[0][...], w_refs[0][...], preferred_element_type=F32)
        for m in range(1, N_MIXERS):
            acc += jnp.dot(y_refs[m][...], w_refs[m][...], preferred_element_type=F32)
        o_ref[...] = acc

    is_prompt = pl.program_id(0) < PROMPT_MM_TILES
    pl.when(is_prompt)(lambda: project(yp_refs))
    pl.when(jnp.logical_not(is_prompt))(lambda: project(ys_refs))


def _out_proj(ys_prompt, ys_sample, w):
    n = w.shape[1]
    yp_spec = pl.BlockSpec((MM_TM, D_GROUP), lambda i, j: (jnp.minimum(i, PROMPT_MM_TILES - 1), 0))
    ys_spec = pl.BlockSpec((MM_TM, D_GROUP), lambda i, j: (0, 0))
    w_specs = [pl.BlockSpec((D_GROUP, MM_TN), functools.partial(lambda i, j, m: (m, j), m=m))
               for m in range(N_MIXERS)]
    return pl.pallas_call(
        _out_proj_kernel,
        grid=(N_TOK // MM_TM, n // MM_TN),
        in_specs=[yp_spec] * N_MIXERS + [ys_spec] * N_MIXERS + w_specs,
        out_specs=pl.BlockSpec((MM_TM, MM_TN), lambda i, j: (i, j)),
        out_shape=jax.ShapeDtypeStruct((N_TOK, n), F32),
        compiler_params=_cparams("parallel", "arbitrary"),
        name="out_proj",
    )(*ys_prompt, *ys_sample, w, w, w, w)


W_IN_ROW_TILE = 256


def _w_in_relayout_kernel(w_ref, o_ref):
    for s in _WIDE_SEGS:
        o_ref[:, WIDE_OFF[s]:WIDE_OFF[s] + IN_SIZES[s]] = w_ref[:, IN_OFFSETS[s]:IN_OFFSETS[s + 1]].astype(BF16)


def _w_in_relayout(w_in):
    depth, k, n = w_in.shape
    return pl.pallas_call(
        _w_in_relayout_kernel,
        grid=(depth, k // W_IN_ROW_TILE),
        in_specs=[pl.BlockSpec((None, W_IN_ROW_TILE, n), lambda l, i: (l, i, 0))],
        out_specs=pl.BlockSpec((None, W_IN_ROW_TILE, WIDE_COLS), lambda l, i: (l, i, 0)),
        out_shape=jax.ShapeDtypeStruct((depth, k, WIDE_COLS), BF16),
        compiler_params=_cparams("parallel", "parallel"),
        name="w_in_relayout",
    )(w_in)


def _ada_kernel(c_ref, w_ref, o_ref):
    o_ref[...] = _dot(_silu(c_ref[...]), w_ref[...])


def _ada_matmul(c_pad, w_ada):
    k, n = w_ada.shape
    return pl.pallas_call(
        _ada_kernel,
        grid=(n // ADA_TN,),
        in_specs=[pl.BlockSpec((ADA_ROWS, k), lambda j: (0, 0)),
                  pl.BlockSpec((k, ADA_TN), lambda j: (0, j))],
        out_specs=pl.BlockSpec((ADA_ROWS, ADA_TN), lambda j: (0, j)),
        out_shape=jax.ShapeDtypeStruct((ADA_ROWS, n), F32),
        compiler_params=_cparams("arbitrary"),
        name="adaln_matmul",
    )(c_pad, w_ada)


def _route(lg):
    lane = lax.broadcasted_iota(jnp.int32, lg.shape, 1).astype(F32)
    neg = -jnp.inf
    big = float(ROUTER_LANES)
    gl = jnp.where(lane < N_EXPERT_GROUPS, lg, neg)
    gmax = jnp.max(gl, axis=-1, keepdims=True)
    gidx = jnp.min(jnp.where(gl == gmax, lane, big), axis=-1, keepdims=True)
    g_w = 1.0 / jnp.sum(jnp.exp(gl - gmax), axis=-1, keepdims=True)
    lo = EXPERT_LANE0 + EXPERTS_PER_GROUP * gidx
    el = jnp.where((lane >= lo) & (lane < lo + EXPERTS_PER_GROUP), lg, neg)
    e1 = jnp.max(el, axis=-1, keepdims=True)
    i1 = jnp.min(jnp.where(el == e1, lane, big), axis=-1, keepdims=True)
    el2 = jnp.where(lane == i1, neg, el)
    e2 = jnp.max(el2, axis=-1, keepdims=True)
    i2 = jnp.min(jnp.where(el2 == e2, lane, big), axis=-1, keepdims=True)
    t = jnp.exp(e2 - e1)
    w1 = g_w / (1.0 + t)
    w2 = g_w * t / (1.0 + t)
    out = jnp.where(lane == i1, w1, 0.0) + jnp.where(lane == i2, w2, 0.0)
    pa = jnp.minimum(i1, i2) - lo
    pb = jnp.maximum(i1, i2) - lo
    pair = pa * (2 * EXPERTS_PER_GROUP - 1 - pa) * 0.5 + (pb - pa - 1.0)
    return out + jnp.where(lane == float(BUCKET_LANE), gidx * N_PAIRS + pair, 0.0)


def _rowwise_kernel(*refs, has_resid, has_mod, has_router):
    it = iter(refs)
    x_ref = next(it)
    if has_resid:
        y_ref, gate_m, gate_t = next(it), next(it), next(it)
    g_ref = next(it)
    if has_mod:
        sc_m, sc_t, sh_m, sh_t = next(it), next(it), next(it), next(it)
    if has_router:
        wr_ref, br_ref = next(it), next(it)
    if has_resid:
        xo_ref = next(it)
    h_ref = next(it)
    if has_router:
        gt_ref = next(it)

    def mod(m_ref, t_ref):
        m = m_ref[...] + t_ref[...]
        return jnp.broadcast_to(m[:, None, :], (MOD_ROWS, MOD_REPEAT, m.shape[-1])).reshape(ROW_TILE, m.shape[-1])

    x = x_ref[...]
    if has_resid:
        x = x + mod(gate_m, gate_t) * y_ref[...]
        xo_ref[...] = x
    y = x * lax.rsqrt(jnp.mean(x * x, axis=-1, keepdims=True) + EPS)
    h = y * g_ref[...]
    if has_mod:
        h = h * (1.0 + mod(sc_m, sc_t)) + mod(sh_m, sh_t)
    h_ref[...] = h.astype(h_ref.dtype)
    if has_router:
        h_hi = h.astype(BF16)
        h_lo = (h - h_hi.astype(F32)).astype(BF16)
        hi_terms = jnp.dot(h_hi, wr_ref[...], preferred_element_type=F32)
        lg = (hi_terms[:, :ROUTER_LANES] + hi_terms[:, ROUTER_LANES:]
              + jnp.dot(h_lo, wr_ref[:, :ROUTER_LANES], preferred_element_type=F32))
        gt_ref[...] = _route(lg + br_ref[...])


def _mod_tile_index(i):
    return jnp.where(i < PROMPT_ROW_TILES, i // PROMPT_TILES_PER_SEQ, BATCH + i - PROMPT_ROW_TILES)


def _rowwise(x, g_row, *, y=None, mod_tiles=None, ada_rows=None, layer=0, gate_chunk=None, gate_layer=None,
             mod_chunks=None, router=None, h_dtype=BF16, name="rowwise"):
    has_resid = y is not None
    has_mod = mod_chunks is not None
    has_router = router is not None
    d = D_MODEL
    row_spec = pl.BlockSpec((ROW_TILE, d), lambda i: (i, 0))
    vec_spec = pl.BlockSpec((1, d), lambda i: (0, 0))

    def mod_specs(chunk, lyr):
        return [pl.BlockSpec((None, MOD_ROWS, d), lambda i: (_mod_tile_index(i), 0, chunk)),
                pl.BlockSpec((None, 1, d), lambda i: (lyr * 6 + chunk, 0, 0))]

    args, specs = [x], [row_spec]
    if has_resid:
        args += [y, mod_tiles, ada_rows]
        specs += [row_spec] + mod_specs(gate_chunk, layer if gate_layer is None else gate_layer)
    args.append(g_row)
    specs.append(vec_spec)
    if has_mod:
        sc_chunk, sh_chunk = mod_chunks
        args += [mod_tiles, ada_rows, mod_tiles, ada_rows]
        specs += mod_specs(sc_chunk, layer) + mod_specs(sh_chunk, layer)
    if has_router:
        w_r, b_r = router
        args += [w_r, b_r]
        specs += [pl.BlockSpec((d, 2 * ROUTER_LANES), lambda i: (0, 0)),
                  pl.BlockSpec((1, ROUTER_LANES), lambda i: (0, 0))]
    out_shape, out_specs = [], []
    if has_resid:
        out_shape.append(jax.ShapeDtypeStruct((N_TOK, d), F32))
        out_specs.append(row_spec)
    out_shape.append(jax.ShapeDtypeStruct((N_TOK, d), h_dtype))
    out_specs.append(row_spec)
    if has_router:
        out_shape.append(jax.ShapeDtypeStruct((N_TOK, ROUTER_LANES), F32))
        out_specs.append(pl.BlockSpec((ROW_TILE, ROUTER_LANES), lambda i: (i, 0)))
    return pl.pallas_call(
        functools.partial(_rowwise_kernel, has_resid=has_resid, has_mod=has_mod, has_router=has_router),
        grid=(N_ROW_TILES,),
        in_specs=specs,
        out_specs=out_specs,
        out_shape=out_shape,
        compiler_params=_cparams("parallel"),
        name=name,
    )(*args)


def _moe_kernel(ea_ref, eb_ref, nu_ref, x_ref, gt_ref, wg_ref, wu_ref, wd_ref, o_ref):
    i = pl.program_id(0)
    j = pl.program_id(1)
    used = i < nu_ref[0]

    @pl.when(jnp.logical_and(j == 0, jnp.logical_not(used)))
    def _():
        o_ref[...] = jnp.zeros_like(o_ref)

    @pl.when(used)
    def _():
        x = x_ref[...]
        a = jnp.dot(x, wg_ref[...], preferred_element_type=F32)
        b = jnp.dot(x, wu_ref[...], preferred_element_type=F32)
        he = _silu(a) * b
        gates = gt_ref[...]
        lane = lax.broadcasted_iota(jnp.int32, gates.shape, 1)
        col = EXPERT_LANE0 + jnp.where(j < MOE_FF_STEPS, ea_ref[i], eb_ref[i])
        gcol = jnp.sum(jnp.where(lane == col, gates, 0.0), axis=-1, keepdims=True)
        contrib = gcol * jnp.dot(he.astype(BF16), wd_ref[...], preferred_element_type=F32)

        @pl.when(j == 0)
        def _():
            o_ref[...] = contrib

        @pl.when(j > 0)
        def _():
            o_ref[...] += contrib


def _moe(x_sorted, gates_sorted, tile_ea, tile_eb, n_used, w_gate, w_up, w_down, layer):
    d = D_MODEL

    def expert_of(i, j, ea, eb, nu):
        jj = jnp.where(i < nu[0], j, MOE_STEPS_PER_TILE - 1)
        return jnp.where(jj < MOE_FF_STEPS, ea[i], eb[i]), jj % MOE_FF_STEPS

    def in_row_map(i, j, ea, eb, nu):
        return (jnp.minimum(i, jnp.minimum(nu[0], MOE_MAX_TILES - 1)), 0)

    def wgu_map(i, j, ea, eb, nu):
        e, f = expert_of(i, j, ea, eb, nu)
        return (layer, e, 0, f)

    def wd_map(i, j, ea, eb, nu):
        e, f = expert_of(i, j, ea, eb, nu)
        return (layer, e, f, 0)

    grid_spec = pltpu.PrefetchScalarGridSpec(
        num_scalar_prefetch=3,
        grid=(MOE_MAX_TILES, MOE_STEPS_PER_TILE),
        in_specs=[pl.BlockSpec((MOE_TM, d), in_row_map),
                  pl.BlockSpec((MOE_TM, ROUTER_LANES), in_row_map),
                  pl.BlockSpec((None, None, d, MOE_FF_TILE), wgu_map),
                  pl.BlockSpec((None, None, d, MOE_FF_TILE), wgu_map),
                  pl.BlockSpec((None, None, MOE_FF_TILE, d), wd_map)],
        out_specs=pl.BlockSpec((MOE_TM, d), lambda i, j, ea, eb, nu: (i, 0)),
    )
    return pl.pallas_call(
        _moe_kernel,
        grid_spec=grid_spec,
        out_shape=jax.ShapeDtypeStruct((MOE_SLOTS, d), F32),
        compiler_params=_cparams("arbitrary", "arbitrary"),
        name="moe_experts",
    )(tile_ea, tile_eb, n_used, x_sorted, gates_sorted, w_gate, w_up, w_down)


def _take_rows(a, idx):
    return a.at[idx].get(mode="promise_in_bounds")


def _moe_plan(gates):
    bid = gates[:, BUCKET_LANE].astype(jnp.int32)
    onehot = (bid[:, None] == jnp.arange(N_BUCKETS, dtype=jnp.int32)[None, :]).astype(jnp.int32)
    rank = jnp.take_along_axis(jnp.cumsum(onehot, axis=0), bid[:, None], axis=1)[:, 0] - 1
    counts = jnp.sum(onehot, axis=0)
    tiles = (counts + MOE_TM - 1) // MOE_TM
    tile_end = jnp.cumsum(tiles)
    tile_start = tile_end - tiles
    dest = tile_start[bid] * MOE_TM + rank
    src = (jnp.arange(MOE_SLOTS, dtype=jnp.int32) % N_TOK).at[dest].set(jnp.arange(N_TOK, dtype=jnp.int32))
    n_used = tile_end[-1]
    t = jnp.arange(MOE_MAX_TILES, dtype=jnp.int32)
    tile_bucket = jnp.sum((jnp.minimum(t, n_used - 1)[:, None] >= tile_end[None, :]).astype(jnp.int32), axis=1)
    slot = jnp.arange(MOE_SLOTS, dtype=jnp.int32)
    slot_bucket = jnp.repeat(tile_bucket, MOE_TM)
    valid = ((slot - tile_start[slot_bucket] * MOE_TM < counts[slot_bucket])
             & (slot < n_used * MOE_TM)).astype(F32)
    group0 = (tile_bucket // N_PAIRS) * EXPERTS_PER_GROUP
    pair = tile_bucket % N_PAIRS
    tile_ea = group0 + jnp.asarray([p[0] for p in PAIRS], jnp.int32)[pair]
    tile_eb = group0 + jnp.asarray([p[1] for p in PAIRS], jnp.int32)[pair]
    return src, valid, dest, tile_ea, tile_eb, n_used.reshape(1).astype(jnp.int32)


def _conv_step(u_ref, win_scr, w_ref, b_ref, cs):
    win_scr[CONV_PAD:CONV_PAD + cs, :] = u_ref[...]
    lo = CONV_PAD - CONV_PREV
    out = b_ref[...] + win_scr[lo:lo + cs, :] * w_ref[0:1, :]
    for j in range(1, CONV_W):
        out = out + win_scr[lo + j:lo + j + cs, :] * w_ref[j:j + 1, :]
    return out


def _conv_tail(win_scr, cs):
    return win_scr[CONV_PAD + cs - CONV_PREV:CONV_PAD + cs, :]


def _conv_advance(win_scr, cs):
    win_scr[CONV_PAD - CONV_PREV:CONV_PAD, :] = _conv_tail(win_scr, cs)


def _causal_masks(cs):
    row = lax.broadcasted_iota(jnp.int32, (cs, cs), 0)
    col = lax.broadcasted_iota(jnp.int32, (cs, cs), 1)
    return row >= col, row > col


def _cumsum_rows(x):
    cs = x.shape[0]
    row = lax.broadcasted_iota(jnp.int32, x.shape, 0)
    shift = 1
    while shift < cs:
        x = x + jnp.where(row >= shift, pltpu.roll(x, shift, 0), 0.0)
        shift *= 2
    return x


def _rms(x, width):
    return x * lax.rsqrt(jnp.sum(x * x, axis=-1, keepdims=True) * (1.0 / width) + EPS)


def _seq_specs(cs, layer, g):
    def rows(width, col):
        return pl.BlockSpec((g, cs, width), lambda b, c: (b, c, col))

    def const(shape):
        nd = len(shape)
        return pl.BlockSpec(shape, lambda b, c: (0,) * nd)

    def per_seq(shape, last=0):
        nd = len(shape)
        return pl.BlockSpec((g,) + shape, lambda b, c: (b,) + (0,) * (nd - 1) + (last,))

    def state_in(shape, last=0):
        nd = len(shape)
        return pl.BlockSpec((None, g) + shape, lambda b, c: (layer, b) + (0,) * (nd - 1) + (last,))

    return rows, const, per_seq, state_in


def _seq_batched_kernel(*refs, body, per_seq_refs, g, cs, nc):
    c = pl.program_id(1)
    views = [tuple(r.at[s] if flag else r for r, flag in zip(refs, per_seq_refs)) for s in range(g)]

    @pl.when(c == 0)
    def _():
        for v in views:
            body(*v, cs=cs, phase="init")

    for v in views:
        body(*v, cs=cs, phase="main")

    @pl.when(c == nc - 1)
    def _():
        for v in views:
            body(*v, cs=cs, phase="final")


def _seq_call(body, name, in_arrays, in_specs, n_const, out_specs, out_shape, scratch, *, n_seq, seq_len, cs, g):
    assert n_seq % g == 0
    nc = seq_len // cs
    flags = ([True] * (len(in_arrays) - n_const) + [False] * n_const
             + [True] * (len(out_shape) + len(scratch)))
    return pl.pallas_call(
        functools.partial(_seq_batched_kernel, body=body, per_seq_refs=tuple(flags), g=g, cs=cs, nc=nc),
        grid=(n_seq // g, nc),
        in_specs=in_specs,
        out_specs=out_specs,
        out_shape=out_shape,
        scratch_shapes=[pltpu.VMEM((g,) + shape, F32) for shape in scratch],
        compiler_params=_cparams("parallel", "arbitrary"),
        name=name,
    )(*in_arrays)


def _ssd_kernel(z_ref, x_ref, bc_ref, nar_ref, s0_ref, c0x_ref, c0bc_ref,
                cwx_ref, cbx_ref, cwbc_ref, cbbc_ref, bias_ref, alog_ref, dsk_ref, ng_ref,
                y_ref, s_out_ref, cx_out_ref, cbc_out_ref,
                st_scr, wx_scr, wbc_scr, ycat_scr, *, cs, phase):
    lo = CONV_PAD - CONV_PREV
    if phase == "init":
        st_scr[...] = s0_ref[...]
        wx_scr[lo:CONV_PAD, :] = c0x_ref[...]
        wbc_scr[lo:CONV_PAD, :] = c0bc_ref[...]
        return
    if phase == "final":
        s_out_ref[...] = st_scr[...]
        cx_out_ref[...] = wx_scr[lo:CONV_PAD, :]
        cbc_out_ref[...] = wbc_scr[lo:CONV_PAD, :]
        return

    xs = _silu(_conv_step(x_ref, wx_scr, cwx_ref, cbx_ref, cs))
    bcs = _silu(_conv_step(bc_ref, wbc_scr, cwbc_ref, cbbc_ref, cs))
    incl, _ = _causal_masks(cs)
    dt = jax.nn.softplus(nar_ref[...] + bias_ref[...])
    cum = _cumsum_rows(dt * (-jnp.exp(alog_ref[...])))
    cum_t = cum.T
    exp_cum = jnp.exp(cum)
    cum_last = cum[cs - 1:cs, :]
    w_end = jnp.exp(cum_last - cum)
    chunk_decay = jnp.exp(cum_last)
    heads_per_group = SSD_HEADS // SSD_NGROUPS
    heads = range(SSD_HEADS)
    hp = SSD_HEAD_DIM
    s_old = [st_scr[h] for h in heads]
    col = lambda arr, h: arr[:, LANE_SSD_DT + h:LANE_SSD_DT + h + 1]
    b_gs = [bcs[:, g * SSD_STATE:(g + 1) * SSD_STATE] for g in range(SSD_NGROUPS)]
    c_gs = [bcs[:, (SSD_NGROUPS + g) * SSD_STATE:(SSD_NGROUPS + g + 1) * SSD_STATE] for g in range(SSD_NGROUPS)]
    cbs = [_dot_nt(c_gs[g], b_gs[g]) for g in range(SSD_NGROUPS)]
    grp = [h // heads_per_group for h in heads]
    scores = [cbs[grp[h]] * jnp.exp(jnp.where(incl, col(cum, h) - cum_t[LANE_SSD_DT + h:LANE_SSD_DT + h + 1, :],
                                              -jnp.inf)) for h in heads]
    xdts = [xs[:, h * hp:(h + 1) * hp] * col(dt, h) for h in heads]
    ys = [_dot(scores[h], xdts[h]) + _dot_nt(c_gs[grp[h]] * col(exp_cum, h), s_old[h]) for h in heads]
    s_new = [s_old[h] * col(chunk_decay, h) + _dot_tn(xdts[h] * col(w_end, h), b_gs[grp[h]]) for h in heads]
    for h in heads:
        st_scr[h] = s_new[h]
        ycat_scr[:, h * hp:(h + 1) * hp] = ys[h]
    y = (ycat_scr[...] + dsk_ref[...] * xs) * _silu(z_ref[...])
    gw = D_GROUP // SSD_NGROUPS
    for g in range(SSD_NGROUPS):
        y_g = _rms(y[:, g * gw:(g + 1) * gw], gw) * ng_ref[:, g * gw:(g + 1) * gw]
        y_ref[:, g * gw:(g + 1) * gw] = y_g.astype(y_ref.dtype)
    _conv_advance(wx_scr, cs)
    _conv_advance(wbc_scr, cs)


def _ssd_call(wide, narrow, s0, conv0, prm, *, n_seq, seq_len, cs, g, layer):
    rows, const, per_seq, state_in = _seq_specs(cs, layer, g)
    xw, bcw = D_GROUP, SSD_BC_DIM
    x_off = WIDE_OFF[SEG_SSD_XBC]
    in_specs = [rows(xw, WIDE_OFF[SEG_SSD_Z] // xw), rows(xw, x_off // xw), rows(bcw, (x_off + xw) // bcw),
                rows(NARROW_COLS, 0),
                state_in((SSD_HEADS, SSD_HEAD_DIM, SSD_STATE)),
                state_in((CONV_PREV, xw)), state_in((CONV_PREV, bcw), last=xw // bcw),
                const((CONV_W, xw)), const((1, xw)), const((CONV_W, bcw)), const((1, bcw)),
                const((1, LANES)), const((1, LANES)), const((1, xw)), const((1, xw))]
    out_specs = [rows(xw, 0), per_seq((SSD_HEADS, SSD_HEAD_DIM, SSD_STATE)),
                 per_seq((CONV_PREV, xw)), per_seq((CONV_PREV, bcw))]
    out_shape = [jax.ShapeDtypeStruct((n_seq, seq_len, xw), BF16),
                 jax.ShapeDtypeStruct((n_seq, SSD_HEADS, SSD_HEAD_DIM, SSD_STATE), F32),
                 jax.ShapeDtypeStruct((n_seq, CONV_PREV, xw), F32),
                 jax.ShapeDtypeStruct((n_seq, CONV_PREV, bcw), F32)]
    scratch = [(SSD_HEADS, SSD_HEAD_DIM, SSD_STATE), (CONV_PAD + cs, xw), (CONV_PAD + cs, bcw), (cs, xw)]
    in_arrays = (wide, wide, wide, narrow, s0, conv0, conv0,
                 prm['cw'][:, :xw], prm['cb'][:, :xw], prm['cw'][:, xw:], prm['cb'][:, xw:],
                 prm['bias'], prm['alog'], prm['dskip'], prm['ng'])
    y, s_new, cx, cbc = _seq_call(_ssd_kernel, "ssd_mixer", in_arrays, in_specs, 8, out_specs, out_shape, scratch,
                                  n_seq=n_seq, seq_len=seq_len, cs=cs, g=g)
    return y, s_new, jnp.concatenate([cx, cbc], axis=-1)


def _ssd_params(conv_w, conv_b, dt_bias, a_log, d_skip, norm_g):
    pad = jnp.zeros((LANES - SSD_HEADS,), F32)
    return dict(cw=conv_w, cb=conv_b[None, :],
                bias=jnp.concatenate([dt_bias, pad])[None, :],
                alog=jnp.concatenate([a_log, pad])[None, :],
                dskip=jnp.repeat(d_skip, SSD_HEAD_DIM)[None, :], ng=norm_g[None, :])


def _split_hi_lo(x):
    hi = x.astype(BF16).astype(F32)
    lo = (x - hi).astype(BF16).astype(F32)
    return hi, lo


def _lhs3(a):
    hi, lo = _split_hi_lo(a)
    return jnp.concatenate([hi, lo, hi], axis=1).astype(BF16)


def _rhs3(b):
    hi, lo = _split_hi_lo(b)
    return jnp.concatenate([hi, hi, lo], axis=0).astype(BF16)


def _dot3(lhs3, rhs3):
    return jnp.dot(lhs3, rhs3, preferred_element_type=F32)


def _inv_unit_lower(a_list, cs):
    row = lax.broadcasted_iota(jnp.int32, (cs, cs), 0)
    col = lax.broadcasted_iota(jnp.int32, (cs, cs), 1)
    eye = jnp.where(row == col, 1.0, 0.0)
    ps = [-a for a in a_list]
    ts = [eye + p for p in ps]
    forms = [(_lhs3(p), _rhs3(p)) for p in ps]
    n = 1
    while 2 * n < cs:
        ps = [_dot3(lhs, rhs) for lhs, rhs in forms]
        forms = [(_lhs3(p), _rhs3(p)) for p in ps]
        ts = [t + _dot3(_lhs3(t), rhs) for t, (_, rhs) in zip(ts, forms)]
        n *= 2
    return ts


def _gdn_kernel(qkv_ref, z_ref, nar_ref, s0_ref, c0_ref, cw_ref, cb_ref, bias_ref, alog_ref, ng_ref,
                y_ref, s_out_ref, c_out_ref, st_scr, win_scr, *, cs, phase):
    if phase == "init":
        st_scr[...] = s0_ref[...]
        win_scr[CONV_PAD - CONV_PREV:CONV_PAD, :] = c0_ref[...]
        return
    if phase == "final":
        s_out_ref[...] = st_scr[...]
        c_out_ref[...] = win_scr[CONV_PAD - CONV_PREV:CONV_PAD, :]
        return

    qkv = _silu(_conv_step(qkv_ref, win_scr, cw_ref, cb_ref, cs))
    incl, strict = _causal_masks(cs)
    nar = nar_ref[...]
    g_log = -jnp.exp(alog_ref[...]) * jax.nn.softplus(nar + bias_ref[...])
    beta_all = jax.nn.sigmoid(nar)
    gc = _cumsum_rows(g_log)
    gc_t = gc.T
    exp_gc = jnp.exp(gc)
    gc_last = gc[cs - 1:cs, :]
    exp_to_end = jnp.exp(gc_last - gc)
    g_last = jnp.exp(gc_last)
    hd = GDN_HEAD_DIM
    heads = range(GDN_HEADS)
    s_old = [st_scr[h] for h in heads]
    col = lambda arr, lane: arr[:, lane:lane + 1]
    qs = [qkv[:, h * hd:(h + 1) * hd] for h in heads]
    ks = [qkv[:, D_GROUP + h * hd:D_GROUP + (h + 1) * hd] for h in heads]
    vs = [qkv[:, 2 * D_GROUP + h * hd:2 * D_GROUP + (h + 1) * hd] for h in heads]
    qs = [q * lax.rsqrt(jnp.sum(q * q, axis=-1, keepdims=True) + EPS) * (hd ** -0.5) for q in qs]
    ks = [k * lax.rsqrt(jnp.sum(k * k, axis=-1, keepdims=True) + EPS) for k in ks]
    decays = [jnp.exp(jnp.where(incl, col(gc, LANE_GDN_A + h) - gc_t[LANE_GDN_A + h:LANE_GDN_A + h + 1, :],
                                -jnp.inf)) for h in heads]
    betas = [col(beta_all, LANE_GDN_B + h) for h in heads]
    a_mats = [jnp.where(strict, betas[h] * _dot_nt(ks[h], ks[h]) * decays[h], 0.0) for h in heads]
    t_invs = _inv_unit_lower(a_mats, cs)
    rhs = [jnp.concatenate([vs[h] * betas[h], ks[h] * (betas[h] * col(exp_gc, LANE_GDN_A + h))], axis=1)
           for h in heads]
    uw = [_dot3(_lhs3(t_invs[h]), _rhs3(rhs[h])) for h in heads]
    qks = [_dot_nt(qs[h], ks[h]) * decays[h] for h in heads]
    v_new = [uw[h][:, :hd] - _dot(uw[h][:, hd:], s_old[h]) for h in heads]
    outs = [_dot(qs[h] * col(exp_gc, LANE_GDN_A + h), s_old[h]) + _dot(qks[h], v_new[h]) for h in heads]
    s_new = [s_old[h] * col(g_last, LANE_GDN_A + h) + _dot_tn(ks[h] * col(exp_to_end, LANE_GDN_A + h), v_new[h])
             for h in heads]
    for h in heads:
        st_scr[h] = s_new[h]
    y = jnp.concatenate([_rms(o, hd) for o in outs], axis=1) * ng_ref[...] * _silu(z_ref[...])
    y_ref[...] = y.astype(y_ref.dtype)
    _conv_advance(win_scr, cs)


def _gdn_call(wide, narrow, s0, conv0, prm, *, n_seq, seq_len, cs, g, layer):
    rows, const, per_seq, state_in = _seq_specs(cs, layer, g)
    hd = GDN_HEAD_DIM
    in_specs = [rows(GDN_CONV_DIM, WIDE_OFF[SEG_GDN_QKV] // GDN_CONV_DIM),
                rows(D_GROUP, WIDE_OFF[SEG_GDN_Z] // D_GROUP),
                rows(NARROW_COLS, 0),
                state_in((GDN_HEADS, hd, hd)), state_in((CONV_PREV, GDN_CONV_DIM)),
                const((CONV_W, GDN_CONV_DIM)), const((1, GDN_CONV_DIM)),
                const((1, LANES)), const((1, LANES)), const((1, D_GROUP))]
    out_specs = [rows(D_GROUP, 0), per_seq((GDN_HEADS, hd, hd)), per_seq((CONV_PREV, GDN_CONV_DIM))]
    out_shape = [jax.ShapeDtypeStruct((n_seq, seq_len, D_GROUP), BF16),
                 jax.ShapeDtypeStruct((n_seq, GDN_HEADS, hd, hd), F32),
                 jax.ShapeDtypeStruct((n_seq, CONV_PREV, GDN_CONV_DIM), F32)]
    scratch = [(GDN_HEADS, hd, hd), (CONV_PAD + cs, GDN_CONV_DIM)]
    in_arrays = (wide, wide, narrow, s0, conv0, prm['cw'], prm['cb'], prm['bias'], prm['alog'], prm['ng'])
    return _seq_call(_gdn_kernel, "gdn_mixer", in_arrays, in_specs, 5, out_specs, out_shape, scratch,
                     n_seq=n_seq, seq_len=seq_len, cs=cs, g=g)


def _lane_row(vec, lane0):
    return jnp.zeros((LANES,), F32).at[lane0:lane0 + vec.shape[0]].set(vec)[None, :]


def _gdn_params(conv_w, conv_b, dt_bias, a_log, norm_g):
    return dict(cw=conv_w, cb=conv_b[None, :], bias=_lane_row(dt_bias, LANE_GDN_A),
                alog=_lane_row(a_log, LANE_GDN_A), ng=norm_g[None, :])


def _mlstm_kernel(qkv_ref, o_ref, nar_ref, c0_ref, n0_ref, m0_ref, bias_ref, ng_ref,
                  y_ref, c_out_ref, n_out_ref, m_out_ref, c_scr, n_scr, m_scr, *, cs, phase):
    if phase == "init":
        c_scr[...] = c0_ref[...]
        n_scr[...] = n0_ref[...]
        m_scr[...] = m0_ref[...]
        return
    if phase == "final":
        c_out_ref[...] = c_scr[...]
        n_out_ref[...] = n_scr[...]
        m_out_ref[...] = m_scr[...]
        return

    incl, _ = _causal_masks(cs)
    pre = nar_ref[...] + bias_ref[...]
    f_cum = _cumsum_rows(jax.nn.log_sigmoid(pre))
    f_cum_t = f_cum.T
    pre_t = pre.T
    hd = MLSTM_HEAD_DIM
    heads = range(MLSTM_HEADS)
    c_old = [c_scr[h] for h in heads]
    n_all = n_scr[...]
    m_all = m_scr[...]
    qkv = qkv_ref[...]
    col = lambda arr, lane: arr[:, lane:lane + 1]
    qs = [qkv[:, h * hd:(h + 1) * hd] for h in heads]
    ks = [qkv[:, D_GROUP + h * hd:D_GROUP + (h + 1) * hd] * (hd ** -0.5) for h in heads]
    vs = [qkv[:, 2 * D_GROUP + h * hd:2 * D_GROUP + (h + 1) * hd] for h in heads]
    f_cols = [col(f_cum, LANE_ML_F + h) for h in heads]
    d_logs = [jnp.where(incl, f_cols[h] - f_cum_t[LANE_ML_F + h:LANE_ML_F + h + 1, :]
                        + pre_t[LANE_ML_I + h:LANE_ML_I + h + 1, :], -jnp.inf) for h in heads]
    m_prev = [col(m_all, LANE_ML_I + h) for h in heads]
    m_t = [jnp.maximum(f_cols[h] + m_prev[h], jnp.max(d_logs[h], axis=-1, keepdims=True)) for h in heads]
    w_carry = [jnp.exp(f_cols[h] + m_prev[h] - m_t[h]) for h in heads]
    ps = [jnp.exp(d_logs[h] - m_t[h]) * _dot_nt(qs[h], ks[h]) for h in heads]
    num = [w_carry[h] * _dot(qs[h], c_old[h]) + _dot(ps[h], vs[h]) for h in heads]
    den = [w_carry[h] * jnp.sum(qs[h] * n_all[h:h + 1, :], axis=-1, keepdims=True)
           + jnp.sum(ps[h], axis=-1, keepdims=True) for h in heads]
    hs = [num[h] / jnp.maximum(jnp.abs(den[h]), jnp.exp(-m_t[h])) for h in heads]
    m_end = [m_t[h][cs - 1:cs, :] for h in heads]
    f_last = [f_cols[h][cs - 1:cs, :] for h in heads]
    w_prev = [jnp.exp(f_last[h] + m_prev[h] - m_end[h]) for h in heads]
    kws = [ks[h] * jnp.exp(f_last[h] - f_cols[h] + col(pre, LANE_ML_I + h) - m_end[h]) for h in heads]
    c_new = [w_prev[h] * c_old[h] + _dot_tn(kws[h], vs[h]) for h in heads]
    n_new = [w_prev[h] * n_all[h:h + 1, :] + jnp.sum(kws[h], axis=0, keepdims=True) for h in heads]
    lane = lax.broadcasted_iota(jnp.int32, m_all.shape, 1)
    m_new = m_all
    for h in heads:
        c_scr[h] = c_new[h]
        m_new = jnp.where(lane == LANE_ML_I + h, m_end[h], m_new)
    n_scr[...] = jnp.concatenate(n_new, axis=0)
    m_scr[...] = m_new
    y = jnp.concatenate([_rms(hh, hd) for hh in hs], axis=1) * ng_ref[...] * jax.nn.sigmoid(o_ref[...])
    y_ref[...] = y.astype(y_ref.dtype)


def _mlstm_call(wide, narrow, c0, n0, m0, prm, *, n_seq, seq_len, cs, g, layer):
    rows, const, per_seq, state_in = _seq_specs(cs, layer, g)
    hd = MLSTM_HEAD_DIM
    qkv_w = 3 * D_GROUP
    in_specs = [rows(qkv_w, WIDE_OFF[SEG_ML_QKV] // qkv_w),
                rows(D_GROUP, WIDE_OFF[SEG_ML_O] // D_GROUP),
                rows(NARROW_COLS, 0),
                state_in((MLSTM_HEADS, hd, hd)), state_in((MLSTM_HEADS, hd)), state_in((1, LANES)),
                const((1, LANES)), const((1, D_GROUP))]
    out_specs = [rows(D_GROUP, 0), per_seq((MLSTM_HEADS, hd, hd)), per_seq((MLSTM_HEADS, hd)), per_seq((1, LANES))]
    out_shape = [jax.ShapeDtypeStruct((n_seq, seq_len, D_GROUP), BF16),
                 jax.ShapeDtypeStruct((n_seq, MLSTM_HEADS, hd, hd), F32),
                 jax.ShapeDtypeStruct((n_seq, MLSTM_HEADS, hd), F32),
                 jax.ShapeDtypeStruct((n_seq, 1, LANES), F32)]
    scratch = [(MLSTM_HEADS, hd, hd), (MLSTM_HEADS, hd), (1, LANES)]
    in_arrays = (wide, wide, narrow, c0, n0, m0, prm['bias'], prm['ng'])
    return _seq_call(_mlstm_kernel, "mlstm_mixer", in_arrays, in_specs, 2, out_specs, out_shape, scratch,
                     n_seq=n_seq, seq_len=seq_len, cs=cs, g=g)


def _mlstm_params(i_bias, f_bias, norm_g):
    return dict(bias=_lane_row(i_bias, LANE_ML_I) + _lane_row(f_bias, LANE_ML_F), ng=norm_g[None, :])


def _lru_kernel(x_ref, gate_ref, h0_ref, c0_ref, cw_ref, cb_ref, wa_ref, ba_ref, wx_ref, bx_ref, lam_ref,
                y_ref, h_out_ref, c_out_ref, h_scr, win_scr, *, cs, phase):
    if phase == "init":
        h_scr[...] = h0_ref[...]
        win_scr[CONV_PAD - CONV_PREV:CONV_PAD, :] = c0_ref[...]
        return
    if phase == "final":
        h_out_ref[...] = h_scr[...]
        c_out_ref[...] = win_scr[CONV_PAD - CONV_PREV:CONV_PAD, :]
        return

    xc = _conv_step(x_ref, win_scr, cw_ref, cb_ref, cs)
    r_parts, i_parts = [], []
    for g in range(LRU_BLOCKS):
        x_g = xc[:, g * LRU_BLOCK:(g + 1) * LRU_BLOCK].astype(BF16)
        r_parts.append(jnp.dot(x_g, wa_ref[g], preferred_element_type=F32))
        i_parts.append(jnp.dot(x_g, wx_ref[g], preferred_element_type=F32))
    r = jax.nn.sigmoid(jnp.concatenate(r_parts, axis=-1) + ba_ref[...])
    i = jax.nn.sigmoid(jnp.concatenate(i_parts, axis=-1) + bx_ref[...])
    log_a = -LRU_C * r * jax.nn.softplus(-lam_ref[...])
    a = jnp.exp(log_a)
    u = jnp.sqrt(jnp.tanh(-log_a) * (a * a + 1.0)) * (i * xc)
    row = lax.broadcasted_iota(jnp.int32, (cs, D_GROUP), 0)
    u = u + jnp.where(row == 0, a * h_scr[...], 0.0)
    shift = 1
    while shift < cs:
        a_sh = pltpu.roll(a, shift, 0)
        u_sh = pltpu.roll(u, shift, 0)
        live = row >= shift
        u = jnp.where(live, a * u_sh + u, u)
        a = jnp.where(live, a * a_sh, a)
        shift *= 2
    h_scr[...] = u[cs - 1:cs, :]
    y_ref[...] = (u * jax.nn.gelu(gate_ref[...])).astype(y_ref.dtype)
    _conv_advance(win_scr, cs)


def _lru_call(wide, h0, conv0, prm, *, n_seq, seq_len, cs, g, layer):
    rows, const, per_seq, state_in = _seq_specs(cs, layer, g)
    in_specs = [rows(D_GROUP, WIDE_OFF[SEG_LRU_X] // D_GROUP), rows(D_GROUP, WIDE_OFF[SEG_LRU_GATE] // D_GROUP),
                state_in((1, D_GROUP)), state_in((CONV_PREV, D_GROUP)),
                const((CONV_W, D_GROUP)), const((1, D_GROUP)),
                const((LRU_BLOCKS, LRU_BLOCK, LRU_BLOCK)), const((1, D_GROUP)),
                const((LRU_BLOCKS, LRU_BLOCK, LRU_BLOCK)), const((1, D_GROUP)), const((1, D_GROUP))]
    out_specs = [rows(D_GROUP, 0), per_seq((1, D_GROUP)), per_seq((CONV_PREV, D_GROUP))]
    out_shape = [jax.ShapeDtypeStruct((n_seq, seq_len, D_GROUP), BF16),
                 jax.ShapeDtypeStruct((n_seq, 1, D_GROUP), F32),
                 jax.ShapeDtypeStruct((n_seq, CONV_PREV, D_GROUP), F32)]
    scratch = [(1, D_GROUP), (CONV_PAD + cs, D_GROUP)]
    in_arrays = (wide, wide, h0, conv0, prm['cw'], prm['cb'], prm['wa'], prm['ba'], prm['wx'], prm['bx'], prm['lam'])
    return _seq_call(_lru_kernel, "rglru_mixer", in_arrays, in_specs, 7, out_specs, out_shape, scratch,
                     n_seq=n_seq, seq_len=seq_len, cs=cs, g=g)


def _lru_params(conv_w, conv_b, w_a, b_a, w_x, b_x, lam):
    return dict(cw=conv_w, cb=conv_b[None, :], wa=w_a.astype(BF16), ba=b_a.reshape(1, D_GROUP),
                wx=w_x.astype(BF16), bx=b_x.reshape(1, D_GROUP), lam=lam[None, :])


def _kernel_states(st):
    s_ssd, s_ssd_conv, s_gdn, s_gdn_conv, s_mc, s_mn, s_mm, s_lru, s_lru_conv = st
    lead = s_mm.shape[:2]
    m_rows = jnp.zeros(lead + (1, LANES), F32).at[:, :, 0, LANE_ML_I:LANE_ML_I + MLSTM_HEADS].set(s_mm)
    return (s_ssd, s_ssd_conv, s_gdn, s_gdn_conv, s_mc, s_mn, m_rows, s_lru[:, :, None, :], s_lru_conv)


def _mixer_group(h, w_wide, w_narrow, kst, prm, *, n_seq, seq_len, row0, g, layer):
    s_ssd, s_ssd_conv, s_gdn, s_gdn_conv, s_mc, s_mn, m_rows, s_lru, s_lru_conv = kst
    m = n_seq * seq_len
    wide = _matmul(h, w_wide, MM_TN, "in_proj", row0, m).reshape(n_seq, seq_len, WIDE_COLS)
    narrow = _matmul(h, w_narrow, NARROW_COLS, "in_proj_gates", row0, m).reshape(n_seq, seq_len, NARROW_COLS)
    cs = min(seq_len, CHUNK)
    kw = dict(n_seq=n_seq, seq_len=seq_len, cs=cs, g=g, layer=layer)
    y_a, ssd_new, ssd_conv_new = _ssd_call(wide, narrow, s_ssd, s_ssd_conv, prm['ssd'], **kw)
    y_b, gdn_new, gdn_conv_new = _gdn_call(wide, narrow, s_gdn, s_gdn_conv, prm['gdn'], **kw)
    y_c, mc_new, mn_new, mm_new = _mlstm_call(wide, narrow, s_mc, s_mn, m_rows, prm['mlstm'], **kw)
    y_d, lru_new, lru_conv_new = _lru_call(wide, s_lru, s_lru_conv, prm['lru'], **kw)
    new = (ssd_new, ssd_conv_new, gdn_new, gdn_conv_new, mc_new, mn_new,
           mm_new[:, 0, LANE_ML_I:LANE_ML_I + MLSTM_HEADS], lru_new[:, 0, :], lru_conv_new)
    return tuple(y.reshape(m, D_GROUP) for y in (y_a, y_b, y_c, y_d)), new


def _state_shapes(n):
    return ((n, SSD_HEADS, SSD_HEAD_DIM, SSD_STATE),
            (n, CONV_PREV, SSD_CONV_DIM),
            (n, GDN_HEADS, GDN_HEAD_DIM, GDN_HEAD_DIM),
            (n, CONV_PREV, GDN_CONV_DIM),
            (n, MLSTM_HEADS, MLSTM_HEAD_DIM, MLSTM_HEAD_DIM),
            (n, MLSTM_HEADS, MLSTM_HEAD_DIM),
            (n, MLSTM_HEADS),
            (n, D_GROUP),
            (n, CONV_PREV, D_GROUP))


def kernel(x_prompt, x_sample, state_ssd, state_ssd_conv, state_gdn, state_gdn_conv, state_mlstm_c, state_mlstm_n, state_mlstm_m, state_rglru, state_rglru_conv, c_prompt, c_sample, w_ada, ada_table, norm1_g, norm2_g, final_g, w_in, w_out, ssd_conv_w, ssd_conv_b, ssd_dt_bias, ssd_a_log, ssd_d, ssd_norm_g, gdn_conv_w, gdn_conv_b, gdn_dt_bias, gdn_a_log, gdn_norm_g, mlstm_i_bias, mlstm_f_bias, mlstm_norm_g, lru_conv_w, lru_conv_b, lru_w_a, lru_b_a, lru_w_x, lru_b_x, lru_lambda, moe_w_group, moe_b_group, moe_w_expert, moe_b_expert, moe_w_gate, moe_w_up, moe_w_down):
    d = D_MODEL

    w_in_wide = _w_in_relayout(w_in)
    w_in_narrow = jnp.concatenate(
        [w_in[:, :, IN_OFFSETS[s]:IN_OFFSETS[s + 1]].astype(BF16) for s in _NARROW_SEGS]
        + [jnp.zeros((DEPTH, d, NARROW_COLS - sum(IN_SIZES[s] for s in _NARROW_SEGS)), BF16)], axis=-1)
    w_out_b = w_out.astype(BF16)
    w_gate_b = moe_w_gate.astype(BF16)
    w_up_b = moe_w_up.astype(BF16)
    w_down_b = moe_w_down.astype(BF16)
    router_pad = jnp.zeros((DEPTH, d, ROUTER_LANES - N_EXPERT_GROUPS - N_EXPERTS), F32)
    w_router = jnp.concatenate([moe_w_group, moe_w_expert, router_pad], axis=-1)
    w_router_hi = w_router.astype(BF16)
    w_router_lo = (w_router - w_router_hi.astype(F32)).astype(BF16)
    w_router = jnp.concatenate([w_router_hi, w_router_lo], axis=-1)
    b_router = jnp.concatenate([moe_b_group, moe_b_expert, router_pad[:, 0, :]], axis=-1)[:, None, :]

    c_pad = jnp.concatenate([c_prompt, c_sample, jnp.zeros((ADA_ROWS - BATCH - DEC_BATCH, d), F32)], axis=0)
    mod_shared = _ada_matmul(c_pad, w_ada)
    mod_tiles = jnp.concatenate(
        [jnp.broadcast_to(mod_shared[:BATCH, None, :], (BATCH, MOD_ROWS, 6 * d)),
         mod_shared[BATCH:BATCH + DEC_BATCH].reshape(SAMPLE_ROW_TILES, MOD_ROWS, 6 * d)], axis=0)
    ada_rows = ada_table.reshape(DEPTH * 6, 1, d)

    x = jnp.concatenate([x_prompt.reshape(N_PROMPT_TOK, d), x_sample.reshape(N_SAMPLE_TOK, d)], axis=0)

    st_sample = (state_ssd, state_ssd_conv, state_gdn, state_gdn_conv, state_mlstm_c,
                 state_mlstm_n, state_mlstm_m, state_rglru, state_rglru_conv)
    kst_sample = _kernel_states(st_sample)
    kst_prompt = _kernel_states(tuple(jnp.zeros((1,) + s, F32) for s in _state_shapes(BATCH)))
    new_p = [[] for _ in st_sample]
    new_s = [[] for _ in st_sample]

    y_prev = None
    for l in range(DEPTH):
        if l == 0:
            (h,) = _rowwise(x, norm1_g[l][None, :], mod_tiles=mod_tiles, ada_rows=ada_rows, layer=l,
                            mod_chunks=(1, 0), name="norm1")
        else:
            x, h = _rowwise(x, norm1_g[l][None, :], y=y_prev, mod_tiles=mod_tiles, ada_rows=ada_rows,
                            layer=l, gate_chunk=5, gate_layer=l - 1, mod_chunks=(1, 0), name="resid_norm1")
        prm = dict(ssd=_ssd_params(ssd_conv_w[l], ssd_conv_b[l], ssd_dt_bias[l], ssd_a_log[l], ssd_d[l],
                                   ssd_norm_g[l]),
                   gdn=_gdn_params(gdn_conv_w[l], gdn_conv_b[l], gdn_dt_bias[l], gdn_a_log[l], gdn_norm_g[l]),
                   mlstm=_mlstm_params(mlstm_i_bias[l], mlstm_f_bias[l], mlstm_norm_g[l]),
                   lru=_lru_params(lru_conv_w[l], lru_conv_b[l], lru_w_a[l], lru_b_a[l], lru_w_x[l], lru_b_x[l],
                                   lru_lambda[l]))
        ys_p, st_p = _mixer_group(h, w_in_wide[l], w_in_narrow[l], kst_prompt, prm, n_seq=BATCH, seq_len=SEQ,
                                  row0=0, g=PROMPT_SEQS_PER_STEP, layer=0)
        ys_s, st_s = _mixer_group(h, w_in_wide[l], w_in_narrow[l], kst_sample, prm, n_seq=DEC_BATCH,
                                  seq_len=DEC_SEQ, row0=N_PROMPT_TOK, g=SAMPLE_SEQS_PER_STEP, layer=l)
        for acc, s in zip(new_p, st_p):
            acc.append(s)
        for acc, s in zip(new_s, st_s):
            acc.append(s)
        y_mix = _out_proj(ys_p, ys_s, w_out_b[l])

        x, h2, gates = _rowwise(x, norm2_g[l][None, :], y=y_mix, mod_tiles=mod_tiles, ada_rows=ada_rows,
                                layer=l, gate_chunk=2, mod_chunks=(4, 3),
                                router=(w_router[l], b_router[l]), name="resid_norm2_router")
        src, valid, dest, tile_ea, tile_eb, n_used = _moe_plan(gates)
        x_sorted = _take_rows(h2, src)
        gates_sorted = _take_rows(gates, src) * valid[:, None]
        y_sorted = _moe(x_sorted, gates_sorted, tile_ea, tile_eb, n_used, w_gate_b, w_up_b, w_down_b, l)
        y_prev = _take_rows(y_sorted, dest)

    x, y_fin = _rowwise(x, final_g[None, :], y=y_prev, mod_tiles=mod_tiles, ada_rows=ada_rows,
                        layer=DEPTH - 1, gate_chunk=5, h_dtype=F32, name="resid_final_norm")
    outs = [y_fin[:N_PROMPT_TOK].reshape(BATCH, SEQ, d), y_fin[N_PROMPT_TOK:].reshape(DEC_BATCH, DEC_SEQ, d)]
    for acc_p, acc_s in zip(new_p, new_s):
        outs.append(jnp.stack(acc_p))
        outs.append(jnp.stack(acc_s))
    return tuple(outs)
```

```python
import functools

import numpy as np
import jax
import jax.numpy as jnp
from jax import lax
from jax.experimental import pallas as pl
from jax.experimental.pallas import tpu as pltpu

F32 = jnp.float32
BF16 = jnp.bfloat16

D_MODEL = 4096
BATCH = 4
SEQ = 2048
DEPTH = 4
DEC_BATCH = 128
DEC_SEQ = 8
EPS = 1e-6
N_MIXERS = 4
D_GROUP = D_MODEL // N_MIXERS
CONV_W = 4
CHUNK = 64
SSD_HEAD_DIM = 64
SSD_HEADS = D_GROUP // SSD_HEAD_DIM
SSD_NGROUPS = 2
SSD_STATE = 128
SSD_BC_DIM = 2 * SSD_NGROUPS * SSD_STATE
SSD_CONV_DIM = D_GROUP + SSD_BC_DIM
GDN_HEAD_DIM = 128
GDN_HEADS = D_GROUP // GDN_HEAD_DIM
GDN_CONV_DIM = 3 * D_GROUP
MLSTM_HEAD_DIM = 128
MLSTM_HEADS = D_GROUP // MLSTM_HEAD_DIM
LRU_BLOCKS = 8
LRU_BLOCK = D_GROUP // LRU_BLOCKS
LRU_C = 8.0
N_EXPERT_GROUPS = 4
EXPERTS_PER_GROUP = 4
N_EXPERTS = N_EXPERT_GROUPS * EXPERTS_PER_GROUP
D_FF_EXPERT = D_MODEL // 8
IN_SIZES = (D_GROUP, SSD_CONV_DIM, SSD_HEADS,
            GDN_CONV_DIM, GDN_HEADS, GDN_HEADS, D_GROUP,
            3 * D_GROUP, MLSTM_HEADS, MLSTM_HEADS, D_GROUP,
            D_GROUP, D_GROUP)
IN_OFFSETS = tuple(int(o) for o in np.cumsum((0,) + IN_SIZES))
(SEG_SSD_Z, SEG_SSD_XBC, SEG_SSD_DT, SEG_GDN_QKV, SEG_GDN_A, SEG_GDN_B, SEG_GDN_Z,
 SEG_ML_QKV, SEG_ML_I, SEG_ML_F, SEG_ML_O, SEG_LRU_X, SEG_LRU_GATE) = range(len(IN_SIZES))

N_PROMPT_TOK = BATCH * SEQ
N_SAMPLE_TOK = DEC_BATCH * DEC_SEQ
N_TOK = N_PROMPT_TOK + N_SAMPLE_TOK

LANES = 128
SUBLANES = 8
VMEM_LIMIT_BYTES = 56 * 1024 * 1024

ROW_TILE = 256
MOD_REPEAT = DEC_SEQ
MOD_ROWS = ROW_TILE // MOD_REPEAT
N_ROW_TILES = N_TOK // ROW_TILE
PROMPT_TILES_PER_SEQ = SEQ // ROW_TILE
PROMPT_ROW_TILES = N_PROMPT_TOK // ROW_TILE
SAMPLE_ROW_TILES = N_SAMPLE_TOK // ROW_TILE
MM_TM = 1024
MM_TN = 512
ADA_ROWS = 144
ADA_TN = 512
MOE_TM = 512
MOE_FF_TILE = 256
MOE_FF_STEPS = D_FF_EXPERT // MOE_FF_TILE
TOP_K_IN_GROUP = 2
PAIRS = tuple((a, b) for a in range(EXPERTS_PER_GROUP) for b in range(a + 1, EXPERTS_PER_GROUP))
N_PAIRS = len(PAIRS)
N_BUCKETS = N_EXPERT_GROUPS * N_PAIRS
MOE_STEPS_PER_TILE = TOP_K_IN_GROUP * MOE_FF_STEPS
MOE_MAX_TILES = N_TOK // MOE_TM + N_BUCKETS
MOE_SLOTS = MOE_MAX_TILES * MOE_TM
ROUTER_LANES = LANES
EXPERT_LANE0 = N_EXPERT_GROUPS
BUCKET_LANE = 0

_WIDE_SEGS = (SEG_GDN_QKV, SEG_ML_QKV, SEG_SSD_Z, SEG_GDN_Z, SEG_ML_O, SEG_LRU_X, SEG_LRU_GATE, SEG_SSD_XBC)
_NARROW_SEGS = (SEG_SSD_DT, SEG_GDN_A, SEG_GDN_B, SEG_ML_I, SEG_ML_F)
WIDE_OFF = {}
_o = 0
for _s in _WIDE_SEGS:
    WIDE_OFF[_s] = _o
    _o += IN_SIZES[_s]
WIDE_COLS = _o
NARROW_OFF = {}
_o = 0
for _s in _NARROW_SEGS:
    NARROW_OFF[_s] = _o
    _o += IN_SIZES[_s]
NARROW_COLS = LANES
LANE_SSD_DT = NARROW_OFF[SEG_SSD_DT]
LANE_GDN_A = NARROW_OFF[SEG_GDN_A]
LANE_GDN_B = NARROW_OFF[SEG_GDN_B]
LANE_ML_I = NARROW_OFF[SEG_ML_I]
LANE_ML_F = NARROW_OFF[SEG_ML_F]

PROMPT_SEQS_PER_STEP = 2
SAMPLE_SEQS_PER_STEP = 8
CONV_PAD = SUBLANES
CONV_PREV = CONV_W - 1


def _cparams(*sem):
    return pltpu.CompilerParams(dimension_semantics=sem, vmem_limit_bytes=VMEM_LIMIT_BYTES)


def _dot(a, b):
    return jnp.dot(a.astype(BF16), b.astype(BF16), preferred_element_type=F32)


def _dot_nt(a, b):
    return lax.dot_general(a.astype(BF16), b.astype(BF16), (((1,), (1,)), ((), ())), preferred_element_type=F32)


def _dot_tn(a, b):
    return lax.dot_general(a.astype(BF16), b.astype(BF16), (((0,), (0,)), ((), ())), preferred_element_type=F32)


def _dot_f32(a, b):
    return jnp.dot(a, b, preferred_element_type=F32, precision=lax.Precision.HIGHEST)


def _silu(x):
    return x * jax.nn.sigmoid(x)


def _mm_kernel(a_ref, w_ref, o_ref):
    o_ref[...] = jnp.dot(a_ref[...], w_ref[...], preferred_element_type=F32).astype(o_ref.dtype)


def _matmul(a, w, layer, tn, name, row0, m):
    k = a.shape[1]
    n = w.shape[2]
    assert m % MM_TM == 0 and row0 % MM_TM == 0 and n % tn == 0
    tile0 = row0 // MM_TM
    return pl.pallas_call(
        _mm_kernel,
        grid=(m // MM_TM, n // tn),
        in_specs=[pl.BlockSpec((MM_TM, k), lambda i, j: (tile0 + i, 0)),
                  pl.BlockSpec((None, k, tn), lambda i, j: (layer, 0, j))],
        out_specs=pl.BlockSpec((MM_TM, tn), lambda i, j: (i, j)),
        out_shape=jax.ShapeDtypeStruct((m, n), F32),
        compiler_params=_cparams("parallel", "arbitrary"),
        name=name,
    )(a, w)


PROMPT_MM_TILES = N_PROMPT_TOK // MM_TM
assert N_SAMPLE_TOK == MM_TM


def _out_proj_kernel(*refs):
    yp_refs, ys_refs = refs[:N_MIXERS], refs[N_MIXERS:2 * N_MIXERS]
    w_refs, o_ref = refs[2 * N_MIXERS:3 * N_MIXERS], refs[3 * N_MIXERS]

    def project(y_refs):
        acc = jnp.dot(y_refs[0][...], w_refs[0][...], preferred_element_type=F32)
        for m in range(1, N_MIXERS):
            acc += jnp.dot(y_refs[m][...], w_refs[m][...], preferred_element_type=F32)
        o_ref[...] = acc

    is_prompt = pl.program_id(0) < PROMPT_MM_TILES
    pl.when(is_prompt)(lambda: project(yp_refs))
    pl.when(jnp.logical_not(is_prompt))(lambda: project(ys_refs))


def _out_proj(ys_prompt, ys_sample, w, layer):
    n = w.shape[2]
    yp_spec = pl.BlockSpec((MM_TM, D_GROUP), lambda i, j: (jnp.minimum(i, PROMPT_MM_TILES - 1), 0))
    ys_spec = pl.BlockSpec((MM_TM, D_GROUP), lambda i, j: (0, 0))
    w_specs = [pl.BlockSpec((None, D_GROUP, MM_TN), functools.partial(lambda i, j, m: (layer, m, j), m=m))
               for m in range(N_MIXERS)]
    return pl.pallas_call(
        _out_proj_kernel,
        grid=(N_TOK // MM_TM, n // MM_TN),
        in_specs=[yp_spec] * N_MIXERS + [ys_spec] * N_MIXERS + w_specs,
        out_specs=pl.BlockSpec((MM_TM, MM_TN), lambda i, j: (i, j)),
        out_shape=jax.ShapeDtypeStruct((N_TOK, n), F32),
        compiler_params=_cparams("parallel", "arbitrary"),
        name="out_proj",
    )(*ys_prompt, *ys_sample, w, w, w, w)


W_IN_ROW_TILE = 256


def _w_in_relayout_kernel(w_ref, wide_ref, narrow_ref):
    for s in _WIDE_SEGS:
        wide_ref[:, WIDE_OFF[s]:WIDE_OFF[s] + IN_SIZES[s]] = w_ref[:, IN_OFFSETS[s]:IN_OFFSETS[s + 1]].astype(BF16)
    narrow_ref[...] = jnp.zeros_like(narrow_ref)
    for s in _NARROW_SEGS:
        narrow_ref[:, NARROW_OFF[s]:NARROW_OFF[s] + IN_SIZES[s]] = (
            w_ref[:, IN_OFFSETS[s]:IN_OFFSETS[s + 1]].astype(BF16))


def _w_in_relayout(w_in):
    depth, k, n = w_in.shape
    return pl.pallas_call(
        _w_in_relayout_kernel,
        grid=(depth, k // W_IN_ROW_TILE),
        in_specs=[pl.BlockSpec((None, W_IN_ROW_TILE, n), lambda l, i: (l, i, 0))],
        out_specs=[pl.BlockSpec((None, W_IN_ROW_TILE, WIDE_COLS), lambda l, i: (l, i, 0)),
                   pl.BlockSpec((None, W_IN_ROW_TILE, NARROW_COLS), lambda l, i: (l, i, 0))],
        out_shape=[jax.ShapeDtypeStruct((depth, k, WIDE_COLS), BF16),
                   jax.ShapeDtypeStruct((depth, k, NARROW_COLS), BF16)],
        compiler_params=_cparams("parallel", "parallel"),
        name="w_in_relayout",
    )(w_in)


def _ada_kernel(c_ref, w_ref, o_ref):
    o_ref[...] = _dot(_silu(c_ref[...]), w_ref[...])


def _ada_matmul(c_pad, w_ada):
    k, n = w_ada.shape
    return pl.pallas_call(
        _ada_kernel,
        grid=(n // ADA_TN,),
        in_specs=[pl.BlockSpec((ADA_ROWS, k), lambda j: (0, 0)),
                  pl.BlockSpec((k, ADA_TN), lambda j: (0, j))],
        out_specs=pl.BlockSpec((ADA_ROWS, ADA_TN), lambda j: (0, j)),
        out_shape=jax.ShapeDtypeStruct((ADA_ROWS, n), F32),
        compiler_params=_cparams("arbitrary"),
        name="adaln_matmul",
    )(c_pad, w_ada)


def _route(lg):
    lane = lax.broadcasted_iota(jnp.int32, lg.shape, 1).astype(F32)
    neg = -jnp.inf
    big = float(ROUTER_LANES)
    gl = jnp.where(lane < N_EXPERT_GROUPS, lg, neg)
    gmax = jnp.max(gl, axis=-1, keepdims=True)
    gidx = jnp.min(jnp.where(gl == gmax, lane, big), axis=-1, keepdims=True)
    g_w = 1.0 / jnp.sum(jnp.exp(gl - gmax), axis=-1, keepdims=True)
    lo = EXPERT_LANE0 + EXPERTS_PER_GROUP * gidx
    el = jnp.where((lane >= lo) & (lane < lo + EXPERTS_PER_GROUP), lg, neg)
    e1 = jnp.max(el, axis=-1, keepdims=True)
    i1 = jnp.min(jnp.where(el == e1, lane, big), axis=-1, keepdims=True)
    el2 = jnp.where(lane == i1, neg, el)
    e2 = jnp.max(el2, axis=-1, keepdims=True)
    i2 = jnp.min(jnp.where(el2 == e2, lane, big), axis=-1, keepdims=True)
    t = jnp.exp(e2 - e1)
    w1 = g_w / (1.0 + t)
    w2 = g_w * t / (1.0 + t)
    out = jnp.where(lane == i1, w1, 0.0) + jnp.where(lane == i2, w2, 0.0)
    pa = jnp.minimum(i1, i2) - lo
    pb = jnp.maximum(i1, i2) - lo
    pair = pa * (2 * EXPERTS_PER_GROUP - 1 - pa) * 0.5 + (pb - pa - 1.0)
    return out + jnp.where(lane == float(BUCKET_LANE), gidx * N_PAIRS + pair, 0.0)


def _rowwise_kernel(*refs, has_resid, has_mod, has_router):
    it = iter(refs)
    x_ref = next(it)
    if has_resid:
        y_ref, gate_m, gate_t = next(it), next(it), next(it)
    g_ref = next(it)
    if has_mod:
        sc_m, sc_t, sh_m, sh_t = next(it), next(it), next(it), next(it)
    if has_router:
        wr_ref, br_ref = next(it), next(it)
    if has_resid:
        xo_ref = next(it)
    h_ref = next(it)
    if has_router:
        gt_ref = next(it)

    def mod(m_ref, t_ref):
        m = m_ref[...] + t_ref[...]
        return jnp.broadcast_to(m[:, None, :], (MOD_ROWS, MOD_REPEAT, m.shape[-1])).reshape(ROW_TILE, m.shape[-1])

    x = x_ref[...]
    if has_resid:
        x = x + mod(gate_m, gate_t) * y_ref[...]
        xo_ref[...] = x
    y = x * lax.rsqrt(jnp.mean(x * x, axis=-1, keepdims=True) + EPS)
    h = y * g_ref[...]
    if has_mod:
        h = h * (1.0 + mod(sc_m, sc_t)) + mod(sh_m, sh_t)
    h_ref[...] = h.astype(h_ref.dtype)
    if has_router:
        h_hi = h.astype(BF16)
        h_lo = (h - h_hi.astype(F32)).astype(BF16)
        hi_terms = jnp.dot(h_hi, wr_ref[...], preferred_element_type=F32)
        lg = (hi_terms[:, :ROUTER_LANES] + hi_terms[:, ROUTER_LANES:]
              + jnp.dot(h_lo, wr_ref[:, :ROUTER_LANES], preferred_element_type=F32))
        gt_ref[...] = _route(lg + br_ref[...])


def _mod_tile_index(i):
    return jnp.where(i < PROMPT_ROW_TILES, i // PROMPT_TILES_PER_SEQ, BATCH + i - PROMPT_ROW_TILES)


def _rowwise(x, g_row, *, y=None, mod_tiles=None, ada_rows=None, layer=0, gate_chunk=None, gate_layer=None,
             mod_chunks=None, router=None, h_dtype=BF16, name="rowwise"):
    has_resid = y is not None
    has_mod = mod_chunks is not None
    has_router = router is not None
    d = D_MODEL
    row_spec = pl.BlockSpec((ROW_TILE, d), lambda i: (i, 0))
    vec_spec = pl.BlockSpec((1, d), lambda i: (0, 0))

    def mod_specs(chunk, lyr):
        return [pl.BlockSpec((None, MOD_ROWS, d), lambda i: (_mod_tile_index(i), 0, chunk)),
                pl.BlockSpec((None, 1, d), lambda i: (lyr * 6 + chunk, 0, 0))]

    args, specs = [x], [row_spec]
    if has_resid:
        args += [y, mod_tiles, ada_rows]
        specs += [row_spec] + mod_specs(gate_chunk, layer if gate_layer is None else gate_layer)
    args.append(g_row)
    specs.append(vec_spec)
    if has_mod:
        sc_chunk, sh_chunk = mod_chunks
        args += [mod_tiles, ada_rows, mod_tiles, ada_rows]
        specs += mod_specs(sc_chunk, layer) + mod_specs(sh_chunk, layer)
    if has_router:
        w_r, b_r = router
        args += [w_r, b_r]
        specs += [pl.BlockSpec((d, 2 * ROUTER_LANES), lambda i: (0, 0)),
                  pl.BlockSpec((1, ROUTER_LANES), lambda i: (0, 0))]
    out_shape, out_specs = [], []
    if has_resid:
        out_shape.append(jax.ShapeDtypeStruct((N_TOK, d), F32))
        out_specs.append(row_spec)
    out_shape.append(jax.ShapeDtypeStruct((N_TOK, d), h_dtype))
    out_specs.append(row_spec)
    if has_router:
        out_shape.append(jax.ShapeDtypeStruct((N_TOK, ROUTER_LANES), F32))
        out_specs.append(pl.BlockSpec((ROW_TILE, ROUTER_LANES), lambda i: (i, 0)))
    return pl.pallas_call(
        functools.partial(_rowwise_kernel, has_resid=has_resid, has_mod=has_mod, has_router=has_router),
        grid=(N_ROW_TILES,),
        in_specs=specs,
        out_specs=out_specs,
        out_shape=out_shape,
        compiler_params=_cparams("parallel"),
        name=name,
    )(*args)


def _moe_kernel(ea_ref, eb_ref, nu_ref, x_ref, gt_ref, wg_ref, wu_ref, wd_ref, o_ref):
    i = pl.program_id(0)
    j = pl.program_id(1)
    used = i < nu_ref[0]

    @pl.when(jnp.logical_and(j == 0, jnp.logical_not(used)))
    def _():
        o_ref[...] = jnp.zeros_like(o_ref)

    @pl.when(used)
    def _():
        x = x_ref[...]
        a = jnp.dot(x, wg_ref[...], preferred_element_type=F32)
        b = jnp.dot(x, wu_ref[...], preferred_element_type=F32)
        he = _silu(a) * b
        gates = gt_ref[...]
        lane = lax.broadcasted_iota(jnp.int32, gates.shape, 1)
        col = EXPERT_LANE0 + jnp.where(j < MOE_FF_STEPS, ea_ref[i], eb_ref[i])
        gcol = jnp.sum(jnp.where(lane == col, gates, 0.0), axis=-1, keepdims=True)
        contrib = gcol * jnp.dot(he.astype(BF16), wd_ref[...], preferred_element_type=F32)

        @pl.when(j == 0)
        def _():
            o_ref[...] = contrib

        @pl.when(j > 0)
        def _():
            o_ref[...] += contrib


def _moe(x_sorted, gates_sorted, tile_ea, tile_eb, n_used, w_gate, w_up, w_down, layer):
    d = D_MODEL

    def expert_of(i, j, ea, eb, nu):
        jj = jnp.where(i < nu[0], j, MOE_STEPS_PER_TILE - 1)
        return jnp.where(jj < MOE_FF_STEPS, ea[i], eb[i]), jj % MOE_FF_STEPS

    def in_row_map(i, j, ea, eb, nu):
        return (jnp.minimum(i, jnp.minimum(nu[0], MOE_MAX_TILES - 1)), 0)

    def wgu_map(i, j, ea, eb, nu):
        e, f = expert_of(i, j, ea, eb, nu)
        return (layer, e, 0, f)

    def wd_map(i, j, ea, eb, nu):
        e, f = expert_of(i, j, ea, eb, nu)
        return (layer, e, f, 0)

    grid_spec = pltpu.PrefetchScalarGridSpec(
        num_scalar_prefetch=3,
        grid=(MOE_MAX_TILES, MOE_STEPS_PER_TILE),
        in_specs=[pl.BlockSpec((MOE_TM, d), in_row_map),
                  pl.BlockSpec((MOE_TM, ROUTER_LANES), in_row_map),
                  pl.BlockSpec((None, None, d, MOE_FF_TILE), wgu_map),
                  pl.BlockSpec((None, None, d, MOE_FF_TILE), wgu_map),
                  pl.BlockSpec((None, None, MOE_FF_TILE, d), wd_map)],
        out_specs=pl.BlockSpec((MOE_TM, d), lambda i, j, ea, eb, nu: (i, 0)),
    )
    return pl.pallas_call(
        _moe_kernel,
        grid_spec=grid_spec,
        out_shape=jax.ShapeDtypeStruct((MOE_SLOTS, d), F32),
        compiler_params=_cparams("arbitrary", "arbitrary"),
        name="moe_experts",
    )(tile_ea, tile_eb, n_used, x_sorted, gates_sorted, w_gate, w_up, w_down)


def _take_rows(a, idx):
    return a.at[idx].get(mode="promise_in_bounds")


def _moe_plan(gates):
    bid = gates[:, BUCKET_LANE].astype(jnp.int32)
    onehot = (bid[:, None] == jnp.arange(N_BUCKETS, dtype=jnp.int32)[None, :]).astype(jnp.int32)
    rank = jnp.take_along_axis(jnp.cumsum(onehot, axis=0), bid[:, None], axis=1)[:, 0] - 1
    counts = jnp.sum(onehot, axis=0)
    tiles = (counts + MOE_TM - 1) // MOE_TM
    tile_end = jnp.cumsum(tiles)
    tile_start = tile_end - tiles
    dest = tile_start[bid] * MOE_TM + rank
    src = (jnp.arange(MOE_SLOTS, dtype=jnp.int32) % N_TOK).at[dest].set(jnp.arange(N_TOK, dtype=jnp.int32))
    n_used = tile_end[-1]
    t = jnp.arange(MOE_MAX_TILES, dtype=jnp.int32)
    tile_bucket = jnp.sum((jnp.minimum(t, n_used - 1)[:, None] >= tile_end[None, :]).astype(jnp.int32), axis=1)
    slot = jnp.arange(MOE_SLOTS, dtype=jnp.int32)
    slot_bucket = jnp.repeat(tile_bucket, MOE_TM)
    valid = ((slot - tile_start[slot_bucket] * MOE_TM < counts[slot_bucket])
             & (slot < n_used * MOE_TM)).astype(F32)
    group0 = (tile_bucket // N_PAIRS) * EXPERTS_PER_GROUP
    pair = tile_bucket % N_PAIRS
    tile_ea = group0 + jnp.asarray([p[0] for p in PAIRS], jnp.int32)[pair]
    tile_eb = group0 + jnp.asarray([p[1] for p in PAIRS], jnp.int32)[pair]
    return src, valid, dest, tile_ea, tile_eb, n_used.reshape(1).astype(jnp.int32)


def _conv_step(u_ref, win_scr, w_ref, b_ref, cs):
    win_scr[CONV_PAD:CONV_PAD + cs, :] = u_ref[...]
    lo = CONV_PAD - CONV_PREV
    out = b_ref[...] + win_scr[lo:lo + cs, :] * w_ref[0:1, :]
    for j in range(1, CONV_W):
        out = out + win_scr[lo + j:lo + j + cs, :] * w_ref[j:j + 1, :]
    return out


def _conv_tail(win_scr, cs):
    return win_scr[CONV_PAD + cs - CONV_PREV:CONV_PAD + cs, :]


def _conv_advance(win_scr, cs):
    win_scr[CONV_PAD - CONV_PREV:CONV_PAD, :] = _conv_tail(win_scr, cs)


def _causal_masks(cs):
    row = lax.broadcasted_iota(jnp.int32, (cs, cs), 0)
    col = lax.broadcasted_iota(jnp.int32, (cs, cs), 1)
    return row >= col, row > col


def _cumsum_rows(x):
    cs = x.shape[0]
    row = lax.broadcasted_iota(jnp.int32, x.shape, 0)
    shift = 1
    while shift < cs:
        x = x + jnp.where(row >= shift, pltpu.roll(x, shift, 0), 0.0)
        shift *= 2
    return x


def _rms(x, width):
    return x * lax.rsqrt(jnp.sum(x * x, axis=-1, keepdims=True) * (1.0 / width) + EPS)


def _seq_specs(cs, layer, g):
    def rows(width, col):
        return pl.BlockSpec((g, cs, width), lambda b, c: (b, c, col))

    def const(shape):
        nd = len(shape)
        return pl.BlockSpec(shape, lambda b, c: (0,) * nd)

    def per_seq(shape, last=0):
        nd = len(shape)
        return pl.BlockSpec((g,) + shape, lambda b, c: (b,) + (0,) * (nd - 1) + (last,))

    def state_in(shape, last=0):
        nd = len(shape)
        return pl.BlockSpec((None, g) + shape, lambda b, c: (layer, b) + (0,) * (nd - 1) + (last,))

    return rows, const, per_seq, state_in


def _seq_batched_kernel(*refs, body, per_seq_refs, alias_range, g, cs, nc):
    refs = refs[:alias_range[0]] + refs[alias_range[1]:]
    c = pl.program_id(1)
    views = [tuple(r.at[s] if flag else r for r, flag in zip(refs, per_seq_refs)) for s in range(g)]

    @pl.when(c == 0)
    def _():
        for v in views:
            body(*v, cs=cs, phase="init")

    for v in views:
        body(*v, cs=cs, phase="main")

    @pl.when(c == nc - 1)
    def _():
        for v in views:
            body(*v, cs=cs, phase="final")


def _seq_call(body, name, in_arrays, in_specs, n_const, state_shapes, scratch, prev, *,
              n_seq, seq_len, cs, g, out_layer):
    assert n_seq % g == 0
    nc = seq_len // cs
    n_out = 1 + len(state_shapes)
    flags = ([True] * (len(in_arrays) - n_const) + [False] * n_const + [True] * (n_out + len(scratch)))
    out_specs = [pl.BlockSpec((g, cs, D_GROUP), lambda b, c: (b, c, 0))]
    out_shape = [jax.ShapeDtypeStruct((n_seq, seq_len, D_GROUP), BF16)]
    for shape in state_shapes:
        nd = len(shape)
        out_specs.append(pl.BlockSpec((None, g) + shape, lambda b, c, nd=nd: (out_layer, b) + (0,) * nd))
        out_shape.append(jax.ShapeDtypeStruct((DEPTH, n_seq) + shape, F32))
    n_in = len(in_arrays)
    aliases = {}
    if prev is not None:
        in_arrays = tuple(in_arrays) + tuple(prev)
        in_specs = list(in_specs) + [pl.BlockSpec(memory_space=pl.ANY)] * len(prev)
        aliases = {n_in + k: 1 + k for k in range(len(prev))}
    return pl.pallas_call(
        functools.partial(_seq_batched_kernel, body=body, per_seq_refs=tuple(flags),
                          alias_range=(n_in, len(in_arrays)), g=g, cs=cs, nc=nc),
        grid=(n_seq // g, nc),
        in_specs=in_specs,
        out_specs=out_specs,
        out_shape=out_shape,
        scratch_shapes=[pltpu.VMEM((g,) + shape, F32) for shape in scratch],
        input_output_aliases=aliases,
        compiler_params=_cparams("parallel", "arbitrary"),
        name=name,
    )(*in_arrays)


def _ssd_kernel(z_ref, x_ref, bc_ref, nar_ref, s0_ref, c0x_ref, c0bc_ref,
                cwx_ref, cbx_ref, cwbc_ref, cbbc_ref, bias_ref, alog_ref, dsk_ref, ng_ref,
                y_ref, s_out_ref, cx_out_ref, cbc_out_ref,
                st_scr, wx_scr, wbc_scr, ycat_scr, *, cs, phase):
    lo = CONV_PAD - CONV_PREV
    if phase == "init":
        st_scr[...] = s0_ref[...]
        wx_scr[lo:CONV_PAD, :] = c0x_ref[...]
        wbc_scr[lo:CONV_PAD, :] = c0bc_ref[...]
        return
    if phase == "final":
        s_out_ref[...] = st_scr[...]
        cx_out_ref[...] = wx_scr[lo:CONV_PAD, :]
        cbc_out_ref[...] = wbc_scr[lo:CONV_PAD, :]
        return

    xs = _silu(_conv_step(x_ref, wx_scr, cwx_ref, cbx_ref, cs))
    bcs = _silu(_conv_step(bc_ref, wbc_scr, cwbc_ref, cbbc_ref, cs))
    incl, _ = _causal_masks(cs)
    dt = jax.nn.softplus(nar_ref[...] + bias_ref[...])
    cum = _cumsum_rows(dt * (-jnp.exp(alog_ref[...])))
    cum_t = cum.T
    exp_cum = jnp.exp(cum)
    cum_last = cum[cs - 1:cs, :]
    w_end = jnp.exp(cum_last - cum)
    chunk_decay = jnp.exp(cum_last)
    heads_per_group = SSD_HEADS // SSD_NGROUPS
    heads = range(SSD_HEADS)
    hp = SSD_HEAD_DIM
    s_old = [st_scr[h] for h in heads]
    col = lambda arr, h: arr[:, LANE_SSD_DT + h:LANE_SSD_DT + h + 1]
    b_gs = [bcs[:, g * SSD_STATE:(g + 1) * SSD_STATE] for g in range(SSD_NGROUPS)]
    c_gs = [bcs[:, (SSD_NGROUPS + g) * SSD_STATE:(SSD_NGROUPS + g + 1) * SSD_STATE] for g in range(SSD_NGROUPS)]
    cbs = [_dot_nt(c_gs[g], b_gs[g]) for g in range(SSD_NGROUPS)]
    grp = [h // heads_per_group for h in heads]
    scores = [cbs[grp[h]] * jnp.exp(jnp.where(incl, col(cum, h) - cum_t[LANE_SSD_DT + h:LANE_SSD_DT + h + 1, :],
                                              -jnp.inf)) for h in heads]
    xdts = [xs[:, h * hp:(h + 1) * hp] * col(dt, h) for h in heads]
    ys = [_dot(scores[h], xdts[h]) + _dot_nt(c_gs[grp[h]] * col(exp_cum, h), s_old[h]) for h in heads]
    s_new = [s_old[h] * col(chunk_decay, h) + _dot_tn(xdts[h] * col(w_end, h), b_gs[grp[h]]) for h in heads]
    for h in heads:
        st_scr[h] = s_new[h]
        ycat_scr[:, h * hp:(h + 1) * hp] = ys[h]
    y = (ycat_scr[...] + dsk_ref[...] * xs) * _silu(z_ref[...])
    gw = D_GROUP // SSD_NGROUPS
    for g in range(SSD_NGROUPS):
        y_g = _rms(y[:, g * gw:(g + 1) * gw], gw) * ng_ref[:, g * gw:(g + 1) * gw]
        y_ref[:, g * gw:(g + 1) * gw] = y_g.astype(y_ref.dtype)
    _conv_advance(wx_scr, cs)
    _conv_advance(wbc_scr, cs)


def _ssd_call(wide, narrow, s0, conv0, prm, prev, *, n_seq, seq_len, cs, g, layer, out_layer):
    rows, const, per_seq, state_in = _seq_specs(cs, layer, g)
    xw, bcw = D_GROUP, SSD_BC_DIM
    x_off = WIDE_OFF[SEG_SSD_XBC]
    in_specs = [rows(xw, WIDE_OFF[SEG_SSD_Z] // xw), rows(xw, x_off // xw), rows(bcw, (x_off + xw) // bcw),
                rows(NARROW_COLS, 0),
                state_in((SSD_HEADS, SSD_HEAD_DIM, SSD_STATE)),
                state_in((CONV_PREV, xw)), state_in((CONV_PREV, bcw), last=xw // bcw),
                const((CONV_W, xw)), const((1, xw)), const((CONV_W, bcw)), const((1, bcw)),
                const((1, LANES)), const((1, LANES)), const((1, xw)), const((1, xw))]
    state_shapes = [(SSD_HEADS, SSD_HEAD_DIM, SSD_STATE), (CONV_PREV, xw), (CONV_PREV, bcw)]
    scratch = [(SSD_HEADS, SSD_HEAD_DIM, SSD_STATE), (CONV_PAD + cs, xw), (CONV_PAD + cs, bcw), (cs, xw)]
    in_arrays = (wide, wide, wide, narrow, s0, conv0, conv0,
                 prm['cw'][:, :xw], prm['cb'][:, :xw], prm['cw'][:, xw:], prm['cb'][:, xw:],
                 prm['bias'], prm['alog'], prm['dskip'], prm['ng'])
    return _seq_call(_ssd_kernel, "ssd_mixer", in_arrays, in_specs, 8, state_shapes, scratch, prev,
                     n_seq=n_seq, seq_len=seq_len, cs=cs, g=g, out_layer=out_layer)


def _ssd_params(conv_w, conv_b, dt_bias, a_log, d_skip, norm_g):
    pad = jnp.zeros((LANES - SSD_HEADS,), F32)
    return dict(cw=conv_w, cb=conv_b[None, :],
                bias=jnp.concatenate([dt_bias, pad])[None, :],
                alog=jnp.concatenate([a_log, pad])[None, :],
                dskip=jnp.repeat(d_skip, SSD_HEAD_DIM)[None, :], ng=norm_g[None, :])


def _split_hi_lo(x):
    hi = x.astype(BF16).astype(F32)
    lo = (x - hi).astype(BF16).astype(F32)
    return hi, lo


def _lhs3(a):
    hi, lo = _split_hi_lo(a)
    return jnp.concatenate([hi, lo, hi], axis=1).astype(BF16)


def _rhs3(b):
    hi, lo = _split_hi_lo(b)
    return jnp.concatenate([hi, hi, lo], axis=0).astype(BF16)


def _dot3(lhs3, rhs3):
    return jnp.dot(lhs3, rhs3, preferred_element_type=F32)


def _inv_unit_lower(a_list, cs):
    row = lax.broadcasted_iota(jnp.int32, (cs, cs), 0)
    col = lax.broadcasted_iota(jnp.int32, (cs, cs), 1)
    eye = jnp.where(row == col, 1.0, 0.0)
    ps = [-a for a in a_list]
    ts = [eye + p for p in ps]
    forms = [(_lhs3(p), _rhs3(p)) for p in ps]
    n = 1
    while 2 * n < cs:
        ps = [_dot3(lhs, rhs) for lhs, rhs in forms]
        forms = [(_lhs3(p), _rhs3(p)) for p in ps]
        ts = [t + _dot3(_lhs3(t), rhs) for t, (_, rhs) in zip(ts, forms)]
        n *= 2
    return ts


def _gdn_kernel(qkv_ref, z_ref, nar_ref, s0_ref, c0_ref, cw_ref, cb_ref, bias_ref, alog_ref, ng_ref,
                y_ref, s_out_ref, c_out_ref, st_scr, win_scr, *, cs, phase):
    if phase == "init":
        st_scr[...] = s0_ref[...]
        win_scr[CONV_PAD - CONV_PREV:CONV_PAD, :] = c0_ref[...]
        return
    if phase == "final":
        s_out_ref[...] = st_scr[...]
        c_out_ref[...] = win_scr[CONV_PAD - CONV_PREV:CONV_PAD, :]
        return

    qkv = _silu(_conv_step(qkv_ref, win_scr, cw_ref, cb_ref, cs))
    incl, strict = _causal_masks(cs)
    nar = nar_ref[...]
    g_log = -jnp.exp(alog_ref[...]) * jax.nn.softplus(nar + bias_ref[...])
    beta_all = jax.nn.sigmoid(nar)
    gc = _cumsum_rows(g_log)
    gc_t = gc.T
    exp_gc = jnp.exp(gc)
    gc_last = gc[cs - 1:cs, :]
    exp_to_end = jnp.exp(gc_last - gc)
    g_last = jnp.exp(gc_last)
    hd = GDN_HEAD_DIM
    heads = range(GDN_HEADS)
    s_old = [st_scr[h] for h in heads]
    col = lambda arr, lane: arr[:, lane:lane + 1]
    qs = [qkv[:, h * hd:(h + 1) * hd] for h in heads]
    ks = [qkv[:, D_GROUP + h * hd:D_GROUP + (h + 1) * hd] for h in heads]
    vs = [qkv[:, 2 * D_GROUP + h * hd:2 * D_GROUP + (h + 1) * hd] for h in heads]
    qs = [q * lax.rsqrt(jnp.sum(q * q, axis=-1, keepdims=True) + EPS) * (hd ** -0.5) for q in qs]
    ks = [k * lax.rsqrt(jnp.sum(k * k, axis=-1, keepdims=True) + EPS) for k in ks]
    decays = [jnp.exp(jnp.where(incl, col(gc, LANE_GDN_A + h) - gc_t[LANE_GDN_A + h:LANE_GDN_A + h + 1, :],
                                -jnp.inf)) for h in heads]
    betas = [col(beta_all, LANE_GDN_B + h) for h in heads]
    a_mats = [jnp.where(strict, betas[h] * _dot_nt(ks[h], ks[h]) * decays[h], 0.0) for h in heads]
    t_invs = _inv_unit_lower(a_mats, cs)
    rhs = [jnp.concatenate([vs[h] * betas[h], ks[h] * (betas[h] * col(exp_gc, LANE_GDN_A + h))], axis=1)
           for h in heads]
    uw = [_dot3(_lhs3(t_invs[h]), _rhs3(rhs[h])) for h in heads]
    qks = [_dot_nt(qs[h], ks[h]) * decays[h] for h in heads]
    v_new = [uw[h][:, :hd] - _dot(uw[h][:, hd:], s_old[h]) for h in heads]
    outs = [_dot(qs[h] * col(exp_gc, LANE_GDN_A + h), s_old[h]) + _dot(qks[h], v_new[h]) for h in heads]
    s_new = [s_old[h] * col(g_last, LANE_GDN_A + h) + _dot_tn(ks[h] * col(exp_to_end, LANE_GDN_A + h), v_new[h])
             for h in heads]
    for h in heads:
        st_scr[h] = s_new[h]
    y = jnp.concatenate([_rms(o, hd) for o in outs], axis=1) * ng_ref[...] * _silu(z_ref[...])
    y_ref[...] = y.astype(y_ref.dtype)
    _conv_advance(win_scr, cs)


def _gdn_call(wide, narrow, s0, conv0, prm, prev, *, n_seq, seq_len, cs, g, layer, out_layer):
    rows, const, per_seq, state_in = _seq_specs(cs, layer, g)
    hd = GDN_HEAD_DIM
    in_specs = [rows(GDN_CONV_DIM, WIDE_OFF[SEG_GDN_QKV] // GDN_CONV_DIM),
                rows(D_GROUP, WIDE_OFF[SEG_GDN_Z] // D_GROUP),
                rows(NARROW_COLS, 0),
                state_in((GDN_HEADS, hd, hd)), state_in((CONV_PREV, GDN_CONV_DIM)),
                const((CONV_W, GDN_CONV_DIM)), const((1, GDN_CONV_DIM)),
                const((1, LANES)), const((1, LANES)), const((1, D_GROUP))]
    state_shapes = [(GDN_HEADS, hd, hd), (CONV_PREV, GDN_CONV_DIM)]
    scratch = [(GDN_HEADS, hd, hd), (CONV_PAD + cs, GDN_CONV_DIM)]
    in_arrays = (wide, wide, narrow, s0, conv0, prm['cw'], prm['cb'], prm['bias'], prm['alog'], prm['ng'])
    return _seq_call(_gdn_kernel, "gdn_mixer", in_arrays, in_specs, 5, state_shapes, scratch, prev,
                     n_seq=n_seq, seq_len=seq_len, cs=cs, g=g, out_layer=out_layer)


def _lane_row(vec, lane0):
    return jnp.zeros((LANES,), F32).at[lane0:lane0 + vec.shape[0]].set(vec)[None, :]


def _gdn_params(conv_w, conv_b, dt_bias, a_log, norm_g):
    return dict(cw=conv_w, cb=conv_b[None, :], bias=_lane_row(dt_bias, LANE_GDN_A),
                alog=_lane_row(a_log, LANE_GDN_A), ng=norm_g[None, :])


def _mlstm_kernel(qkv_ref, o_ref, nar_ref, c0_ref, n0_ref, m0_ref, bias_ref, ng_ref,
                  y_ref, c_out_ref, n_out_ref, m_out_ref, c_scr, n_scr, m_scr, *, cs, phase):
    if phase == "init":
        c_scr[...] = c0_ref[...]
        n_scr[...] = n0_ref[...]
        m_scr[...] = m0_ref[...]
        return
    if phase == "final":
        c_out_ref[...] = c_scr[...]
        n_out_ref[...] = n_scr[...]
        m_out_ref[...] = m_scr[...]
        return

    incl, _ = _causal_masks(cs)
    pre = nar_ref[...] + bias_ref[...]
    f_cum = _cumsum_rows(jax.nn.log_sigmoid(pre))
    f_cum_t = f_cum.T
    pre_t = pre.T
    hd = MLSTM_HEAD_DIM
    heads = range(MLSTM_HEADS)
    c_old = [c_scr[h] for h in heads]
    n_all = n_scr[...]
    m_all = m_scr[...]
    qkv = qkv_ref[...]
    col = lambda arr, lane: arr[:, lane:lane + 1]
    qs = [qkv[:, h * hd:(h + 1) * hd] for h in heads]
    ks = [qkv[:, D_GROUP + h * hd:D_GROUP + (h + 1) * hd] * (hd ** -0.5) for h in heads]
    vs = [qkv[:, 2 * D_GROUP + h * hd:2 * D_GROUP + (h + 1) * hd] for h in heads]
    f_cols = [col(f_cum, LANE_ML_F + h) for h in heads]
    d_logs = [jnp.where(incl, f_cols[h] - f_cum_t[LANE_ML_F + h:LANE_ML_F + h + 1, :]
                        + pre_t[LANE_ML_I + h:LANE_ML_I + h + 1, :], -jnp.inf) for h in heads]
    m_prev = [col(m_all, LANE_ML_I + h) for h in heads]
    m_t = [jnp.maximum(f_cols[h] + m_prev[h], jnp.max(d_logs[h], axis=-1, keepdims=True)) for h in heads]
    w_carry = [jnp.exp(f_cols[h] + m_prev[h] - m_t[h]) for h in heads]
    ps = [jnp.exp(d_logs[h] - m_t[h]) * _dot_nt(qs[h], ks[h]) for h in heads]
    num = [w_carry[h] * _dot(qs[h], c_old[h]) + _dot(ps[h], vs[h]) for h in heads]
    den = [w_carry[h] * jnp.sum(qs[h] * n_all[h:h + 1, :], axis=-1, keepdims=True)
           + jnp.sum(ps[h], axis=-1, keepdims=True) for h in heads]
    hs = [num[h] / jnp.maximum(jnp.abs(den[h]), jnp.exp(-m_t[h])) for h in heads]
    m_end = [m_t[h][cs - 1:cs, :] for h in heads]
    f_last = [f_cols[h][cs - 1:cs, :] for h in heads]
    w_prev = [jnp.exp(f_last[h] + m_prev[h] - m_end[h]) for h in heads]
    kws = [ks[h] * jnp.exp(f_last[h] - f_cols[h] + col(pre, LANE_ML_I + h) - m_end[h]) for h in heads]
    c_new = [w_prev[h] * c_old[h] + _dot_tn(kws[h], vs[h]) for h in heads]
    n_new = [w_prev[h] * n_all[h:h + 1, :] + jnp.sum(kws[h], axis=0, keepdims=True) for h in heads]
    lane = lax.broadcasted_iota(jnp.int32, m_all.shape, 1)
    m_new = m_all
    for h in heads:
        c_scr[h] = c_new[h]
        m_new = jnp.where(lane == LANE_ML_I + h, m_end[h], m_new)
    n_scr[...] = jnp.concatenate(n_new, axis=0)
    m_scr[...] = m_new
    y = jnp.concatenate([_rms(hh, hd) for hh in hs], axis=1) * ng_ref[...] * jax.nn.sigmoid(o_ref[...])
    y_ref[...] = y.astype(y_ref.dtype)


def _mlstm_call(wide, narrow, c0, n0, m0, prm, prev, *, n_seq, seq_len, cs, g, layer, out_layer):
    rows, const, per_seq, state_in = _seq_specs(cs, layer, g)
    hd = MLSTM_HEAD_DIM
    qkv_w = 3 * D_GROUP
    in_specs = [rows(qkv_w, WIDE_OFF[SEG_ML_QKV] // qkv_w),
                rows(D_GROUP, WIDE_OFF[SEG_ML_O] // D_GROUP),
                rows(NARROW_COLS, 0),
                state_in((MLSTM_HEADS, hd, hd)), state_in((MLSTM_HEADS, hd)), state_in((1, LANES)),
                const((1, LANES)), const((1, D_GROUP))]
    state_shapes = [(MLSTM_HEADS, hd, hd), (MLSTM_HEADS, hd), (1, LANES)]
    scratch = [(MLSTM_HEADS, hd, hd), (MLSTM_HEADS, hd), (1, LANES)]
    in_arrays = (wide, wide, narrow, c0, n0, m0, prm['bias'], prm['ng'])
    return _seq_call(_mlstm_kernel, "mlstm_mixer", in_arrays, in_specs, 2, state_shapes, scratch, prev,
                     n_seq=n_seq, seq_len=seq_len, cs=cs, g=g, out_layer=out_layer)


def _mlstm_params(i_bias, f_bias, norm_g):
    return dict(bias=_lane_row(i_bias, LANE_ML_I) + _lane_row(f_bias, LANE_ML_F), ng=norm_g[None, :])


def _lru_kernel(x_ref, gate_ref, h0_ref, c0_ref, cw_ref, cb_ref, wa_ref, ba_ref, wx_ref, bx_ref, lam_ref,
                y_ref, h_out_ref, c_out_ref, h_scr, win_scr, *, cs, phase):
    if phase == "init":
        h_scr[...] = h0_ref[...]
        win_scr[CONV_PAD - CONV_PREV:CONV_PAD, :] = c0_ref[...]
        return
    if phase == "final":
        h_out_ref[...] = h_scr[...]
        c_out_ref[...] = win_scr[CONV_PAD - CONV_PREV:CONV_PAD, :]
        return

    xc = _conv_step(x_ref, win_scr, cw_ref, cb_ref, cs)
    r_parts, i_parts = [], []
    for g in range(LRU_BLOCKS):
        x_g = xc[:, g * LRU_BLOCK:(g + 1) * LRU_BLOCK].astype(BF16)
        r_parts.append(jnp.dot(x_g, wa_ref[g], preferred_element_type=F32))
        i_parts.append(jnp.dot(x_g, wx_ref[g], preferred_element_type=F32))
    r = jax.nn.sigmoid(jnp.concatenate(r_parts, axis=-1) + ba_ref[...])
    i = jax.nn.sigmoid(jnp.concatenate(i_parts, axis=-1) + bx_ref[...])
    log_a = -LRU_C * r * jax.nn.softplus(-lam_ref[...])
    a = jnp.exp(log_a)
    u = jnp.sqrt(jnp.tanh(-log_a) * (a * a + 1.0)) * (i * xc)
    row = lax.broadcasted_iota(jnp.int32, (cs, D_GROUP), 0)
    u = u + jnp.where(row == 0, a * h_scr[...], 0.0)
    shift = 1
    while shift < cs:
        a_sh = pltpu.roll(a, shift, 0)
        u_sh = pltpu.roll(u, shift, 0)
        live = row >= shift
        u = jnp.where(live, a * u_sh + u, u)
        a = jnp.where(live, a * a_sh, a)
        shift *= 2
    h_scr[...] = u[cs - 1:cs, :]
    y_ref[...] = (u * jax.nn.gelu(gate_ref[...])).astype(y_ref.dtype)
    _conv_advance(win_scr, cs)


def _lru_call(wide, h0, conv0, prm, prev, *, n_seq, seq_len, cs, g, layer, out_layer):
    rows, const, per_seq, state_in = _seq_specs(cs, layer, g)
    in_specs = [rows(D_GROUP, WIDE_OFF[SEG_LRU_X] // D_GROUP), rows(D_GROUP, WIDE_OFF[SEG_LRU_GATE] // D_GROUP),
                state_in((1, D_GROUP)), state_in((CONV_PREV, D_GROUP)),
                const((CONV_W, D_GROUP)), const((1, D_GROUP)),
                const((LRU_BLOCKS, LRU_BLOCK, LRU_BLOCK)), const((1, D_GROUP)),
                const((LRU_BLOCKS, LRU_BLOCK, LRU_BLOCK)), const((1, D_GROUP)), const((1, D_GROUP))]
    state_shapes = [(1, D_GROUP), (CONV_PREV, D_GROUP)]
    scratch = [(1, D_GROUP), (CONV_PAD + cs, D_GROUP)]
    in_arrays = (wide, wide, h0, conv0, prm['cw'], prm['cb'], prm['wa'], prm['ba'], prm['wx'], prm['bx'], prm['lam'])
    return _seq_call(_lru_kernel, "rglru_mixer", in_arrays, in_specs, 7, state_shapes, scratch, prev,
                     n_seq=n_seq, seq_len=seq_len, cs=cs, g=g, out_layer=out_layer)


def _lru_params(conv_w, conv_b, w_a, b_a, w_x, b_x, lam):
    return dict(cw=conv_w, cb=conv_b[None, :], wa=w_a.astype(BF16), ba=b_a.reshape(1, D_GROUP),
                wx=w_x.astype(BF16), bx=b_x.reshape(1, D_GROUP), lam=lam[None, :])


def _kernel_states(st):
    s_ssd, s_ssd_conv, s_gdn, s_gdn_conv, s_mc, s_mn, s_mm, s_lru, s_lru_conv = st
    lead = s_mm.shape[:2]
    m_rows = jnp.zeros(lead + (1, LANES), F32).at[:, :, 0, LANE_ML_I:LANE_ML_I + MLSTM_HEADS].set(s_mm)
    return (s_ssd, s_ssd_conv, s_gdn, s_gdn_conv, s_mc, s_mn, m_rows, s_lru[:, :, None, :], s_lru_conv)


def _mixer_group(h, w_wide, w_narrow, w_layer, kst, prm, prev, *, n_seq, seq_len, row0, g, layer):
    s_ssd, s_ssd_conv, s_gdn, s_gdn_conv, s_mc, s_mn, m_rows, s_lru, s_lru_conv = kst
    m = n_seq * seq_len
    wide = _matmul(h, w_wide, w_layer, MM_TN, "in_proj", row0, m).reshape(n_seq, seq_len, WIDE_COLS)
    narrow = _matmul(h, w_narrow, w_layer, NARROW_COLS, "in_proj_gates", row0, m).reshape(
        n_seq, seq_len, NARROW_COLS)
    cs = min(seq_len, CHUNK)
    kw = dict(n_seq=n_seq, seq_len=seq_len, cs=cs, g=g, layer=layer, out_layer=w_layer)
    p_ssd, p_gdn, p_ml, p_lru = (None,) * N_MIXERS if prev is None else prev
    y_a, *o_ssd = _ssd_call(wide, narrow, s_ssd, s_ssd_conv, prm['ssd'], p_ssd, **kw)
    y_b, *o_gdn = _gdn_call(wide, narrow, s_gdn, s_gdn_conv, prm['gdn'], p_gdn, **kw)
    y_c, *o_ml = _mlstm_call(wide, narrow, s_mc, s_mn, m_rows, prm['mlstm'], p_ml, **kw)
    y_d, *o_lru = _lru_call(wide, s_lru, s_lru_conv, prm['lru'], p_lru, **kw)
    return tuple(y.reshape(m, D_GROUP) for y in (y_a, y_b, y_c, y_d)), (o_ssd, o_gdn, o_ml, o_lru)


def _reference_states(stacked):
    (ssd, ssd_cx, ssd_cbc), (gdn, gdn_conv), (mc, mn, m_rows), (lru, lru_conv) = stacked
    return (ssd, jnp.concatenate([ssd_cx, ssd_cbc], axis=-1), gdn, gdn_conv, mc, mn,
            m_rows[:, :, 0, LANE_ML_I:LANE_ML_I + MLSTM_HEADS], lru[:, :, 0, :], lru_conv)


def _state_shapes(n):
    return ((n, SSD_HEADS, SSD_HEAD_DIM, SSD_STATE),
            (n, CONV_PREV, SSD_CONV_DIM),
            (n, GDN_HEADS, GDN_HEAD_DIM, GDN_HEAD_DIM),
            (n, CONV_PREV, GDN_CONV_DIM),
            (n, MLSTM_HEADS, MLSTM_HEAD_DIM, MLSTM_HEAD_DIM),
            (n, MLSTM_HEADS, MLSTM_HEAD_DIM),
            (n, MLSTM_HEADS),
            (n, D_GROUP),
            (n, CONV_PREV, D_GROUP))


def kernel(x_prompt, x_sample, state_ssd, state_ssd_conv, state_gdn, state_gdn_conv, state_mlstm_c, state_mlstm_n, state_mlstm_m, state_rglru, state_rglru_conv, c_prompt, c_sample, w_ada, ada_table, norm1_g, norm2_g, final_g, w_in, w_out, ssd_conv_w, ssd_conv_b, ssd_dt_bias, ssd_a_log, ssd_d, ssd_norm_g, gdn_conv_w, gdn_conv_b, gdn_dt_bias, gdn_a_log, gdn_norm_g, mlstm_i_bias, mlstm_f_bias, mlstm_norm_g, lru_conv_w, lru_conv_b, lru_w_a, lru_b_a, lru_w_x, lru_b_x, lru_lambda, moe_w_group, moe_b_group, moe_w_expert, moe_b_expert, moe_w_gate, moe_w_up, moe_w_down):
    d = D_MODEL

    w_in_wide, w_in_narrow = _w_in_relayout(w_in)
    w_out_b = w_out.astype(BF16)
    w_gate_b = moe_w_gate.astype(BF16)
    w_up_b = moe_w_up.astype(BF16)
    w_down_b = moe_w_down.astype(BF16)
    router_pad = jnp.zeros((DEPTH, d, ROUTER_LANES - N_EXPERT_GROUPS - N_EXPERTS), F32)
    w_router = jnp.concatenate([moe_w_group, moe_w_expert, router_pad], axis=-1)
    w_router_hi = w_router.astype(BF16)
    w_router_lo = (w_router - w_router_hi.astype(F32)).astype(BF16)
    w_router = jnp.concatenate([w_router_hi, w_router_lo], axis=-1)
    b_router = jnp.concatenate([moe_b_group, moe_b_expert, router_pad[:, 0, :]], axis=-1)[:, None, :]

    c_pad = jnp.concatenate([c_prompt, c_sample, jnp.zeros((ADA_ROWS - BATCH - DEC_BATCH, d), F32)], axis=0)
    mod_shared = _ada_matmul(c_pad, w_ada)
    mod_tiles = jnp.concatenate(
        [jnp.broadcast_to(mod_shared[:BATCH, None, :], (BATCH, MOD_ROWS, 6 * d)),
         mod_shared[BATCH:BATCH + DEC_BATCH].reshape(SAMPLE_ROW_TILES, MOD_ROWS, 6 * d)], axis=0)
    ada_rows = ada_table.reshape(DEPTH * 6, 1, d)

    x = jnp.concatenate([x_prompt.reshape(N_PROMPT_TOK, d), x_sample.reshape(N_SAMPLE_TOK, d)], axis=0)

    st_sample = (state_ssd, state_ssd_conv, state_gdn, state_gdn_conv, state_mlstm_c,
                 state_mlstm_n, state_mlstm_m, state_rglru, state_rglru_conv)
    kst_sample = _kernel_states(st_sample)
    kst_prompt = _kernel_states(tuple(jnp.zeros((1,) + s, F32) for s in _state_shapes(BATCH)))
    stacked_p = stacked_s = None

    y_prev = None
    for l in range(DEPTH):
        if l == 0:
            (h,) = _rowwise(x, norm1_g[l][None, :], mod_tiles=mod_tiles, ada_rows=ada_rows, layer=l,
                            mod_chunks=(1, 0), name="norm1")
        else:
            x, h = _rowwise(x, norm1_g[l][None, :], y=y_prev, mod_tiles=mod_tiles, ada_rows=ada_rows,
                            layer=l, gate_chunk=5, gate_layer=l - 1, mod_chunks=(1, 0), name="resid_norm1")
        prm = dict(ssd=_ssd_params(ssd_conv_w[l], ssd_conv_b[l], ssd_dt_bias[l], ssd_a_log[l], ssd_d[l],
                                   ssd_norm_g[l]),
                   gdn=_gdn_params(gdn_conv_w[l], gdn_conv_b[l], gdn_dt_bias[l], gdn_a_log[l], gdn_norm_g[l]),
                   mlstm=_mlstm_params(mlstm_i_bias[l], mlstm_f_bias[l], mlstm_norm_g[l]),
                   lru=_lru_params(lru_conv_w[l], lru_conv_b[l], lru_w_a[l], lru_b_a[l], lru_w_x[l], lru_b_x[l],
                                   lru_lambda[l]))
        ys_p, stacked_p = _mixer_group(h, w_in_wide, w_in_narrow, l, kst_prompt, prm, stacked_p, n_seq=BATCH,
                                       seq_len=SEQ, row0=0, g=PROMPT_SEQS_PER_STEP, layer=0)
        ys_s, stacked_s = _mixer_group(h, w_in_wide, w_in_narrow, l, kst_sample, prm, stacked_s, n_seq=DEC_BATCH,
                                       seq_len=DEC_SEQ, row0=N_PROMPT_TOK, g=SAMPLE_SEQS_PER_STEP, layer=l)
        y_mix = _out_proj(ys_p, ys_s, w_out_b, l)

        x, h2, gates = _rowwise(x, norm2_g[l][None, :], y=y_mix, mod_tiles=mod_tiles, ada_rows=ada_rows,
                                layer=l, gate_chunk=2, mod_chunks=(4, 3),
                                router=(w_router[l], b_router[l]), name="resid_norm2_router")
        src, valid, dest, tile_ea, tile_eb, n_used = _moe_plan(gates)
        x_sorted = _take_rows(h2, src)
        gates_sorted = _take_rows(gates, src) * valid[:, None]
        y_sorted = _moe(x_sorted, gates_sorted, tile_ea, tile_eb, n_used, w_gate_b, w_up_b, w_down_b, l)
        y_prev = _take_rows(y_sorted, dest)

    x, y_fin = _rowwise(x, final_g[None, :], y=y_prev, mod_tiles=mod_tiles, ada_rows=ada_rows,
                        layer=DEPTH - 1, gate_chunk=5, h_dtype=F32, name="resid_final_norm")
    outs = [y_fin[:N_PROMPT_TOK].reshape(BATCH, SEQ, d), y_fin[N_PROMPT_TOK:].reshape(DEC_BATCH, DEC_SEQ, d)]
    for st_p, st_s in zip(_reference_states(stacked_p), _reference_states(stacked_s)):
        outs.append(st_p)
        outs.append(st_s)
    return tuple(outs)
```

```python
import functools

import numpy as np
import jax
import jax.numpy as jnp
from jax import lax
from jax.experimental import pallas as pl
from jax.experimental.pallas import tpu as pltpu

F32 = jnp.float32
BF16 = jnp.bfloat16

D_MODEL = 4096
BATCH = 4
SEQ = 2048
DEPTH = 4
DEC_BATCH = 128
DEC_SEQ = 8
EPS = 1e-6
N_MIXERS = 4
D_GROUP = D_MODEL // N_MIXERS
CONV_W = 4
CHUNK = 64
SSD_HEAD_DIM = 64
SSD_HEADS = D_GROUP // SSD_HEAD_DIM
SSD_NGROUPS = 2
SSD_STATE = 128
SSD_BC_DIM = 2 * SSD_NGROUPS * SSD_STATE
SSD_CONV_DIM = D_GROUP + SSD_BC_DIM
GDN_HEAD_DIM = 128
GDN_HEADS = D_GROUP // GDN_HEAD_DIM
GDN_CONV_DIM = 3 * D_GROUP
MLSTM_HEAD_DIM = 128
MLSTM_HEADS = D_GROUP // MLSTM_HEAD_DIM
LRU_BLOCKS = 8
LRU_BLOCK = D_GROUP // LRU_BLOCKS
LRU_C = 8.0
N_EXPERT_GROUPS = 4
EXPERTS_PER_GROUP = 4
N_EXPERTS = N_EXPERT_GROUPS * EXPERTS_PER_GROUP
D_FF_EXPERT = D_MODEL // 8
IN_SIZES = (D_GROUP, SSD_CONV_DIM, SSD_HEADS,
            GDN_CONV_DIM, GDN_HEADS, GDN_HEADS, D_GROUP,
            3 * D_GROUP, MLSTM_HEADS, MLSTM_HEADS, D_GROUP,
            D_GROUP, D_GROUP)
IN_OFFSETS = tuple(int(o) for o in np.cumsum((0,) + IN_SIZES))
(SEG_SSD_Z, SEG_SSD_XBC, SEG_SSD_DT, SEG_GDN_QKV, SEG_GDN_A, SEG_GDN_B, SEG_GDN_Z,
 SEG_ML_QKV, SEG_ML_I, SEG_ML_F, SEG_ML_O, SEG_LRU_X, SEG_LRU_GATE) = range(len(IN_SIZES))

N_PROMPT_TOK = BATCH * SEQ
N_SAMPLE_TOK = DEC_BATCH * DEC_SEQ
N_TOK = N_PROMPT_TOK + N_SAMPLE_TOK

LANES = 128
SUBLANES = 8
VMEM_LIMIT_BYTES = 56 * 1024 * 1024

ROW_TILE = 256
MOD_REPEAT = DEC_SEQ
MOD_ROWS = ROW_TILE // MOD_REPEAT
N_ROW_TILES = N_TOK // ROW_TILE
PROMPT_TILES_PER_SEQ = SEQ // ROW_TILE
PROMPT_ROW_TILES = N_PROMPT_TOK // ROW_TILE
SAMPLE_ROW_TILES = N_SAMPLE_TOK // ROW_TILE
MM_TM = 1024
MM_TN = 512
ADA_ROWS = 144
ADA_TN = 512
MOE_TM = 512
MOE_FF_TILE = 256
MOE_FF_STEPS = D_FF_EXPERT // MOE_FF_TILE
TOP_K_IN_GROUP = 2
PAIRS = tuple((a, b) for a in range(EXPERTS_PER_GROUP) for b in range(a + 1, EXPERTS_PER_GROUP))
N_PAIRS = len(PAIRS)
N_BUCKETS = N_EXPERT_GROUPS * N_PAIRS
MOE_STEPS_PER_TILE = TOP_K_IN_GROUP * MOE_FF_STEPS
MOE_MAX_TILES = N_TOK // MOE_TM + N_BUCKETS
MOE_SLOTS = MOE_MAX_TILES * MOE_TM
ROUTER_LANES = LANES
EXPERT_LANE0 = N_EXPERT_GROUPS
BUCKET_LANE = 0

_WIDE_SEGS = (SEG_GDN_QKV, SEG_ML_QKV, SEG_SSD_Z, SEG_GDN_Z, SEG_ML_O, SEG_LRU_X, SEG_LRU_GATE, SEG_SSD_XBC)
_NARROW_SEGS = (SEG_SSD_DT, SEG_GDN_A, SEG_GDN_B, SEG_ML_I, SEG_ML_F)
WIDE_OFF = {}
_o = 0
for _s in _WIDE_SEGS:
    WIDE_OFF[_s] = _o
    _o += IN_SIZES[_s]
WIDE_COLS = _o
NARROW_OFF = {}
_o = 0
for _s in _NARROW_SEGS:
    NARROW_OFF[_s] = _o
    _o += IN_SIZES[_s]
NARROW_COLS = LANES
LANE_SSD_DT = NARROW_OFF[SEG_SSD_DT]
LANE_GDN_A = NARROW_OFF[SEG_GDN_A]
LANE_GDN_B = NARROW_OFF[SEG_GDN_B]
LANE_ML_I = NARROW_OFF[SEG_ML_I]
LANE_ML_F = NARROW_OFF[SEG_ML_F]

PROMPT_SEQS_PER_STEP = 2
SAMPLE_SEQS_PER_STEP = 8
CONV_PAD = SUBLANES
CONV_PREV = CONV_W - 1


def _cparams(*sem):
    return pltpu.CompilerParams(dimension_semantics=sem, vmem_limit_bytes=VMEM_LIMIT_BYTES)


def _dot(a, b):
    return jnp.dot(a.astype(BF16), b.astype(BF16), preferred_element_type=F32)


def _dot_nt(a, b):
    return lax.dot_general(a.astype(BF16), b.astype(BF16), (((1,), (1,)), ((), ())), preferred_element_type=F32)


def _dot_tn(a, b):
    return lax.dot_general(a.astype(BF16), b.astype(BF16), (((0,), (0,)), ((), ())), preferred_element_type=F32)


def _dot_f32(a, b):
    return jnp.dot(a, b, preferred_element_type=F32, precision=lax.Precision.HIGHEST)


def _silu(x):
    return x * jax.nn.sigmoid(x)


def _mm_kernel(a_ref, w_ref, o_ref):
    o_ref[...] = jnp.dot(a_ref[...], w_ref[...], preferred_element_type=F32).astype(o_ref.dtype)


def _matmul(a, w, layer, tn, name, row0, m):
    k = a.shape[1]
    n = w.shape[2]
    assert m % MM_TM == 0 and row0 % MM_TM == 0 and n % tn == 0
    tile0 = row0 // MM_TM
    return pl.pallas_call(
        _mm_kernel,
        grid=(m // MM_TM, n // tn),
        in_specs=[pl.BlockSpec((MM_TM, k), lambda i, j: (tile0 + i, 0)),
                  pl.BlockSpec((None, k, tn), lambda i, j: (layer, 0, j))],
        out_specs=pl.BlockSpec((MM_TM, tn), lambda i, j: (i, j)),
        out_shape=jax.ShapeDtypeStruct((m, n), F32),
        compiler_params=_cparams("parallel", "arbitrary"),
        name=name,
    )(a, w)


PROMPT_MM_TILES = N_PROMPT_TOK // MM_TM
assert N_SAMPLE_TOK == MM_TM


def _out_proj_kernel(*refs):
    yp_refs, ys_refs = refs[:N_MIXERS], refs[N_MIXERS:2 * N_MIXERS]
    w_refs, o_ref = refs[2 * N_MIXERS:3 * N_MIXERS], refs[3 * N_MIXERS]

    def project(y_refs):
        acc = jnp.dot(y_refs[0][...], w_refs[0][...], preferred_element_type=F32)
        for m in range(1, N_MIXERS):
            acc += jnp.dot(y_refs[m][...], w_refs[m][...], preferred_element_type=F32)
        o_ref[...] = acc

    is_prompt = pl.program_id(0) < PROMPT_MM_TILES
    pl.when(is_prompt)(lambda: project(yp_refs))
    pl.when(jnp.logical_not(is_prompt))(lambda: project(ys_refs))


def _out_proj(ys_prompt, ys_sample, w, layer):
    n = w.shape[2]
    yp_spec = pl.BlockSpec((MM_TM, D_GROUP), lambda i, j: (jnp.minimum(i, PROMPT_MM_TILES - 1), 0))
    ys_spec = pl.BlockSpec((MM_TM, D_GROUP), lambda i, j: (0, 0))
    w_specs = [pl.BlockSpec((None, D_GROUP, MM_TN), functools.partial(lambda i, j, m: (layer, m, j), m=m))
               for m in range(N_MIXERS)]
    return pl.pallas_call(
        _out_proj_kernel,
        grid=(N_TOK // MM_TM, n // MM_TN),
        in_specs=[yp_spec] * N_MIXERS + [ys_spec] * N_MIXERS + w_specs,
        out_specs=pl.BlockSpec((MM_TM, MM_TN), lambda i, j: (i, j)),
        out_shape=jax.ShapeDtypeStruct((N_TOK, n), F32),
        compiler_params=_cparams("parallel", "arbitrary"),
        name="out_proj",
    )(*ys_prompt, *ys_sample, w, w, w, w)


W_IN_ROW_TILE = 256


def _w_in_relayout_kernel(wt_ref, wide_ref, narrow_ref):
    for s in _WIDE_SEGS:
        wide_ref[:, WIDE_OFF[s]:WIDE_OFF[s] + IN_SIZES[s]] = (
            wt_ref[IN_OFFSETS[s]:IN_OFFSETS[s + 1], :].T.astype(BF16))
    rows = [wt_ref[IN_OFFSETS[s]:IN_OFFSETS[s + 1], :] for s in _NARROW_SEGS]
    used = sum(IN_SIZES[s] for s in _NARROW_SEGS)
    rows.append(jnp.zeros((NARROW_COLS - used, wt_ref.shape[1]), F32))
    narrow_ref[...] = jnp.concatenate(rows, axis=0).T.astype(BF16)


def _w_in_relayout(w_in):
    wt = jnp.swapaxes(w_in, 1, 2)
    depth, n, k = wt.shape
    return pl.pallas_call(
        _w_in_relayout_kernel,
        grid=(depth, k // W_IN_ROW_TILE),
        in_specs=[pl.BlockSpec((None, n, W_IN_ROW_TILE), lambda l, i: (l, 0, i))],
        out_specs=[pl.BlockSpec((None, W_IN_ROW_TILE, WIDE_COLS), lambda l, i: (l, i, 0)),
                   pl.BlockSpec((None, W_IN_ROW_TILE, NARROW_COLS), lambda l, i: (l, i, 0))],
        out_shape=[jax.ShapeDtypeStruct((depth, k, WIDE_COLS), BF16),
                   jax.ShapeDtypeStruct((depth, k, NARROW_COLS), BF16)],
        compiler_params=_cparams("parallel", "parallel"),
        name="w_in_relayout",
    )(wt)


def _ada_kernel(c_ref, w_ref, o_ref):
    o_ref[...] = _dot(_silu(c_ref[...]), w_ref[...])


def _ada_matmul(c_pad, w_ada):
    k, n = w_ada.shape
    return pl.pallas_call(
        _ada_kernel,
        grid=(n // ADA_TN,),
        in_specs=[pl.BlockSpec((ADA_ROWS, k), lambda j: (0, 0)),
                  pl.BlockSpec((k, ADA_TN), lambda j: (0, j))],
        out_specs=pl.BlockSpec((ADA_ROWS, ADA_TN), lambda j: (0, j)),
        out_shape=jax.ShapeDtypeStruct((ADA_ROWS, n), F32),
        compiler_params=_cparams("arbitrary"),
        name="adaln_matmul",
    )(c_pad, w_ada)


def _route(lg):
    lane = lax.broadcasted_iota(jnp.int32, lg.shape, 1).astype(F32)
    neg = -jnp.inf
    big = float(ROUTER_LANES)
    gl = jnp.where(lane < N_EXPERT_GROUPS, lg, neg)
    gmax = jnp.max(gl, axis=-1, keepdims=True)
    gidx = jnp.min(jnp.where(gl == gmax, lane, big), axis=-1, keepdims=True)
    g_w = 1.0 / jnp.sum(jnp.exp(gl - gmax), axis=-1, keepdims=True)
    lo = EXPERT_LANE0 + EXPERTS_PER_GROUP * gidx
    el = jnp.where((lane >= lo) & (lane < lo + EXPERTS_PER_GROUP), lg, neg)
    e1 = jnp.max(el, axis=-1, keepdims=True)
    i1 = jnp.min(jnp.where(el == e1, lane, big), axis=-1, keepdims=True)
    el2 = jnp.where(lane == i1, neg, el)
    e2 = jnp.max(el2, axis=-1, keepdims=True)
    i2 = jnp.min(jnp.where(el2 == e2, lane, big), axis=-1, keepdims=True)
    t = jnp.exp(e2 - e1)
    w1 = g_w / (1.0 + t)
    w2 = g_w * t / (1.0 + t)
    out = jnp.where(lane == i1, w1, 0.0) + jnp.where(lane == i2, w2, 0.0)
    pa = jnp.minimum(i1, i2) - lo
    pb = jnp.maximum(i1, i2) - lo
    pair = pa * (2 * EXPERTS_PER_GROUP - 1 - pa) * 0.5 + (pb - pa - 1.0)
    return out + jnp.where(lane == float(BUCKET_LANE), gidx * N_PAIRS + pair, 0.0)


def _rowwise_kernel(*refs, has_resid, has_mod, has_router):
    it = iter(refs)
    x_ref = next(it)
    if has_resid:
        y_ref, gate_m, gate_t = next(it), next(it), next(it)
    g_ref = next(it)
    if has_mod:
        sc_m, sc_t, sh_m, sh_t = next(it), next(it), next(it), next(it)
    if has_router:
        wr_ref, br_ref = next(it), next(it)
    if has_resid:
        xo_ref = next(it)
    h_ref = next(it)
    if has_router:
        gt_ref = next(it)

    def mod(m_ref, t_ref):
        m = m_ref[...] + t_ref[...]
        return jnp.broadcast_to(m[:, None, :], (MOD_ROWS, MOD_REPEAT, m.shape[-1])).reshape(ROW_TILE, m.shape[-1])

    x = x_ref[...]
    if has_resid:
        x = x + mod(gate_m, gate_t) * y_ref[...]
        xo_ref[...] = x
    y = x * lax.rsqrt(jnp.mean(x * x, axis=-1, keepdims=True) + EPS)
    h = y * g_ref[...]
    if has_mod:
        h = h * (1.0 + mod(sc_m, sc_t)) + mod(sh_m, sh_t)
    h_ref[...] = h.astype(h_ref.dtype)
    if has_router:
        h_hi = h.astype(BF16)
        h_lo = (h - h_hi.astype(F32)).astype(BF16)
        hi_terms = jnp.dot(h_hi, wr_ref[...], preferred_element_type=F32)
        lg = (hi_terms[:, :ROUTER_LANES] + hi_terms[:, ROUTER_LANES:]
              + jnp.dot(h_lo, wr_ref[:, :ROUTER_LANES], preferred_element_type=F32))
        gt_ref[...] = _route(lg + br_ref[...])


def _mod_tile_index(i):
    return jnp.where(i < PROMPT_ROW_TILES, i // PROMPT_TILES_PER_SEQ, BATCH + i - PROMPT_ROW_TILES)


def _rowwise(x, g_row, *, y=None, mod_tiles=None, ada_rows=None, layer=0, gate_chunk=None, gate_layer=None,
             mod_chunks=None, router=None, h_dtype=BF16, name="rowwise"):
    has_resid = y is not None
    has_mod = mod_chunks is not None
    has_router = router is not None
    d = D_MODEL
    row_spec = pl.BlockSpec((ROW_TILE, d), lambda i: (i, 0))
    vec_spec = pl.BlockSpec((1, d), lambda i: (0, 0))

    def mod_specs(chunk, lyr):
        return [pl.BlockSpec((None, MOD_ROWS, d), lambda i: (_mod_tile_index(i), 0, chunk)),
                pl.BlockSpec((None, 1, d), lambda i: (lyr * 6 + chunk, 0, 0))]

    args, specs = [x], [row_spec]
    if has_resid:
        args += [y, mod_tiles, ada_rows]
        specs += [row_spec] + mod_specs(gate_chunk, layer if gate_layer is None else gate_layer)
    args.append(g_row)
    specs.append(vec_spec)
    if has_mod:
        sc_chunk, sh_chunk = mod_chunks
        args += [mod_tiles, ada_rows, mod_tiles, ada_rows]
        specs += mod_specs(sc_chunk, layer) + mod_specs(sh_chunk, layer)
    if has_router:
        w_r, b_r = router
        args += [w_r, b_r]
        specs += [pl.BlockSpec((d, 2 * ROUTER_LANES), lambda i: (0, 0)),
                  pl.BlockSpec((1, ROUTER_LANES), lambda i: (0, 0))]
    out_shape, out_specs = [], []
    if has_resid:
        out_shape.append(jax.ShapeDtypeStruct((N_TOK, d), F32))
        out_specs.append(row_spec)
    out_shape.append(jax.ShapeDtypeStruct((N_TOK, d), h_dtype))
    out_specs.append(row_spec)
    if has_router:
        out_shape.append(jax.ShapeDtypeStruct((N_TOK, ROUTER_LANES), F32))
        out_specs.append(pl.BlockSpec((ROW_TILE, ROUTER_LANES), lambda i: (i, 0)))
    return pl.pallas_call(
        functools.partial(_rowwise_kernel, has_resid=has_resid, has_mod=has_mod, has_router=has_router),
        grid=(N_ROW_TILES,),
        in_specs=specs,
        out_specs=out_specs,
        out_shape=out_shape,
        compiler_params=_cparams("parallel"),
        name=name,
    )(*args)


def _moe_kernel(ea_ref, eb_ref, nu_ref, x_ref, gt_ref, wg_ref, wu_ref, wd_ref, o_ref):
    i = pl.program_id(0)
    j = pl.program_id(1)
    used = i < nu_ref[0]

    @pl.when(jnp.logical_and(j == 0, jnp.logical_not(used)))
    def _():
        o_ref[...] = jnp.zeros_like(o_ref)

    @pl.when(used)
    def _():
        x = x_ref[...]
        a = jnp.dot(x, wg_ref[...], preferred_element_type=F32)
        b = jnp.dot(x, wu_ref[...], preferred_element_type=F32)
        he = _silu(a) * b
        gates = gt_ref[...]
        lane = lax.broadcasted_iota(jnp.int32, gates.shape, 1)
        col = EXPERT_LANE0 + jnp.where(j < MOE_FF_STEPS, ea_ref[i], eb_ref[i])
        gcol = jnp.sum(jnp.where(lane == col, gates, 0.0), axis=-1, keepdims=True)
        contrib = gcol * jnp.dot(he.astype(BF16), wd_ref[...], preferred_element_type=F32)

        @pl.when(j == 0)
        def _():
            o_ref[...] = contrib

        @pl.when(j > 0)
        def _():
            o_ref[...] += contrib


def _moe(x_sorted, gates_sorted, tile_ea, tile_eb, n_used, w_gate, w_up, w_down, layer):
    d = D_MODEL

    def expert_of(i, j, ea, eb, nu):
        jj = jnp.where(i < nu[0], j, MOE_STEPS_PER_TILE - 1)
        return jnp.where(jj < MOE_FF_STEPS, ea[i], eb[i]), jj % MOE_FF_STEPS

    def in_row_map(i, j, ea, eb, nu):
        return (jnp.minimum(i, jnp.minimum(nu[0], MOE_MAX_TILES - 1)), 0)

    def wgu_map(i, j, ea, eb, nu):
        e, f = expert_of(i, j, ea, eb, nu)
        return (layer, e, 0, f)

    def wd_map(i, j, ea, eb, nu):
        e, f = expert_of(i, j, ea, eb, nu)
        return (layer, e, f, 0)

    grid_spec = pltpu.PrefetchScalarGridSpec(
        num_scalar_prefetch=3,
        grid=(MOE_MAX_TILES, MOE_STEPS_PER_TILE),
        in_specs=[pl.BlockSpec((MOE_TM, d), in_row_map),
                  pl.BlockSpec((MOE_TM, ROUTER_LANES), in_row_map),
                  pl.BlockSpec((None, None, d, MOE_FF_TILE), wgu_map),
                  pl.BlockSpec((None, None, d, MOE_FF_TILE), wgu_map),
                  pl.BlockSpec((None, None, MOE_FF_TILE, d), wd_map)],
        out_specs=pl.BlockSpec((MOE_TM, d), lambda i, j, ea, eb, nu: (i, 0)),
    )
    return pl.pallas_call(
        _moe_kernel,
        grid_spec=grid_spec,
        out_shape=jax.ShapeDtypeStruct((MOE_SLOTS, d), F32),
        compiler_params=_cparams("arbitrary", "arbitrary"),
        name="moe_experts",
    )(tile_ea, tile_eb, n_used, x_sorted, gates_sorted, w_gate, w_up, w_down)


def _take_rows(a, idx):
    return a.at[idx].get(mode="promise_in_bounds")


def _moe_plan(gates):
    bid = gates[:, BUCKET_LANE].astype(jnp.int32)
    onehot = (bid[:, None] == jnp.arange(N_BUCKETS, dtype=jnp.int32)[None, :]).astype(jnp.int32)
    rank = jnp.take_along_axis(jnp.cumsum(onehot, axis=0), bid[:, None], axis=1)[:, 0] - 1
    counts = jnp.sum(onehot, axis=0)
    tiles = (counts + MOE_TM - 1) // MOE_TM
    tile_end = jnp.cumsum(tiles)
    tile_start = tile_end - tiles
    dest = tile_start[bid] * MOE_TM + rank
    src = (jnp.arange(MOE_SLOTS, dtype=jnp.int32) % N_TOK).at[dest].set(jnp.arange(N_TOK, dtype=jnp.int32))
    n_used = tile_end[-1]
    t = jnp.arange(MOE_MAX_TILES, dtype=jnp.int32)
    tile_bucket = jnp.sum((jnp.minimum(t, n_used - 1)[:, None] >= tile_end[None, :]).astype(jnp.int32), axis=1)
    slot = jnp.arange(MOE_SLOTS, dtype=jnp.int32)
    slot_bucket = jnp.repeat(tile_bucket, MOE_TM)
    valid = ((slot - tile_start[slot_bucket] * MOE_TM < counts[slot_bucket])
             & (slot < n_used * MOE_TM)).astype(F32)
    group0 = (tile_bucket // N_PAIRS) * EXPERTS_PER_GROUP
    pair = tile_bucket % N_PAIRS
    tile_ea = group0 + jnp.asarray([p[0] for p in PAIRS], jnp.int32)[pair]
    tile_eb = group0 + jnp.asarray([p[1] for p in PAIRS], jnp.int32)[pair]
    return src, valid, dest, tile_ea, tile_eb, n_used.reshape(1).astype(jnp.int32)


def _conv_init(win_scr, c0_ref):
    win_scr[0:CONV_PAD, :] = jnp.zeros((CONV_PAD, win_scr.shape[1]), F32)
    win_scr[CONV_PAD - CONV_PREV:CONV_PAD, :] = c0_ref[...]


def _conv_step(u_ref, win_scr, w_ref, b_ref, cs):
    u = u_ref[...]
    prev = win_scr[0:CONV_PAD, :]
    row = lax.broadcasted_iota(jnp.int32, (CONV_PAD, 1), 0)
    out = b_ref[...]
    for j in range(CONV_W):
        shift = CONV_PREV - j
        if shift == 0:
            tap = u
        else:
            rolled = pltpu.roll(u, shift, 0)
            head = jnp.where(row >= shift, rolled[0:CONV_PAD, :], pltpu.roll(prev, shift, 0))
            tap = head if cs == CONV_PAD else jnp.concatenate([head, rolled[CONV_PAD:, :]], axis=0)
        out = out + tap * w_ref[j:j + 1, :]
    win_scr[CONV_PAD:2 * CONV_PAD, :] = u[cs - CONV_PAD:cs, :]
    return out


def _conv_advance(win_scr, cs):
    win_scr[0:CONV_PAD, :] = win_scr[CONV_PAD:2 * CONV_PAD, :]


def _causal_masks(cs):
    row = lax.broadcasted_iota(jnp.int32, (cs, cs), 0)
    col = lax.broadcasted_iota(jnp.int32, (cs, cs), 1)
    return row >= col, row > col


def _cumsum_rows(x):
    cs = x.shape[0]
    row = lax.broadcasted_iota(jnp.int32, x.shape, 0)
    shift = 1
    while shift < cs:
        x = x + jnp.where(row >= shift, pltpu.roll(x, shift, 0), 0.0)
        shift *= 2
    return x


def _rms(x, width):
    return x * lax.rsqrt(jnp.sum(x * x, axis=-1, keepdims=True) * (1.0 / width) + EPS)


def _seq_specs(cs, layer, g):
    def rows(width, col):
        return pl.BlockSpec((g, cs, width), lambda b, c: (b, c, col))

    def const(shape):
        nd = len(shape)
        return pl.BlockSpec(shape, lambda b, c: (0,) * nd)

    def per_seq(shape, last=0):
        nd = len(shape)
        return pl.BlockSpec((g,) + shape, lambda b, c: (b,) + (0,) * (nd - 1) + (last,))

    def state_in(shape, last=0):
        nd = len(shape)
        return pl.BlockSpec((None, g) + shape, lambda b, c: (layer, b) + (0,) * (nd - 1) + (last,))

    return rows, const, per_seq, state_in


def _seq_batched_kernel(*refs, body, per_seq_refs, alias_range, g, cs, nc):
    refs = refs[:alias_range[0]] + refs[alias_range[1]:]
    c = pl.program_id(1)
    views = [tuple(r.at[s] if flag else r for r, flag in zip(refs, per_seq_refs)) for s in range(g)]

    @pl.when(c == 0)
    def _():
        for v in views:
            body(*v, cs=cs, phase="init")

    for v in views:
        body(*v, cs=cs, phase="main")

    @pl.when(c == nc - 1)
    def _():
        for v in views:
            body(*v, cs=cs, phase="final")


def _seq_call(body, name, in_arrays, in_specs, n_const, state_shapes, scratch, prev, *,
              n_seq, seq_len, cs, g, out_layer):
    assert n_seq % g == 0
    nc = seq_len // cs
    n_out = 1 + len(state_shapes)
    flags = ([True] * (len(in_arrays) - n_const) + [False] * n_const + [True] * (n_out + len(scratch)))
    out_specs = [pl.BlockSpec((g, cs, D_GROUP), lambda b, c: (b, c, 0))]
    out_shape = [jax.ShapeDtypeStruct((n_seq, seq_len, D_GROUP), BF16)]
    for shape in state_shapes:
        nd = len(shape)
        out_specs.append(pl.BlockSpec((None, g) + shape, lambda b, c, nd=nd: (out_layer, b) + (0,) * nd))
        out_shape.append(jax.ShapeDtypeStruct((DEPTH, n_seq) + shape, F32))
    n_in = len(in_arrays)
    aliases = {}
    if prev is not None:
        in_arrays = tuple(in_arrays) + tuple(prev)
        in_specs = list(in_specs) + [pl.BlockSpec(memory_space=pl.ANY)] * len(prev)
        aliases = {n_in + k: 1 + k for k in range(len(prev))}
    return pl.pallas_call(
        functools.partial(_seq_batched_kernel, body=body, per_seq_refs=tuple(flags),
                          alias_range=(n_in, len(in_arrays)), g=g, cs=cs, nc=nc),
        grid=(n_seq // g, nc),
        in_specs=in_specs,
        out_specs=out_specs,
        out_shape=out_shape,
        scratch_shapes=[pltpu.VMEM((g,) + shape, F32) for shape in scratch],
        input_output_aliases=aliases,
        compiler_params=_cparams("parallel", "arbitrary"),
        name=name,
    )(*in_arrays)


def _ssd_kernel(z_ref, x_ref, bc_ref, nar_ref, s0_ref, c0x_ref, c0bc_ref,
                cwx_ref, cbx_ref, cwbc_ref, cbbc_ref, bias_ref, alog_ref, dsk_ref, ng_ref,
                y_ref, s_out_ref, cx_out_ref, cbc_out_ref,
                st_scr, wx_scr, wbc_scr, ycat_scr, *, cs, phase):
    lo = CONV_PAD - CONV_PREV
    if phase == "init":
        st_scr[...] = s0_ref[...]
        _conv_init(wx_scr, c0x_ref)
        _conv_init(wbc_scr, c0bc_ref)
        return
    if phase == "final":
        s_out_ref[...] = st_scr[...]
        cx_out_ref[...] = wx_scr[lo:CONV_PAD, :]
        cbc_out_ref[...] = wbc_scr[lo:CONV_PAD, :]
        return

    xs = _silu(_conv_step(x_ref, wx_scr, cwx_ref, cbx_ref, cs))
    bcs = _silu(_conv_step(bc_ref, wbc_scr, cwbc_ref, cbbc_ref, cs))
    incl, _ = _causal_masks(cs)
    dt = jax.nn.softplus(nar_ref[...] + bias_ref[...])
    cum = _cumsum_rows(dt * (-jnp.exp(alog_ref[...])))
    cum_t = cum.T
    exp_cum = jnp.exp(cum)
    cum_last = cum[cs - 1:cs, :]
    w_end = jnp.exp(cum_last - cum)
    chunk_decay = jnp.exp(cum_last)
    heads_per_group = SSD_HEADS // SSD_NGROUPS
    heads = range(SSD_HEADS)
    hp = SSD_HEAD_DIM
    s_old = [st_scr[h] for h in heads]
    col = lambda arr, h: arr[:, LANE_SSD_DT + h:LANE_SSD_DT + h + 1]
    b_gs = [bcs[:, g * SSD_STATE:(g + 1) * SSD_STATE] for g in range(SSD_NGROUPS)]
    c_gs = [bcs[:, (SSD_NGROUPS + g) * SSD_STATE:(SSD_NGROUPS + g + 1) * SSD_STATE] for g in range(SSD_NGROUPS)]
    cbs = [_dot_nt(c_gs[g], b_gs[g]) for g in range(SSD_NGROUPS)]
    grp = [h // heads_per_group for h in heads]
    scores = [cbs[grp[h]] * jnp.exp(jnp.where(incl, col(cum, h) - cum_t[LANE_SSD_DT + h:LANE_SSD_DT + h + 1, :],
                                              -jnp.inf)) for h in heads]
    xdts = [xs[:, h * hp:(h + 1) * hp] * col(dt, h) for h in heads]
    ys = [_dot(scores[h], xdts[h]) + _dot_nt(c_gs[grp[h]] * col(exp_cum, h), s_old[h]) for h in heads]
    s_new = [s_old[h] * col(chunk_decay, h) + _dot_tn(xdts[h] * col(w_end, h), b_gs[grp[h]]) for h in heads]
    for h in heads:
        st_scr[h] = s_new[h]
        ycat_scr[:, h * hp:(h + 1) * hp] = ys[h]
    y = (ycat_scr[...] + dsk_ref[...] * xs) * _silu(z_ref[...])
    gw = D_GROUP // SSD_NGROUPS
    for g in range(SSD_NGROUPS):
        y_g = _rms(y[:, g * gw:(g + 1) * gw], gw) * ng_ref[:, g * gw:(g + 1) * gw]
        y_ref[:, g * gw:(g + 1) * gw] = y_g.astype(y_ref.dtype)
    _conv_advance(wx_scr, cs)
    _conv_advance(wbc_scr, cs)


def _ssd_call(wide, narrow, s0, conv0, prm, prev, *, n_seq, seq_len, cs, g, layer, out_layer):
    rows, const, per_seq, state_in = _seq_specs(cs, layer, g)
    xw, bcw = D_GROUP, SSD_BC_DIM
    x_off = WIDE_OFF[SEG_SSD_XBC]
    in_specs = [rows(xw, WIDE_OFF[SEG_SSD_Z] // xw), rows(xw, x_off // xw), rows(bcw, (x_off + xw) // bcw),
                rows(NARROW_COLS, 0),
                state_in((SSD_HEADS, SSD_HEAD_DIM, SSD_STATE)),
                state_in((CONV_PREV, xw)), state_in((CONV_PREV, bcw), last=xw // bcw),
                const((CONV_W, xw)), const((1, xw)), const((CONV_W, bcw)), const((1, bcw)),
                const((1, LANES)), const((1, LANES)), const((1, xw)), const((1, xw))]
    state_shapes = [(SSD_HEADS, SSD_HEAD_DIM, SSD_STATE), (CONV_PREV, xw), (CONV_PREV, bcw)]
    scratch = [(SSD_HEADS, SSD_HEAD_DIM, SSD_STATE), (CONV_PAD + cs, xw), (CONV_PAD + cs, bcw), (cs, xw)]
    in_arrays = (wide, wide, wide, narrow, s0, conv0, conv0,
                 prm['cw'][:, :xw], prm['cb'][:, :xw], prm['cw'][:, xw:], prm['cb'][:, xw:],
                 prm['bias'], prm['alog'], prm['dskip'], prm['ng'])
    return _seq_call(_ssd_kernel, "ssd_mixer", in_arrays, in_specs, 8, state_shapes, scratch, prev,
                     n_seq=n_seq, seq_len=seq_len, cs=cs, g=g, out_layer=out_layer)


def _ssd_params(conv_w, conv_b, dt_bias, a_log, d_skip, norm_g):
    pad = jnp.zeros((LANES - SSD_HEADS,), F32)
    return dict(cw=conv_w, cb=conv_b[None, :],
                bias=jnp.concatenate([dt_bias, pad])[None, :],
                alog=jnp.concatenate([a_log, pad])[None, :],
                dskip=jnp.repeat(d_skip, SSD_HEAD_DIM)[None, :], ng=norm_g[None, :])


def _split_hi_lo(x):
    hi = x.astype(BF16).astype(F32)
    lo = (x - hi).astype(BF16).astype(F32)
    return hi, lo


def _lhs3(a):
    hi, lo = _split_hi_lo(a)
    return jnp.concatenate([hi, lo, hi], axis=1).astype(BF16)


def _rhs3(b):
    hi, lo = _split_hi_lo(b)
    return jnp.concatenate([hi, hi, lo], axis=0).astype(BF16)


def _dot3(lhs3, rhs3):
    return jnp.dot(lhs3, rhs3, preferred_element_type=F32)


def _inv_unit_lower(a_list, cs):
    row = lax.broadcasted_iota(jnp.int32, (cs, cs), 0)
    col = lax.broadcasted_iota(jnp.int32, (cs, cs), 1)
    eye = jnp.where(row == col, 1.0, 0.0)
    ps = [-a for a in a_list]
    ts = [eye + p for p in ps]
    forms = [(_lhs3(p), _rhs3(p)) for p in ps]
    n = 1
    while 2 * n < cs:
        ps = [_dot3(lhs, rhs) for lhs, rhs in forms]
        forms = [(_lhs3(p), _rhs3(p)) for p in ps]
        ts = [t + _dot3(_lhs3(t), rhs) for t, (_, rhs) in zip(ts, forms)]
        n *= 2
    return ts


def _gdn_kernel(qkv_ref, z_ref, nar_ref, s0_ref, c0_ref, cw_ref, cb_ref, bias_ref, alog_ref, ng_ref,
                y_ref, s_out_ref, c_out_ref, st_scr, win_scr, *, cs, phase):
    if phase == "init":
        st_scr[...] = s0_ref[...]
        _conv_init(win_scr, c0_ref)
        return
    if phase == "final":
        s_out_ref[...] = st_scr[...]
        c_out_ref[...] = win_scr[CONV_PAD - CONV_PREV:CONV_PAD, :]
        return

    qkv = _silu(_conv_step(qkv_ref, win_scr, cw_ref, cb_ref, cs))
    incl, strict = _causal_masks(cs)
    nar = nar_ref[...]
    g_log = -jnp.exp(alog_ref[...]) * jax.nn.softplus(nar + bias_ref[...])
    beta_all = jax.nn.sigmoid(nar)
    gc = _cumsum_rows(g_log)
    gc_t = gc.T
    exp_gc = jnp.exp(gc)
    gc_last = gc[cs - 1:cs, :]
    exp_to_end = jnp.exp(gc_last - gc)
    g_last = jnp.exp(gc_last)
    hd = GDN_HEAD_DIM
    heads = range(GDN_HEADS)
    s_old = [st_scr[h] for h in heads]
    col = lambda arr, lane: arr[:, lane:lane + 1]
    qs = [qkv[:, h * hd:(h + 1) * hd] for h in heads]
    ks = [qkv[:, D_GROUP + h * hd:D_GROUP + (h + 1) * hd] for h in heads]
    vs = [qkv[:, 2 * D_GROUP + h * hd:2 * D_GROUP + (h + 1) * hd] for h in heads]
    qs = [q * lax.rsqrt(jnp.sum(q * q, axis=-1, keepdims=True) + EPS) * (hd ** -0.5) for q in qs]
    ks = [k * lax.rsqrt(jnp.sum(k * k, axis=-1, keepdims=True) + EPS) for k in ks]
    decays = [jnp.exp(jnp.where(incl, col(gc, LANE_GDN_A + h) - gc_t[LANE_GDN_A + h:LANE_GDN_A + h + 1, :],
                                -jnp.inf)) for h in heads]
    betas = [col(beta_all, LANE_GDN_B + h) for h in heads]
    a_mats = [jnp.where(strict, betas[h] * _dot_nt(ks[h], ks[h]) * decays[h], 0.0) for h in heads]
    t_invs = _inv_unit_lower(a_mats, cs)
    rhs = [jnp.concatenate([vs[h] * betas[h], ks[h] * (betas[h] * col(exp_gc, LANE_GDN_A + h))], axis=1)
           for h in heads]
    uw = [_dot3(_lhs3(t_invs[h]), _rhs3(rhs[h])) for h in heads]
    qks = [_dot_nt(qs[h], ks[h]) * decays[h] for h in heads]
    v_new = [uw[h][:, :hd] - _dot(uw[h][:, hd:], s_old[h]) for h in heads]
    outs = [_dot(qs[h] * col(exp_gc, LANE_GDN_A + h), s_old[h]) + _dot(qks[h], v_new[h]) for h in heads]
    s_new = [s_old[h] * col(g_last, LANE_GDN_A + h) + _dot_tn(ks[h] * col(exp_to_end, LANE_GDN_A + h), v_new[h])
             for h in heads]
    for h in heads:
        st_scr[h] = s_new[h]
    y = jnp.concatenate([_rms(o, hd) for o in outs], axis=1) * ng_ref[...] * _silu(z_ref[...])
    y_ref[...] = y.astype(y_ref.dtype)
    _conv_advance(win_scr, cs)


def _gdn_call(wide, narrow, s0, conv0, prm, prev, *, n_seq, seq_len, cs, g, layer, out_layer):
    rows, const, per_seq, state_in = _seq_specs(cs, layer, g)
    hd = GDN_HEAD_DIM
    in_specs = [rows(GDN_CONV_DIM, WIDE_OFF[SEG_GDN_QKV] // GDN_CONV_DIM),
                rows(D_GROUP, WIDE_OFF[SEG_GDN_Z] // D_GROUP),
                rows(NARROW_COLS, 0),
                state_in((GDN_HEADS, hd, hd)), state_in((CONV_PREV, GDN_CONV_DIM)),
                const((CONV_W, GDN_CONV_DIM)), const((1, GDN_CONV_DIM)),
                const((1, LANES)), const((1, LANES)), const((1, D_GROUP))]
    state_shapes = [(GDN_HEADS, hd, hd), (CONV_PREV, GDN_CONV_DIM)]
    scratch = [(GDN_HEADS, hd, hd), (CONV_PAD + cs, GDN_CONV_DIM)]
    in_arrays = (wide, wide, narrow, s0, conv0, prm['cw'], prm['cb'], prm['bias'], prm['alog'], prm['ng'])
    return _seq_call(_gdn_kernel, "gdn_mixer", in_arrays, in_specs, 5, state_shapes, scratch, prev,
                     n_seq=n_seq, seq_len=seq_len, cs=cs, g=g, out_layer=out_layer)


def _lane_row(vec, lane0):
    return jnp.zeros((LANES,), F32).at[lane0:lane0 + vec.shape[0]].set(vec)[None, :]


def _gdn_params(conv_w, conv_b, dt_bias, a_log, norm_g):
    return dict(cw=conv_w, cb=conv_b[None, :], bias=_lane_row(dt_bias, LANE_GDN_A),
                alog=_lane_row(a_log, LANE_GDN_A), ng=norm_g[None, :])


def _mlstm_kernel(qkv_ref, o_ref, nar_ref, c0_ref, n0_ref, m0_ref, bias_ref, ng_ref,
                  y_ref, c_out_ref, n_out_ref, m_out_ref, c_scr, n_scr, m_scr, *, cs, phase):
    if phase == "init":
        c_scr[...] = c0_ref[...]
        n_scr[...] = n0_ref[...]
        m_scr[...] = m0_ref[...]
        return
    if phase == "final":
        c_out_ref[...] = c_scr[...]
        n_out_ref[...] = n_scr[...]
        m_out_ref[...] = m_scr[...]
        return

    incl, _ = _causal_masks(cs)
    pre = nar_ref[...] + bias_ref[...]
    f_cum = _cumsum_rows(jax.nn.log_sigmoid(pre))
    f_cum_t = f_cum.T
    pre_t = pre.T
    hd = MLSTM_HEAD_DIM
    heads = range(MLSTM_HEADS)
    c_old = [c_scr[h] for h in heads]
    n_all = n_scr[...]
    m_all = m_scr[...]
    qkv = qkv_ref[...]
    col = lambda arr, lane: arr[:, lane:lane + 1]
    qs = [qkv[:, h * hd:(h + 1) * hd] for h in heads]
    ks = [qkv[:, D_GROUP + h * hd:D_GROUP + (h + 1) * hd] * (hd ** -0.5) for h in heads]
    vs = [qkv[:, 2 * D_GROUP + h * hd:2 * D_GROUP + (h + 1) * hd] for h in heads]
    f_cols = [col(f_cum, LANE_ML_F + h) for h in heads]
    d_logs = [jnp.where(incl, f_cols[h] - f_cum_t[LANE_ML_F + h:LANE_ML_F + h + 1, :]
                        + pre_t[LANE_ML_I + h:LANE_ML_I + h + 1, :], -jnp.inf) for h in heads]
    m_prev = [col(m_all, LANE_ML_I + h) for h in heads]
    m_t = [jnp.maximum(f_cols[h] + m_prev[h], jnp.max(d_logs[h], axis=-1, keepdims=True)) for h in heads]
    w_carry = [jnp.exp(f_cols[h] + m_prev[h] - m_t[h]) for h in heads]
    ps = [jnp.exp(d_logs[h] - m_t[h]) * _dot_nt(qs[h], ks[h]) for h in heads]
    num = [w_carry[h] * _dot(qs[h], c_old[h]) + _dot(ps[h], vs[h]) for h in heads]
    den = [w_carry[h] * jnp.sum(qs[h] * n_all[h:h + 1, :], axis=-1, keepdims=True)
           + jnp.sum(ps[h], axis=-1, keepdims=True) for h in heads]
    hs = [num[h] / jnp.maximum(jnp.abs(den[h]), jnp.exp(-m_t[h])) for h in heads]
    m_end = [m_t[h][cs - 1:cs, :] for h in heads]
    f_last = [f_cols[h][cs - 1:cs, :] for h in heads]
    w_prev = [jnp.exp(f_last[h] + m_prev[h] - m_end[h]) for h in heads]
    kws = [ks[h] * jnp.exp(f_last[h] - f_cols[h] + col(pre, LANE_ML_I + h) - m_end[h]) for h in heads]
    c_new = [w_prev[h] * c_old[h] + _dot_tn(kws[h], vs[h]) for h in heads]
    n_new = [w_prev[h] * n_all[h:h + 1, :] + jnp.sum(kws[h], axis=0, keepdims=True) for h in heads]
    lane = lax.broadcasted_iota(jnp.int32, m_all.shape, 1)
    m_new = m_all
    for h in heads:
        c_scr[h] = c_new[h]
        m_new = jnp.where(lane == LANE_ML_I + h, m_end[h], m_new)
    n_scr[...] = jnp.concatenate(n_new, axis=0)
    m_scr[...] = m_new
    y = jnp.concatenate([_rms(hh, hd) for hh in hs], axis=1) * ng_ref[...] * jax.nn.sigmoid(o_ref[...])
    y_ref[...] = y.astype(y_ref.dtype)


def _mlstm_call(wide, narrow, c0, n0, m0, prm, prev, *, n_seq, seq_len, cs, g, layer, out_layer):
    rows, const, per_seq, state_in = _seq_specs(cs, layer, g)
    hd = MLSTM_HEAD_DIM
    qkv_w = 3 * D_GROUP
    in_specs = [rows(qkv_w, WIDE_OFF[SEG_ML_QKV] // qkv_w),
                rows(D_GROUP, WIDE_OFF[SEG_ML_O] // D_GROUP),
                rows(NARROW_COLS, 0),
                state_in((MLSTM_HEADS, hd, hd)), state_in((MLSTM_HEADS, hd)), state_in((1, LANES)),
                const((1, LANES)), const((1, D_GROUP))]
    state_shapes = [(MLSTM_HEADS, hd, hd), (MLSTM_HEADS, hd), (1, LANES)]
    scratch = [(MLSTM_HEADS, hd, hd), (MLSTM_HEADS, hd), (1, LANES)]
    in_arrays = (wide, wide, narrow, c0, n0, m0, prm['bias'], prm['ng'])
    return _seq_call(_mlstm_kernel, "mlstm_mixer", in_arrays, in_specs, 2, state_shapes, scratch, prev,
                     n_seq=n_seq, seq_len=seq_len, cs=cs, g=g, out_layer=out_layer)


def _mlstm_params(i_bias, f_bias, norm_g):
    return dict(bias=_lane_row(i_bias, LANE_ML_I) + _lane_row(f_bias, LANE_ML_F), ng=norm_g[None, :])


def _lru_kernel(x_ref, gate_ref, h0_ref, c0_ref, cw_ref, cb_ref, wa_ref, ba_ref, wx_ref, bx_ref, lam_ref,
                y_ref, h_out_ref, c_out_ref, h_scr, win_scr, *, cs, phase):
    if phase == "init":
        h_scr[...] = h0_ref[...]
        _conv_init(win_scr, c0_ref)
        return
    if phase == "final":
        h_out_ref[...] = h_scr[...]
        c_out_ref[...] = win_scr[CONV_PAD - CONV_PREV:CONV_PAD, :]
        return

    xc = _conv_step(x_ref, win_scr, cw_ref, cb_ref, cs)
    r_parts, i_parts = [], []
    for g in range(LRU_BLOCKS):
        x_g = xc[:, g * LRU_BLOCK:(g + 1) * LRU_BLOCK].astype(BF16)
        r_parts.append(jnp.dot(x_g, wa_ref[g], preferred_element_type=F32))
        i_parts.append(jnp.dot(x_g, wx_ref[g], preferred_element_type=F32))
    r = jax.nn.sigmoid(jnp.concatenate(r_parts, axis=-1) + ba_ref[...])
    i = jax.nn.sigmoid(jnp.concatenate(i_parts, axis=-1) + bx_ref[...])
    log_a = -LRU_C * r * jax.nn.softplus(-lam_ref[...])
    a = jnp.exp(log_a)
    u = jnp.sqrt(jnp.tanh(-log_a) * (a * a + 1.0)) * (i * xc)
    row = lax.broadcasted_iota(jnp.int32, (cs, D_GROUP), 0)
    u = u + jnp.where(row == 0, a * h_scr[...], 0.0)
    shift = 1
    while shift < cs:
        a_sh = pltpu.roll(a, shift, 0)
        u_sh = pltpu.roll(u, shift, 0)
        live = row >= shift
        u = jnp.where(live, a * u_sh + u, u)
        a = jnp.where(live, a * a_sh, a)
        shift *= 2
    h_scr[...] = u[cs - 1:cs, :]
    y_ref[...] = (u * jax.nn.gelu(gate_ref[...])).astype(y_ref.dtype)
    _conv_advance(win_scr, cs)


def _lru_call(wide, h0, conv0, prm, prev, *, n_seq, seq_len, cs, g, layer, out_layer):
    rows, const, per_seq, state_in = _seq_specs(cs, layer, g)
    in_specs = [rows(D_GROUP, WIDE_OFF[SEG_LRU_X] // D_GROUP), rows(D_GROUP, WIDE_OFF[SEG_LRU_GATE] // D_GROUP),
                state_in((1, D_GROUP)), state_in((CONV_PREV, D_GROUP)),
                const((CONV_W, D_GROUP)), const((1, D_GROUP)),
                const((LRU_BLOCKS, LRU_BLOCK, LRU_BLOCK)), const((1, D_GROUP)),
                const((LRU_BLOCKS, LRU_BLOCK, LRU_BLOCK)), const((1, D_GROUP)), const((1, D_GROUP))]
    state_shapes = [(1, D_GROUP), (CONV_PREV, D_GROUP)]
    scratch = [(1, D_GROUP), (CONV_PAD + cs, D_GROUP)]
    in_arrays = (wide, wide, h0, conv0, prm['cw'], prm['cb'], prm['wa'], prm['ba'], prm['wx'], prm['bx'], prm['lam'])
    return _seq_call(_lru_kernel, "rglru_mixer", in_arrays, in_specs, 7, state_shapes, scratch, prev,
                     n_seq=n_seq, seq_len=seq_len, cs=cs, g=g, out_layer=out_layer)


def _lru_params(conv_w, conv_b, w_a, b_a, w_x, b_x, lam):
    return dict(cw=conv_w, cb=conv_b[None, :], wa=w_a.astype(BF16), ba=b_a.reshape(1, D_GROUP),
                wx=w_x.astype(BF16), bx=b_x.reshape(1, D_GROUP), lam=lam[None, :])


def _kernel_states(st):
    s_ssd, s_ssd_conv, s_gdn, s_gdn_conv, s_mc, s_mn, s_mm, s_lru, s_lru_conv = st
    lead = s_mm.shape[:2]
    m_rows = jnp.zeros(lead + (1, LANES), F32).at[:, :, 0, LANE_ML_I:LANE_ML_I + MLSTM_HEADS].set(s_mm)
    return (s_ssd, s_ssd_conv, s_gdn, s_gdn_conv, s_mc, s_mn, m_rows, s_lru[:, :, None, :], s_lru_conv)


def _mixer_group(h, w_wide, w_narrow, w_layer, kst, prm, prev, *, n_seq, seq_len, row0, g, layer):
    s_ssd, s_ssd_conv, s_gdn, s_gdn_conv, s_mc, s_mn, m_rows, s_lru, s_lru_conv = kst
    m = n_seq * seq_len
    wide = _matmul(h, w_wide, w_layer, MM_TN, "in_proj", row0, m).reshape(n_seq, seq_len, WIDE_COLS)
    narrow = _matmul(h, w_narrow, w_layer, NARROW_COLS, "in_proj_gates", row0, m).reshape(
        n_seq, seq_len, NARROW_COLS)
    cs = min(seq_len, CHUNK)
    kw = dict(n_seq=n_seq, seq_len=seq_len, cs=cs, g=g, layer=layer, out_layer=w_layer)
    p_ssd, p_gdn, p_ml, p_lru = (None,) * N_MIXERS if prev is None else prev
    y_a, *o_ssd = _ssd_call(wide, narrow, s_ssd, s_ssd_conv, prm['ssd'], p_ssd, **kw)
    y_b, *o_gdn = _gdn_call(wide, narrow, s_gdn, s_gdn_conv, prm['gdn'], p_gdn, **kw)
    y_c, *o_ml = _mlstm_call(wide, narrow, s_mc, s_mn, m_rows, prm['mlstm'], p_ml, **kw)
    y_d, *o_lru = _lru_call(wide, s_lru, s_lru_conv, prm['lru'], p_lru, **kw)
    return tuple(y.reshape(m, D_GROUP) for y in (y_a, y_b, y_c, y_d)), (o_ssd, o_gdn, o_ml, o_lru)


def _reference_states(stacked):
    (ssd, ssd_cx, ssd_cbc), (gdn, gdn_conv), (mc, mn, m_rows), (lru, lru_conv) = stacked
    return (ssd, jnp.concatenate([ssd_cx, ssd_cbc], axis=-1), gdn, gdn_conv, mc, mn,
            m_rows[:, :, 0, LANE_ML_I:LANE_ML_I + MLSTM_HEADS], lru[:, :, 0, :], lru_conv)


def _state_shapes(n):
    return ((n, SSD_HEADS, SSD_HEAD_DIM, SSD_STATE),
            (n, CONV_PREV, SSD_CONV_DIM),
            (n, GDN_HEADS, GDN_HEAD_DIM, GDN_HEAD_DIM),
            (n, CONV_PREV, GDN_CONV_DIM),
            (n, MLSTM_HEADS, MLSTM_HEAD_DIM, MLSTM_HEAD_DIM),
            (n, MLSTM_HEADS, MLSTM_HEAD_DIM),
            (n, MLSTM_HEADS),
            (n, D_GROUP),
            (n, CONV_PREV, D_GROUP))


def kernel(x_prompt, x_sample, state_ssd, state_ssd_conv, state_gdn, state_gdn_conv, state_mlstm_c, state_mlstm_n, state_mlstm_m, state_rglru, state_rglru_conv, c_prompt, c_sample, w_ada, ada_table, norm1_g, norm2_g, final_g, w_in, w_out, ssd_conv_w, ssd_conv_b, ssd_dt_bias, ssd_a_log, ssd_d, ssd_norm_g, gdn_conv_w, gdn_conv_b, gdn_dt_bias, gdn_a_log, gdn_norm_g, mlstm_i_bias, mlstm_f_bias, mlstm_norm_g, lru_conv_w, lru_conv_b, lru_w_a, lru_b_a, lru_w_x, lru_b_x, lru_lambda, moe_w_group, moe_b_group, moe_w_expert, moe_b_expert, moe_w_gate, moe_w_up, moe_w_down):
    d = D_MODEL

    w_in_wide, w_in_narrow = _w_in_relayout(w_in)
    w_out_b = w_out.astype(BF16)
    w_gate_b = moe_w_gate.astype(BF16)
    w_up_b = moe_w_up.astype(BF16)
    w_down_b = moe_w_down.astype(BF16)
    router_pad = jnp.zeros((DEPTH, d, ROUTER_LANES - N_EXPERT_GROUPS - N_EXPERTS), F32)
    w_router = jnp.concatenate([moe_w_group, moe_w_expert, router_pad], axis=-1)
    w_router_hi = w_router.astype(BF16)
    w_router_lo = (w_router - w_router_hi.astype(F32)).astype(BF16)
    w_router = jnp.concatenate([w_router_hi, w_router_lo], axis=-1)
    b_router = jnp.concatenate([moe_b_group, moe_b_expert, router_pad[:, 0, :]], axis=-1)[:, None, :]

    c_pad = jnp.concatenate([c_prompt, c_sample, jnp.zeros((ADA_ROWS - BATCH - DEC_BATCH, d), F32)], axis=0)
    mod_shared = _ada_matmul(c_pad, w_ada)
    mod_tiles = jnp.concatenate(
        [jnp.broadcast_to(mod_shared[:BATCH, None, :], (BATCH, MOD_ROWS, 6 * d)),
         mod_shared[BATCH:BATCH + DEC_BATCH].reshape(SAMPLE_ROW_TILES, MOD_ROWS, 6 * d)], axis=0)
    ada_rows = ada_table.reshape(DEPTH * 6, 1, d)

    x = jnp.concatenate([x_prompt.reshape(N_PROMPT_TOK, d), x_sample.reshape(N_SAMPLE_TOK, d)], axis=0)

    st_sample = (state_ssd, state_ssd_conv, state_gdn, state_gdn_conv, state_mlstm_c,
                 state_mlstm_n, state_mlstm_m, state_rglru, state_rglru_conv)
    kst_sample = _kernel_states(st_sample)
    kst_prompt = _kernel_states(tuple(jnp.zeros((1,) + s, F32) for s in _state_shapes(BATCH)))
    stacked_p = stacked_s = None

    y_prev = None
    for l in range(DEPTH):
        if l == 0:
            (h,) = _rowwise(x, norm1_g[l][None, :], mod_tiles=mod_tiles, ada_rows=ada_rows, layer=l,
                            mod_chunks=(1, 0), name="norm1")
        else:
            x, h = _rowwise(x, norm1_g[l][None, :], y=y_prev, mod_tiles=mod_tiles, ada_rows=ada_rows,
                            layer=l, gate_chunk=5, gate_layer=l - 1, mod_chunks=(1, 0), name="resid_norm1")
        prm = dict(ssd=_ssd_params(ssd_conv_w[l], ssd_conv_b[l], ssd_dt_bias[l], ssd_a_log[l], ssd_d[l],
                                   ssd_norm_g[l]),
                   gdn=_gdn_params(gdn_conv_w[l], gdn_conv_b[l], gdn_dt_bias[l], gdn_a_log[l], gdn_norm_g[l]),
                   mlstm=_mlstm_params(mlstm_i_bias[l], mlstm_f_bias[l], mlstm_norm_g[l]),
                   lru=_lru_params(lru_conv_w[l], lru_conv_b[l], lru_w_a[l], lru_b_a[l], lru_w_x[l], lru_b_x[l],
                                   lru_lambda[l]))
        ys_p, stacked_p = _mixer_group(h, w_in_wide, w_in_narrow, l, kst_prompt, prm, stacked_p, n_seq=BATCH,
                                       seq_len=SEQ, row0=0, g=PROMPT_SEQS_PER_STEP, layer=0)
        ys_s, stacked_s = _mixer_group(h, w_in_wide, w_in_narrow, l, kst_sample, prm, stacked_s, n_seq=DEC_BATCH,
                                       seq_len=DEC_SEQ, row0=N_PROMPT_TOK, g=SAMPLE_SEQS_PER_STEP, layer=l)
        y_mix = _out_proj(ys_p, ys_s, w_out_b, l)

        x, h2, gates = _rowwise(x, norm2_g[l][None, :], y=y_mix, mod_tiles=mod_tiles, ada_rows=ada_rows,
                                layer=l, gate_chunk=2, mod_chunks=(4, 3),
                                router=(w_router[l], b_router[l]), name="resid_norm2_router")
        src, valid, dest, tile_ea, tile_eb, n_used = _moe_plan(gates)
        x_sorted = _take_rows(h2, src)
        gates_sorted = _take_rows(gates, src) * valid[:, None]
        y_sorted = _moe(x_sorted, gates_sorted, tile_ea, tile_eb, n_used, w_gate_b, w_up_b, w_down_b, l)
        y_prev = _take_rows(y_sorted, dest)

    x, y_fin = _rowwise(x, final_g[None, :], y=y_prev, mod_tiles=mod_tiles, ada_rows=ada_rows,
                        layer=DEPTH - 1, gate_chunk=5, h_dtype=F32, name="resid_final_norm")
    outs = [y_fin[:N_PROMPT_TOK].reshape(BATCH, SEQ, d), y_fin[N_PROMPT_TOK:].reshape(DEC_BATCH, DEC_SEQ, d)]
    for st_p, st_s in zip(_reference_states(stacked_p), _reference_states(stacked_s)):
        outs.append(st_p)
        outs.append(st_s)
    return tuple(outs)
```

```python
import functools

import numpy as np
import jax
import jax.numpy as jnp
from jax import lax
from jax.experimental import pallas as pl
from jax.experimental.pallas import tpu as pltpu

F32 = jnp.float32
BF16 = jnp.bfloat16

D_MODEL = 4096
BATCH = 4
SEQ = 2048
DEPTH = 4
DEC_BATCH = 128
DEC_SEQ = 8
EPS = 1e-6
N_MIXERS = 4
D_GROUP = D_MODEL // N_MIXERS
CONV_W = 4
CHUNK = 64
SSD_HEAD_DIM = 64
SSD_HEADS = D_GROUP // SSD_HEAD_DIM
SSD_NGROUPS = 2
SSD_STATE = 128
SSD_BC_DIM = 2 * SSD_NGROUPS * SSD_STATE
SSD_CONV_DIM = D_GROUP + SSD_BC_DIM
GDN_HEAD_DIM = 128
GDN_HEADS = D_GROUP // GDN_HEAD_DIM
GDN_CONV_DIM = 3 * D_GROUP
MLSTM_HEAD_DIM = 128
MLSTM_HEADS = D_GROUP // MLSTM_HEAD_DIM
LRU_BLOCKS = 8
LRU_BLOCK = D_GROUP // LRU_BLOCKS
LRU_C = 8.0
N_EXPERT_GROUPS = 4
EXPERTS_PER_GROUP = 4
N_EXPERTS = N_EXPERT_GROUPS * EXPERTS_PER_GROUP
D_FF_EXPERT = D_MODEL // 8
IN_SIZES = (D_GROUP, SSD_CONV_DIM, SSD_HEADS,
            GDN_CONV_DIM, GDN_HEADS, GDN_HEADS, D_GROUP,
            3 * D_GROUP, MLSTM_HEADS, MLSTM_HEADS, D_GROUP,
            D_GROUP, D_GROUP)
IN_OFFSETS = tuple(int(o) for o in np.cumsum((0,) + IN_SIZES))
(SEG_SSD_Z, SEG_SSD_XBC, SEG_SSD_DT, SEG_GDN_QKV, SEG_GDN_A, SEG_GDN_B, SEG_GDN_Z,
 SEG_ML_QKV, SEG_ML_I, SEG_ML_F, SEG_ML_O, SEG_LRU_X, SEG_LRU_GATE) = range(len(IN_SIZES))

N_PROMPT_TOK = BATCH * SEQ
N_SAMPLE_TOK = DEC_BATCH * DEC_SEQ
N_TOK = N_PROMPT_TOK + N_SAMPLE_TOK

LANES = 128
SUBLANES = 8
VMEM_LIMIT_BYTES = 56 * 1024 * 1024

ROW_TILE = 256
MOD_REPEAT = DEC_SEQ
MOD_ROWS = ROW_TILE // MOD_REPEAT
N_ROW_TILES = N_TOK // ROW_TILE
PROMPT_TILES_PER_SEQ = SEQ // ROW_TILE
PROMPT_ROW_TILES = N_PROMPT_TOK // ROW_TILE
SAMPLE_ROW_TILES = N_SAMPLE_TOK // ROW_TILE
MM_TM = 1024
MM_TN = 512
ADA_ROWS = 144
ADA_TN = 512
MOE_TM = 512
MOE_FF_TILE = 256
MOE_FF_STEPS = D_FF_EXPERT // MOE_FF_TILE
TOP_K_IN_GROUP = 2
PAIRS = tuple((a, b) for a in range(EXPERTS_PER_GROUP) for b in range(a + 1, EXPERTS_PER_GROUP))
N_PAIRS = len(PAIRS)
N_BUCKETS = N_EXPERT_GROUPS * N_PAIRS
MOE_STEPS_PER_TILE = TOP_K_IN_GROUP * MOE_FF_STEPS
MOE_MAX_TILES = N_TOK // MOE_TM + N_BUCKETS
MOE_SLOTS = MOE_MAX_TILES * MOE_TM
ROUTER_LANES = LANES
EXPERT_LANE0 = N_EXPERT_GROUPS
BUCKET_LANE = 0

_WIDE_SEGS = (SEG_GDN_QKV, SEG_ML_QKV, SEG_SSD_Z, SEG_GDN_Z, SEG_ML_O, SEG_LRU_X, SEG_LRU_GATE, SEG_SSD_XBC)
_NARROW_SEGS = (SEG_SSD_DT, SEG_GDN_A, SEG_GDN_B, SEG_ML_I, SEG_ML_F)
WIDE_OFF = {}
_o = 0
for _s in _WIDE_SEGS:
    WIDE_OFF[_s] = _o
    _o += IN_SIZES[_s]
WIDE_COLS = _o
NARROW_OFF = {}
_o = 0
for _s in _NARROW_SEGS:
    NARROW_OFF[_s] = _o
    _o += IN_SIZES[_s]
NARROW_COLS = LANES
LANE_SSD_DT = NARROW_OFF[SEG_SSD_DT]
LANE_GDN_A = NARROW_OFF[SEG_GDN_A]
LANE_GDN_B = NARROW_OFF[SEG_GDN_B]
LANE_ML_I = NARROW_OFF[SEG_ML_I]
LANE_ML_F = NARROW_OFF[SEG_ML_F]

PROMPT_SEQS_PER_STEP = 4
SAMPLE_SEQS_PER_STEP = 8
CONV_PAD = SUBLANES
CONV_PREV = CONV_W - 1


def _cparams(*sem):
    return pltpu.CompilerParams(dimension_semantics=sem, vmem_limit_bytes=VMEM_LIMIT_BYTES)


def _dot(a, b):
    return jnp.dot(a.astype(BF16), b.astype(BF16), preferred_element_type=F32)


def _dot_nt(a, b):
    return lax.dot_general(a.astype(BF16), b.astype(BF16), (((1,), (1,)), ((), ())), preferred_element_type=F32)


def _dot_tn(a, b):
    return lax.dot_general(a.astype(BF16), b.astype(BF16), (((0,), (0,)), ((), ())), preferred_element_type=F32)


def _dot_f32(a, b):
    return jnp.dot(a, b, preferred_element_type=F32, precision=lax.Precision.HIGHEST)


def _silu(x):
    return x * jax.nn.sigmoid(x)


def _mm_kernel(a_ref, w_ref, o_ref):
    o_ref[...] = jnp.dot(a_ref[...], w_ref[...], preferred_element_type=F32).astype(o_ref.dtype)


def _matmul(a, w, layer, tn, name, row0, m):
    k = a.shape[1]
    n = w.shape[2]
    assert m % MM_TM == 0 and row0 % MM_TM == 0 and n % tn == 0
    tile0 = row0 // MM_TM
    return pl.pallas_call(
        _mm_kernel,
        grid=(m // MM_TM, n // tn),
        in_specs=[pl.BlockSpec((MM_TM, k), lambda i, j: (tile0 + i, 0)),
                  pl.BlockSpec((None, k, tn), lambda i, j: (layer, 0, j))],
        out_specs=pl.BlockSpec((MM_TM, tn), lambda i, j: (i, j)),
        out_shape=jax.ShapeDtypeStruct((m, n), F32),
        compiler_params=_cparams("parallel", "arbitrary"),
        name=name,
    )(a, w)


PROMPT_MM_TILES = N_PROMPT_TOK // MM_TM
assert N_SAMPLE_TOK == MM_TM


def _out_proj_kernel(*refs):
    yp_refs, ys_refs = refs[:N_MIXERS], refs[N_MIXERS:2 * N_MIXERS]
    w_refs, o_ref = refs[2 * N_MIXERS:3 * N_MIXERS], refs[3 * N_MIXERS]

    def project(y_refs):
        acc = jnp.dot(y_refs[0][...], w_refs[0][...], preferred_element_type=F32)
        for m in range(1, N_MIXERS):
            acc += jnp.dot(y_refs[m][...], w_refs[m][...], preferred_element_type=F32)
        o_ref[...] = acc

    is_prompt = pl.program_id(0) < PROMPT_MM_TILES
    pl.when(is_prompt)(lambda: project(yp_refs))
    pl.when(jnp.logical_not(is_prompt))(lambda: project(ys_refs))


def _out_proj(ys_prompt, ys_sample, w, layer):
    n = w.shape[2]
    yp_spec = pl.BlockSpec((MM_TM, D_GROUP), lambda i, j: (jnp.minimum(i, PROMPT_MM_TILES - 1), 0))
    ys_spec = pl.BlockSpec((MM_TM, D_GROUP), lambda i, j: (0, 0))
    w_specs = [pl.BlockSpec((None, D_GROUP, MM_TN), functools.partial(lambda i, j, m: (layer, m, j), m=m))
               for m in range(N_MIXERS)]
    return pl.pallas_call(
        _out_proj_kernel,
        grid=(N_TOK // MM_TM, n // MM_TN),
        in_specs=[yp_spec] * N_MIXERS + [ys_spec] * N_MIXERS + w_specs,
        out_specs=pl.BlockSpec((MM_TM, MM_TN), lambda i, j: (i, j)),
        out_shape=jax.ShapeDtypeStruct((N_TOK, n), F32),
        compiler_params=_cparams("parallel", "arbitrary"),
        name="out_proj",
    )(*ys_prompt, *ys_sample, w, w, w, w)


W_IN_ROW_TILE = 256


def _w_in_relayout_kernel(wt_ref, wide_ref, narrow_ref):
    for s in _WIDE_SEGS:
        wide_ref[:, WIDE_OFF[s]:WIDE_OFF[s] + IN_SIZES[s]] = (
            wt_ref[IN_OFFSETS[s]:IN_OFFSETS[s + 1], :].T.astype(BF16))
    rows = [wt_ref[IN_OFFSETS[s]:IN_OFFSETS[s + 1], :] for s in _NARROW_SEGS]
    used = sum(IN_SIZES[s] for s in _NARROW_SEGS)
    rows.append(jnp.zeros((NARROW_COLS - used, wt_ref.shape[1]), F32))
    narrow_ref[...] = jnp.concatenate(rows, axis=0).T.astype(BF16)


def _w_in_relayout(w_in):
    wt = jnp.swapaxes(w_in, 1, 2)
    depth, n, k = wt.shape
    return pl.pallas_call(
        _w_in_relayout_kernel,
        grid=(depth, k // W_IN_ROW_TILE),
        in_specs=[pl.BlockSpec((None, n, W_IN_ROW_TILE), lambda l, i: (l, 0, i))],
        out_specs=[pl.BlockSpec((None, W_IN_ROW_TILE, WIDE_COLS), lambda l, i: (l, i, 0)),
                   pl.BlockSpec((None, W_IN_ROW_TILE, NARROW_COLS), lambda l, i: (l, i, 0))],
        out_shape=[jax.ShapeDtypeStruct((depth, k, WIDE_COLS), BF16),
                   jax.ShapeDtypeStruct((depth, k, NARROW_COLS), BF16)],
        compiler_params=_cparams("parallel", "parallel"),
        name="w_in_relayout",
    )(wt)


def _ada_kernel(c_ref, w_ref, o_ref):
    o_ref[...] = _dot(_silu(c_ref[...]), w_ref[...])


def _ada_matmul(c_pad, w_ada):
    k, n = w_ada.shape
    return pl.pallas_call(
        _ada_kernel,
        grid=(n // ADA_TN,),
        in_specs=[pl.BlockSpec((ADA_ROWS, k), lambda j: (0, 0)),
                  pl.BlockSpec((k, ADA_TN), lambda j: (0, j))],
        out_specs=pl.BlockSpec((ADA_ROWS, ADA_TN), lambda j: (0, j)),
        out_shape=jax.ShapeDtypeStruct((ADA_ROWS, n), F32),
        compiler_params=_cparams("arbitrary"),
        name="adaln_matmul",
    )(c_pad, w_ada)


def _route(lg):
    lane = lax.broadcasted_iota(jnp.int32, lg.shape, 1).astype(F32)
    neg = -jnp.inf
    big = float(ROUTER_LANES)
    gl = jnp.where(lane < N_EXPERT_GROUPS, lg, neg)
    gmax = jnp.max(gl, axis=-1, keepdims=True)
    gidx = jnp.min(jnp.where(gl == gmax, lane, big), axis=-1, keepdims=True)
    g_w = 1.0 / jnp.sum(jnp.exp(gl - gmax), axis=-1, keepdims=True)
    lo = EXPERT_LANE0 + EXPERTS_PER_GROUP * gidx
    el = jnp.where((lane >= lo) & (lane < lo + EXPERTS_PER_GROUP), lg, neg)
    e1 = jnp.max(el, axis=-1, keepdims=True)
    i1 = jnp.min(jnp.where(el == e1, lane, big), axis=-1, keepdims=True)
    el2 = jnp.where(lane == i1, neg, el)
    e2 = jnp.max(el2, axis=-1, keepdims=True)
    i2 = jnp.min(jnp.where(el2 == e2, lane, big), axis=-1, keepdims=True)
    t = jnp.exp(e2 - e1)
    w1 = g_w / (1.0 + t)
    w2 = g_w * t / (1.0 + t)
    out = jnp.where(lane == i1, w1, 0.0) + jnp.where(lane == i2, w2, 0.0)
    pa = jnp.minimum(i1, i2) - lo
    pb = jnp.maximum(i1, i2) - lo
    pair = pa * (2 * EXPERTS_PER_GROUP - 1 - pa) * 0.5 + (pb - pa - 1.0)
    return out + jnp.where(lane == float(BUCKET_LANE), gidx * N_PAIRS + pair, 0.0)


def _rowwise_kernel(*refs, has_resid, has_mod, has_router):
    it = iter(refs)
    x_ref = next(it)
    if has_resid:
        y_ref, gate_m, gate_t = next(it), next(it), next(it)
    g_ref = next(it)
    if has_mod:
        sc_m, sc_t, sh_m, sh_t = next(it), next(it), next(it), next(it)
    if has_router:
        wr_ref, br_ref = next(it), next(it)
    if has_resid:
        xo_ref = next(it)
    h_ref = next(it)
    if has_router:
        gt_ref = next(it)

    def mod(m_ref, t_ref):
        m = m_ref[...] + t_ref[...]
        return jnp.broadcast_to(m[:, None, :], (MOD_ROWS, MOD_REPEAT, m.shape[-1])).reshape(ROW_TILE, m.shape[-1])

    x = x_ref[...]
    if has_resid:
        x = x + mod(gate_m, gate_t) * y_ref[...]
        xo_ref[...] = x
    y = x * lax.rsqrt(jnp.mean(x * x, axis=-1, keepdims=True) + EPS)
    h = y * g_ref[...]
    if has_mod:
        h = h * (1.0 + mod(sc_m, sc_t)) + mod(sh_m, sh_t)
    h_ref[...] = h.astype(h_ref.dtype)
    if has_router:
        h_hi = h.astype(BF16)
        h_lo = (h - h_hi.astype(F32)).astype(BF16)
        hi_terms = jnp.dot(h_hi, wr_ref[...], preferred_element_type=F32)
        lg = (hi_terms[:, :ROUTER_LANES] + hi_terms[:, ROUTER_LANES:]
              + jnp.dot(h_lo, wr_ref[:, :ROUTER_LANES], preferred_element_type=F32))
        gt_ref[...] = _route(lg + br_ref[...])


def _mod_tile_index(i):
    return jnp.where(i < PROMPT_ROW_TILES, i // PROMPT_TILES_PER_SEQ, BATCH + i - PROMPT_ROW_TILES)


def _rowwise(x, g_row, *, y=None, mod_tiles=None, ada_rows=None, layer=0, gate_chunk=None, gate_layer=None,
             mod_chunks=None, router=None, h_dtype=BF16, name="rowwise"):
    has_resid = y is not None
    has_mod = mod_chunks is not None
    has_router = router is not None
    d = D_MODEL
    row_spec = pl.BlockSpec((ROW_TILE, d), lambda i: (i, 0))
    vec_spec = pl.BlockSpec((1, d), lambda i: (0, 0))

    def mod_specs(chunk, lyr):
        return [pl.BlockSpec((None, MOD_ROWS, d), lambda i: (_mod_tile_index(i), 0, chunk)),
                pl.BlockSpec((None, 1, d), lambda i: (lyr * 6 + chunk, 0, 0))]

    args, specs = [x], [row_spec]
    if has_resid:
        args += [y, mod_tiles, ada_rows]
        specs += [row_spec] + mod_specs(gate_chunk, layer if gate_layer is None else gate_layer)
    args.append(g_row)
    specs.append(vec_spec)
    if has_mod:
        sc_chunk, sh_chunk = mod_chunks
        args += [mod_tiles, ada_rows, mod_tiles, ada_rows]
        specs += mod_specs(sc_chunk, layer) + mod_specs(sh_chunk, layer)
    if has_router:
        w_r, b_r = router
        args += [w_r, b_r]
        specs += [pl.BlockSpec((d, 2 * ROUTER_LANES), lambda i: (0, 0)),
                  pl.BlockSpec((1, ROUTER_LANES), lambda i: (0, 0))]
    out_shape, out_specs = [], []
    if has_resid:
        out_shape.append(jax.ShapeDtypeStruct((N_TOK, d), F32))
        out_specs.append(row_spec)
    out_shape.append(jax.ShapeDtypeStruct((N_TOK, d), h_dtype))
    out_specs.append(row_spec)
    if has_router:
        out_shape.append(jax.ShapeDtypeStruct((N_TOK, ROUTER_LANES), F32))
        out_specs.append(pl.BlockSpec((ROW_TILE, ROUTER_LANES), lambda i: (i, 0)))
    return pl.pallas_call(
        functools.partial(_rowwise_kernel, has_resid=has_resid, has_mod=has_mod, has_router=has_router),
        grid=(N_ROW_TILES,),
        in_specs=specs,
        out_specs=out_specs,
        out_shape=out_shape,
        compiler_params=_cparams("parallel"),
        name=name,
    )(*args)


def _moe_kernel(ea_ref, eb_ref, nu_ref, x_ref, gt_ref, wg_ref, wu_ref, wd_ref, o_ref):
    i = pl.program_id(0)
    j = pl.program_id(1)
    used = i < nu_ref[0]

    @pl.when(j == 0)
    def _():
        o_ref[...] = jnp.zeros_like(o_ref)

    @pl.when(used)
    def _():
        x = x_ref[...]
        a = jnp.dot(x, wg_ref[...], preferred_element_type=F32)
        b = jnp.dot(x, wu_ref[...], preferred_element_type=F32)
        he = _silu(a) * b
        gates = gt_ref[...]
        lane = lax.broadcasted_iota(jnp.int32, gates.shape, 1)
        col = EXPERT_LANE0 + jnp.where(j < MOE_FF_STEPS, ea_ref[i], eb_ref[i])
        gcol = jnp.sum(jnp.where(lane == col, gates, 0.0), axis=-1, keepdims=True)
        o_ref[...] += gcol * jnp.dot(he.astype(BF16), wd_ref[...], preferred_element_type=F32)


def _moe(x_sorted, gates_sorted, tile_ea, tile_eb, n_used, w_gate, w_up, w_down, layer):
    d = D_MODEL

    def expert_of(i, j, ea, eb, nu):
        jj = jnp.where(i < nu[0], j, MOE_STEPS_PER_TILE - 1)
        return jnp.where(jj < MOE_FF_STEPS, ea[i], eb[i]), jj % MOE_FF_STEPS

    def in_row_map(i, j, ea, eb, nu):
        return (jnp.minimum(i, jnp.minimum(nu[0], MOE_MAX_TILES - 1)), 0)

    def wgu_map(i, j, ea, eb, nu):
        e, f = expert_of(i, j, ea, eb, nu)
        return (layer, e, 0, f)

    def wd_map(i, j, ea, eb, nu):
        e, f = expert_of(i, j, ea, eb, nu)
        return (layer, e, f, 0)

    grid_spec = pltpu.PrefetchScalarGridSpec(
        num_scalar_prefetch=3,
        grid=(MOE_MAX_TILES, MOE_STEPS_PER_TILE),
        in_specs=[pl.BlockSpec((MOE_TM, d), in_row_map),
                  pl.BlockSpec((MOE_TM, ROUTER_LANES), in_row_map),
                  pl.BlockSpec((None, None, d, MOE_FF_TILE), wgu_map),
                  pl.BlockSpec((None, None, d, MOE_FF_TILE), wgu_map),
                  pl.BlockSpec((None, None, MOE_FF_TILE, d), wd_map)],
        out_specs=pl.BlockSpec((MOE_TM, d), lambda i, j, ea, eb, nu: (i, 0)),
    )
    return pl.pallas_call(
        _moe_kernel,
        grid_spec=grid_spec,
        out_shape=jax.ShapeDtypeStruct((MOE_SLOTS, d), F32),
        compiler_params=_cparams("arbitrary", "arbitrary"),
        name="moe_experts",
    )(tile_ea, tile_eb, n_used, x_sorted, gates_sorted, w_gate, w_up, w_down)


def _take_rows(a, idx):
    return a.at[idx].get(mode="promise_in_bounds")


def _moe_plan(gates):
    bid = gates[:, BUCKET_LANE].astype(jnp.int32)
    onehot = (bid[:, None] == jnp.arange(N_BUCKETS, dtype=jnp.int32)[None, :]).astype(jnp.int32)
    rank = jnp.take_along_axis(jnp.cumsum(onehot, axis=0), bid[:, None], axis=1)[:, 0] - 1
    counts = jnp.sum(onehot, axis=0)
    tiles = (counts + MOE_TM - 1) // MOE_TM
    tile_end = jnp.cumsum(tiles)
    tile_start = tile_end - tiles
    dest = tile_start[bid] * MOE_TM + rank
    src = (jnp.arange(MOE_SLOTS, dtype=jnp.int32) % N_TOK).at[dest].set(jnp.arange(N_TOK, dtype=jnp.int32))
    n_used = tile_end[-1]
    t = jnp.arange(MOE_MAX_TILES, dtype=jnp.int32)
    tile_bucket = jnp.sum((jnp.minimum(t, n_used - 1)[:, None] >= tile_end[None, :]).astype(jnp.int32), axis=1)
    slot = jnp.arange(MOE_SLOTS, dtype=jnp.int32)
    slot_bucket = jnp.repeat(tile_bucket, MOE_TM)
    valid = ((slot - tile_start[slot_bucket] * MOE_TM < counts[slot_bucket])
             & (slot < n_used * MOE_TM)).astype(F32)
    group0 = (tile_bucket // N_PAIRS) * EXPERTS_PER_GROUP
    pair = tile_bucket % N_PAIRS
    tile_ea = group0 + jnp.asarray([p[0] for p in PAIRS], jnp.int32)[pair]
    tile_eb = group0 + jnp.asarray([p[1] for p in PAIRS], jnp.int32)[pair]
    return src, valid, dest, tile_ea, tile_eb, n_used.reshape(1).astype(jnp.int32)


def _conv_init(win_scr, c0_ref):
    win_scr[0:CONV_PAD, :] = jnp.zeros((CONV_PAD, win_scr.shape[1]), F32)
    win_scr[CONV_PAD - CONV_PREV:CONV_PAD, :] = c0_ref[...]


def _conv_step(u_ref, win_scr, w_ref, b_ref, cs):
    u = u_ref[...]
    prev = win_scr[0:CONV_PAD, :]
    row = lax.broadcasted_iota(jnp.int32, (CONV_PAD, 1), 0)
    out = b_ref[...]
    for j in range(CONV_W):
        shift = CONV_PREV - j
        if shift == 0:
            tap = u
        else:
            rolled = pltpu.roll(u, shift, 0)
            head = jnp.where(row >= shift, rolled[0:CONV_PAD, :], pltpu.roll(prev, shift, 0))
            tap = head if cs == CONV_PAD else jnp.concatenate([head, rolled[CONV_PAD:, :]], axis=0)
        out = out + tap * w_ref[j:j + 1, :]
    win_scr[CONV_PAD:2 * CONV_PAD, :] = u[cs - CONV_PAD:cs, :]
    return out


def _conv_advance(win_scr, cs):
    win_scr[0:CONV_PAD, :] = win_scr[CONV_PAD:2 * CONV_PAD, :]


def _causal_masks(cs):
    row = lax.broadcasted_iota(jnp.int32, (cs, cs), 0)
    col = lax.broadcasted_iota(jnp.int32, (cs, cs), 1)
    return row >= col, row > col


def _cumsum_rows(x):
    cs = x.shape[0]
    row = lax.broadcasted_iota(jnp.int32, x.shape, 0)
    shift = 1
    while shift < cs:
        x = x + jnp.where(row >= shift, pltpu.roll(x, shift, 0), 0.0)
        shift *= 2
    return x


def _rms(x, width):
    return x * lax.rsqrt(jnp.sum(x * x, axis=-1, keepdims=True) * (1.0 / width) + EPS)


def _seq_specs(cs, layer, g):
    def rows(width, col):
        return pl.BlockSpec((g, cs, width), lambda b, c: (b, c, col))

    def const(shape):
        nd = len(shape)
        return pl.BlockSpec(shape, lambda b, c: (0,) * nd)

    def per_seq(shape, last=0):
        nd = len(shape)
        return pl.BlockSpec((g,) + shape, lambda b, c: (b,) + (0,) * (nd - 1) + (last,))

    def state_in(shape, last=0):
        nd = len(shape)
        return pl.BlockSpec((None, g) + shape, lambda b, c: (layer, b) + (0,) * (nd - 1) + (last,))

    return rows, const, per_seq, state_in


def _seq_batched_kernel(*refs, body, per_seq_refs, alias_range, g, cs, nc):
    refs = refs[:alias_range[0]] + refs[alias_range[1]:]
    c = pl.program_id(1)
    views = [tuple(r.at[s] if flag else r for r, flag in zip(refs, per_seq_refs)) for s in range(g)]

    @pl.when(c == 0)
    def _():
        for v in views:
            body(*v, cs=cs, phase="init")

    for v in views:
        body(*v, cs=cs, phase="main")

    @pl.when(c == nc - 1)
    def _():
        for v in views:
            body(*v, cs=cs, phase="final")


def _seq_call(body, name, in_arrays, in_specs, n_const, state_shapes, scratch, prev, *,
              n_seq, seq_len, cs, g, out_layer):
    assert n_seq % g == 0
    nc = seq_len // cs
    n_out = 1 + len(state_shapes)
    flags = ([True] * (len(in_arrays) - n_const) + [False] * n_const + [True] * (n_out + len(scratch)))
    out_specs = [pl.BlockSpec((g, cs, D_GROUP), lambda b, c: (b, c, 0))]
    out_shape = [jax.ShapeDtypeStruct((n_seq, seq_len, D_GROUP), BF16)]
    for shape in state_shapes:
        nd = len(shape)
        out_specs.append(pl.BlockSpec((None, g) + shape, lambda b, c, nd=nd: (out_layer, b) + (0,) * nd))
        out_shape.append(jax.ShapeDtypeStruct((DEPTH, n_seq) + shape, F32))
    n_in = len(in_arrays)
    aliases = {}
    if prev is not None:
        in_arrays = tuple(in_arrays) + tuple(prev)
        in_specs = list(in_specs) + [pl.BlockSpec(memory_space=pl.ANY)] * len(prev)
        aliases = {n_in + k: 1 + k for k in range(len(prev))}
    return pl.pallas_call(
        functools.partial(_seq_batched_kernel, body=body, per_seq_refs=tuple(flags),
                          alias_range=(n_in, len(in_arrays)), g=g, cs=cs, nc=nc),
        grid=(n_seq // g, nc),
        in_specs=in_specs,
        out_specs=out_specs,
        out_shape=out_shape,
        scratch_shapes=[pltpu.VMEM((g,) + shape, F32) for shape in scratch],
        input_output_aliases=aliases,
        compiler_params=_cparams("parallel", "arbitrary"),
        name=name,
    )(*in_arrays)


def _ssd_kernel(z_ref, x_ref, bc_ref, nar_ref, s0_ref, c0x_ref, c0bc_ref,
                cwx_ref, cbx_ref, cwbc_ref, cbbc_ref, bias_ref, alog_ref, dsk_ref, ng_ref,
                y_ref, s_out_ref, cx_out_ref, cbc_out_ref,
                st_scr, wx_scr, wbc_scr, ycat_scr, *, cs, phase):
    lo = CONV_PAD - CONV_PREV
    if phase == "init":
        st_scr[...] = s0_ref[...]
        _conv_init(wx_scr, c0x_ref)
        _conv_init(wbc_scr, c0bc_ref)
        return
    if phase == "final":
        s_out_ref[...] = st_scr[...]
        cx_out_ref[...] = wx_scr[lo:CONV_PAD, :]
        cbc_out_ref[...] = wbc_scr[lo:CONV_PAD, :]
        return

    xs = _silu(_conv_step(x_ref, wx_scr, cwx_ref, cbx_ref, cs))
    bcs = _silu(_conv_step(bc_ref, wbc_scr, cwbc_ref, cbbc_ref, cs))
    incl, _ = _causal_masks(cs)
    dt = jax.nn.softplus(nar_ref[...] + bias_ref[...])
    cum = _cumsum_rows(dt * (-jnp.exp(alog_ref[...])))
    cum_t = cum.T
    exp_cum = jnp.exp(cum)
    cum_last = cum[cs - 1:cs, :]
    w_end = jnp.exp(cum_last - cum)
    chunk_decay = jnp.exp(cum_last)
    heads_per_group = SSD_HEADS // SSD_NGROUPS
    heads = range(SSD_HEADS)
    hp = SSD_HEAD_DIM
    s_old = [st_scr[h] for h in heads]
    col = lambda arr, h: arr[:, LANE_SSD_DT + h:LANE_SSD_DT + h + 1]
    b_gs = [bcs[:, g * SSD_STATE:(g + 1) * SSD_STATE] for g in range(SSD_NGROUPS)]
    c_gs = [bcs[:, (SSD_NGROUPS + g) * SSD_STATE:(SSD_NGROUPS + g + 1) * SSD_STATE] for g in range(SSD_NGROUPS)]
    cbs = [_dot_nt(c_gs[g], b_gs[g]) for g in range(SSD_NGROUPS)]
    grp = [h // heads_per_group for h in heads]
    scores = [cbs[grp[h]] * jnp.exp(jnp.where(incl, col(cum, h) - cum_t[LANE_SSD_DT + h:LANE_SSD_DT + h + 1, :],
                                              -jnp.inf)) for h in heads]
    xdts = [xs[:, h * hp:(h + 1) * hp] * col(dt, h) for h in heads]
    ys = [_dot(scores[h], xdts[h]) + _dot_nt(c_gs[grp[h]] * col(exp_cum, h), s_old[h]) for h in heads]
    s_new = [s_old[h] * col(chunk_decay, h) + _dot_tn(xdts[h] * col(w_end, h), b_gs[grp[h]]) for h in heads]
    for h in heads:
        st_scr[h] = s_new[h]
        ycat_scr[:, h * hp:(h + 1) * hp] = ys[h]
    y = (ycat_scr[...] + dsk_ref[...] * xs) * _silu(z_ref[...])
    gw = D_GROUP // SSD_NGROUPS
    for g in range(SSD_NGROUPS):
        y_g = _rms(y[:, g * gw:(g + 1) * gw], gw) * ng_ref[:, g * gw:(g + 1) * gw]
        y_ref[:, g * gw:(g + 1) * gw] = y_g.astype(y_ref.dtype)
    _conv_advance(wx_scr, cs)
    _conv_advance(wbc_scr, cs)


def _ssd_call(wide, narrow, s0, conv0, prm, prev, *, n_seq, seq_len, cs, g, layer, out_layer):
    rows, const, per_seq, state_in = _seq_specs(cs, layer, g)
    xw, bcw = D_GROUP, SSD_BC_DIM
    x_off = WIDE_OFF[SEG_SSD_XBC]
    in_specs = [rows(xw, WIDE_OFF[SEG_SSD_Z] // xw), rows(xw, x_off // xw), rows(bcw, (x_off + xw) // bcw),
                rows(NARROW_COLS, 0),
                state_in((SSD_HEADS, SSD_HEAD_DIM, SSD_STATE)),
                state_in((CONV_PREV, xw)), state_in((CONV_PREV, bcw), last=xw // bcw),
                const((CONV_W, xw)), const((1, xw)), const((CONV_W, bcw)), const((1, bcw)),
                const((1, LANES)), const((1, LANES)), const((1, xw)), const((1, xw))]
    state_shapes = [(SSD_HEADS, SSD_HEAD_DIM, SSD_STATE), (CONV_PREV, xw), (CONV_PREV, bcw)]
    scratch = [(SSD_HEADS, SSD_HEAD_DIM, SSD_STATE), (CONV_PAD + cs, xw), (CONV_PAD + cs, bcw), (cs, xw)]
    in_arrays = (wide, wide, wide, narrow, s0, conv0, conv0,
                 prm['cw'][:, :xw], prm['cb'][:, :xw], prm['cw'][:, xw:], prm['cb'][:, xw:],
                 prm['bias'], prm['alog'], prm['dskip'], prm['ng'])
    return _seq_call(_ssd_kernel, "ssd_mixer", in_arrays, in_specs, 8, state_shapes, scratch, prev,
                     n_seq=n_seq, seq_len=seq_len, cs=cs, g=g, out_layer=out_layer)


def _ssd_params(conv_w, conv_b, dt_bias, a_log, d_skip, norm_g):
    pad = jnp.zeros((LANES - SSD_HEADS,), F32)
    return dict(cw=conv_w, cb=conv_b[None, :],
                bias=jnp.concatenate([dt_bias, pad])[None, :],
                alog=jnp.concatenate([a_log, pad])[None, :],
                dskip=jnp.repeat(d_skip, SSD_HEAD_DIM)[None, :], ng=norm_g[None, :])


def _split_hi_lo(x):
    hi = x.astype(BF16).astype(F32)
    lo = (x - hi).astype(BF16).astype(F32)
    return hi, lo


def _lhs3(a):
    hi, lo = _split_hi_lo(a)
    return jnp.concatenate([hi, lo, hi], axis=1).astype(BF16)


def _rhs3(b):
    hi, lo = _split_hi_lo(b)
    return jnp.concatenate([hi, hi, lo], axis=0).astype(BF16)


def _dot3(lhs3, rhs3):
    return jnp.dot(lhs3, rhs3, preferred_element_type=F32)


def _inv_unit_lower(a_list, cs):
    row = lax.broadcasted_iota(jnp.int32, (cs, cs), 0)
    col = lax.broadcasted_iota(jnp.int32, (cs, cs), 1)
    eye = jnp.where(row == col, 1.0, 0.0)
    ps = [-a for a in a_list]
    ts = [eye + p for p in ps]
    forms = [(_lhs3(p), _rhs3(p)) for p in ps]
    n = 1
    while 2 * n < cs:
        ps = [_dot3(lhs, rhs) for lhs, rhs in forms]
        forms = [(_lhs3(p), _rhs3(p)) for p in ps]
        ts = [t + _dot3(_lhs3(t), rhs) for t, (_, rhs) in zip(ts, forms)]
        n *= 2
    return ts


def _gdn_kernel(qkv_ref, z_ref, nar_ref, s0_ref, c0_ref, cw_ref, cb_ref, bias_ref, alog_ref, ng_ref,
                y_ref, s_out_ref, c_out_ref, st_scr, win_scr, *, cs, phase):
    if phase == "init":
        st_scr[...] = s0_ref[...]
        _conv_init(win_scr, c0_ref)
        return
    if phase == "final":
        s_out_ref[...] = st_scr[...]
        c_out_ref[...] = win_scr[CONV_PAD - CONV_PREV:CONV_PAD, :]
        return

    qkv = _silu(_conv_step(qkv_ref, win_scr, cw_ref, cb_ref, cs))
    incl, strict = _causal_masks(cs)
    nar = nar_ref[...]
    g_log = -jnp.exp(alog_ref[...]) * jax.nn.softplus(nar + bias_ref[...])
    beta_all = jax.nn.sigmoid(nar)
    gc = _cumsum_rows(g_log)
    gc_t = gc.T
    exp_gc = jnp.exp(gc)
    gc_last = gc[cs - 1:cs, :]
    exp_to_end = jnp.exp(gc_last - gc)
    g_last = jnp.exp(gc_last)
    hd = GDN_HEAD_DIM
    heads = range(GDN_HEADS)
    s_old = [st_scr[h] for h in heads]
    col = lambda arr, lane: arr[:, lane:lane + 1]
    qs = [qkv[:, h * hd:(h + 1) * hd] for h in heads]
    ks = [qkv[:, D_GROUP + h * hd:D_GROUP + (h + 1) * hd] for h in heads]
    vs = [qkv[:, 2 * D_GROUP + h * hd:2 * D_GROUP + (h + 1) * hd] for h in heads]
    qs = [q * lax.rsqrt(jnp.sum(q * q, axis=-1, keepdims=True) + EPS) * (hd ** -0.5) for q in qs]
    ks = [k * lax.rsqrt(jnp.sum(k * k, axis=-1, keepdims=True) + EPS) for k in ks]
    decays = [jnp.exp(jnp.where(incl, col(gc, LANE_GDN_A + h) - gc_t[LANE_GDN_A + h:LANE_GDN_A + h + 1, :],
                                -jnp.inf)) for h in heads]
    betas = [col(beta_all, LANE_GDN_B + h) for h in heads]
    a_mats = [jnp.where(strict, betas[h] * _dot_nt(ks[h], ks[h]) * decays[h], 0.0) for h in heads]
    t_invs = _inv_unit_lower(a_mats, cs)
    rhs = [jnp.concatenate([vs[h] * betas[h], ks[h] * (betas[h] * col(exp_gc, LANE_GDN_A + h))], axis=1)
           for h in heads]
    uw = [_dot3(_lhs3(t_invs[h]), _rhs3(rhs[h])) for h in heads]
    qks = [_dot_nt(qs[h], ks[h]) * decays[h] for h in heads]
    v_new = [uw[h][:, :hd] - _dot(uw[h][:, hd:], s_old[h]) for h in heads]
    outs = [_dot(qs[h] * col(exp_gc, LANE_GDN_A + h), s_old[h]) + _dot(qks[h], v_new[h]) for h in heads]
    s_new = [s_old[h] * col(g_last, LANE_GDN_A + h) + _dot_tn(ks[h] * col(exp_to_end, LANE_GDN_A + h), v_new[h])
             for h in heads]
    for h in heads:
        st_scr[h] = s_new[h]
    y = jnp.concatenate([_rms(o, hd) for o in outs], axis=1) * ng_ref[...] * _silu(z_ref[...])
    y_ref[...] = y.astype(y_ref.dtype)
    _conv_advance(win_scr, cs)


def _gdn_call(wide, narrow, s0, conv0, prm, prev, *, n_seq, seq_len, cs, g, layer, out_layer):
    rows, const, per_seq, state_in = _seq_specs(cs, layer, g)
    hd = GDN_HEAD_DIM
    in_specs = [rows(GDN_CONV_DIM, WIDE_OFF[SEG_GDN_QKV] // GDN_CONV_DIM),
                rows(D_GROUP, WIDE_OFF[SEG_GDN_Z] // D_GROUP),
                rows(NARROW_COLS, 0),
                state_in((GDN_HEADS, hd, hd)), state_in((CONV_PREV, GDN_CONV_DIM)),
                const((CONV_W, GDN_CONV_DIM)), const((1, GDN_CONV_DIM)),
                const((1, LANES)), const((1, LANES)), const((1, D_GROUP))]
    state_shapes = [(GDN_HEADS, hd, hd), (CONV_PREV, GDN_CONV_DIM)]
    scratch = [(GDN_HEADS, hd, hd), (CONV_PAD + cs, GDN_CONV_DIM)]
    in_arrays = (wide, wide, narrow, s0, conv0, prm['cw'], prm['cb'], prm['bias'], prm['alog'], prm['ng'])
    return _seq_call(_gdn_kernel, "gdn_mixer", in_arrays, in_specs, 5, state_shapes, scratch, prev,
                     n_seq=n_seq, seq_len=seq_len, cs=cs, g=g, out_layer=out_layer)


def _lane_row(vec, lane0):
    return jnp.zeros((LANES,), F32).at[lane0:lane0 + vec.shape[0]].set(vec)[None, :]


def _gdn_params(conv_w, conv_b, dt_bias, a_log, norm_g):
    return dict(cw=conv_w, cb=conv_b[None, :], bias=_lane_row(dt_bias, LANE_GDN_A),
                alog=_lane_row(a_log, LANE_GDN_A), ng=norm_g[None, :])


def _mlstm_kernel(qkv_ref, o_ref, nar_ref, c0_ref, n0_ref, m0_ref, bias_ref, ng_ref,
                  y_ref, c_out_ref, n_out_ref, m_out_ref, c_scr, n_scr, m_scr, *, cs, phase):
    if phase == "init":
        c_scr[...] = c0_ref[...]
        n_scr[...] = n0_ref[...]
        m_scr[...] = m0_ref[...]
        return
    if phase == "final":
        c_out_ref[...] = c_scr[...]
        n_out_ref[...] = n_scr[...]
        m_out_ref[...] = m_scr[...]
        return

    incl, _ = _causal_masks(cs)
    pre = nar_ref[...] + bias_ref[...]
    f_cum = _cumsum_rows(jax.nn.log_sigmoid(pre))
    f_cum_t = f_cum.T
    pre_t = pre.T
    hd = MLSTM_HEAD_DIM
    heads = range(MLSTM_HEADS)
    c_old = [c_scr[h] for h in heads]
    n_all = n_scr[...]
    m_all = m_scr[...]
    qkv = qkv_ref[...]
    col = lambda arr, lane: arr[:, lane:lane + 1]
    qs = [qkv[:, h * hd:(h + 1) * hd] for h in heads]
    ks = [qkv[:, D_GROUP + h * hd:D_GROUP + (h + 1) * hd] * (hd ** -0.5) for h in heads]
    vs = [qkv[:, 2 * D_GROUP + h * hd:2 * D_GROUP + (h + 1) * hd] for h in heads]
    f_cols = [col(f_cum, LANE_ML_F + h) for h in heads]
    d_logs = [jnp.where(incl, f_cols[h] - f_cum_t[LANE_ML_F + h:LANE_ML_F + h + 1, :]
                        + pre_t[LANE_ML_I + h:LANE_ML_I + h + 1, :], -jnp.inf) for h in heads]
    m_prev = [col(m_all, LANE_ML_I + h) for h in heads]
    m_t = [jnp.maximum(f_cols[h] + m_prev[h], jnp.max(d_logs[h], axis=-1, keepdims=True)) for h in heads]
    w_carry = [jnp.exp(f_cols[h] + m_prev[h] - m_t[h]) for h in heads]
    ps = [jnp.exp(d_logs[h] - m_t[h]) * _dot_nt(qs[h], ks[h]) for h in heads]
    num = [w_carry[h] * _dot(qs[h], c_old[h]) + _dot(ps[h], vs[h]) for h in heads]
    den = [w_carry[h] * jnp.sum(qs[h] * n_all[h:h + 1, :], axis=-1, keepdims=True)
           + jnp.sum(ps[h], axis=-1, keepdims=True) for h in heads]
    hs = [num[h] / jnp.maximum(jnp.abs(den[h]), jnp.exp(-m_t[h])) for h in heads]
    m_end = [m_t[h][cs - 1:cs, :] for h in heads]
    f_last = [f_cols[h][cs - 1:cs, :] for h in heads]
    w_prev = [jnp.exp(f_last[h] + m_prev[h] - m_end[h]) for h in heads]
    kws = [ks[h] * jnp.exp(f_last[h] - f_cols[h] + col(pre, LANE_ML_I + h) - m_end[h]) for h in heads]
    c_new = [w_prev[h] * c_old[h] + _dot_tn(kws[h], vs[h]) for h in heads]
    n_new = [w_prev[h] * n_all[h:h + 1, :] + jnp.sum(kws[h], axis=0, keepdims=True) for h in heads]
    lane = lax.broadcasted_iota(jnp.int32, m_all.shape, 1)
    m_new = m_all
    for h in heads:
        c_scr[h] = c_new[h]
        m_new = jnp.where(lane == LANE_ML_I + h, m_end[h], m_new)
    n_scr[...] = jnp.concatenate(n_new, axis=0)
    m_scr[...] = m_new
    y = jnp.concatenate([_rms(hh, hd) for hh in hs], axis=1) * ng_ref[...] * jax.nn.sigmoid(o_ref[...])
    y_ref[...] = y.astype(y_ref.dtype)


def _mlstm_call(wide, narrow, c0, n0, m0, prm, prev, *, n_seq, seq_len, cs, g, layer, out_layer):
    rows, const, per_seq, state_in = _seq_specs(cs, layer, g)
    hd = MLSTM_HEAD_DIM
    qkv_w = 3 * D_GROUP
    in_specs = [rows(qkv_w, WIDE_OFF[SEG_ML_QKV] // qkv_w),
                rows(D_GROUP, WIDE_OFF[SEG_ML_O] // D_GROUP),
                rows(NARROW_COLS, 0),
                state_in((MLSTM_HEADS, hd, hd)), state_in((MLSTM_HEADS, hd)), state_in((1, LANES)),
                const((1, LANES)), const((1, D_GROUP))]
    state_shapes = [(MLSTM_HEADS, hd, hd), (MLSTM_HEADS, hd), (1, LANES)]
    scratch = [(MLSTM_HEADS, hd, hd), (MLSTM_HEADS, hd), (1, LANES)]
    in_arrays = (wide, wide, narrow, c0, n0, m0, prm['bias'], prm['ng'])
    return _seq_call(_mlstm_kernel, "mlstm_mixer", in_arrays, in_specs, 2, state_shapes, scratch, prev,
                     n_seq=n_seq, seq_len=seq_len, cs=cs, g=g, out_layer=out_layer)


def _mlstm_params(i_bias, f_bias, norm_g):
    return dict(bias=_lane_row(i_bias, LANE_ML_I) + _lane_row(f_bias, LANE_ML_F), ng=norm_g[None, :])


def _lru_kernel(x_ref, gate_ref, h0_ref, c0_ref, cw_ref, cb_ref, wa_ref, ba_ref, wx_ref, bx_ref, lam_ref,
                y_ref, h_out_ref, c_out_ref, h_scr, win_scr, *, cs, phase):
    if phase == "init":
        h_scr[...] = h0_ref[...]
        _conv_init(win_scr, c0_ref)
        return
    if phase == "final":
        h_out_ref[...] = h_scr[...]
        c_out_ref[...] = win_scr[CONV_PAD - CONV_PREV:CONV_PAD, :]
        return

    xc = _conv_step(x_ref, win_scr, cw_ref, cb_ref, cs)
    r_parts, i_parts = [], []
    for g in range(LRU_BLOCKS):
        x_g = xc[:, g * LRU_BLOCK:(g + 1) * LRU_BLOCK].astype(BF16)
        r_parts.append(jnp.dot(x_g, wa_ref[g], preferred_element_type=F32))
        i_parts.append(jnp.dot(x_g, wx_ref[g], preferred_element_type=F32))
    r = jax.nn.sigmoid(jnp.concatenate(r_parts, axis=-1) + ba_ref[...])
    i = jax.nn.sigmoid(jnp.concatenate(i_parts, axis=-1) + bx_ref[...])
    log_a = -LRU_C * r * jax.nn.softplus(-lam_ref[...])
    a = jnp.exp(log_a)
    u = jnp.sqrt(jnp.tanh(-log_a) * (a * a + 1.0)) * (i * xc)
    row = lax.broadcasted_iota(jnp.int32, (cs, D_GROUP), 0)
    u = u + jnp.where(row == 0, a * h_scr[...], 0.0)
    shift = 1
    while shift < cs:
        a_sh = pltpu.roll(a, shift, 0)
        u_sh = pltpu.roll(u, shift, 0)
        live = row >= shift
        u = jnp.where(live, a * u_sh + u, u)
        a = jnp.where(live, a * a_sh, a)
        shift *= 2
    h_scr[...] = u[cs - 1:cs, :]
    y_ref[...] = (u * jax.nn.gelu(gate_ref[...])).astype(y_ref.dtype)
    _conv_advance(win_scr, cs)


def _lru_call(wide, h0, conv0, prm, prev, *, n_seq, seq_len, cs, g, layer, out_layer):
    rows, const, per_seq, state_in = _seq_specs(cs, layer, g)
    in_specs = [rows(D_GROUP, WIDE_OFF[SEG_LRU_X] // D_GROUP), rows(D_GROUP, WIDE_OFF[SEG_LRU_GATE] // D_GROUP),
                state_in((1, D_GROUP)), state_in((CONV_PREV, D_GROUP)),
                const((CONV_W, D_GROUP)), const((1, D_GROUP)),
                const((LRU_BLOCKS, LRU_BLOCK, LRU_BLOCK)), const((1, D_GROUP)),
                const((LRU_BLOCKS, LRU_BLOCK, LRU_BLOCK)), const((1, D_GROUP)), const((1, D_GROUP))]
    state_shapes = [(1, D_GROUP), (CONV_PREV, D_GROUP)]
    scratch = [(1, D_GROUP), (CONV_PAD + cs, D_GROUP)]
    in_arrays = (wide, wide, h0, conv0, prm['cw'], prm['cb'], prm['wa'], prm['ba'], prm['wx'], prm['bx'], prm['lam'])
    return _seq_call(_lru_kernel, "rglru_mixer", in_arrays, in_specs, 7, state_shapes, scratch, prev,
                     n_seq=n_seq, seq_len=seq_len, cs=cs, g=g, out_layer=out_layer)


def _lru_params(conv_w, conv_b, w_a, b_a, w_x, b_x, lam):
    return dict(cw=conv_w, cb=conv_b[None, :], wa=w_a.astype(BF16), ba=b_a.reshape(1, D_GROUP),
                wx=w_x.astype(BF16), bx=b_x.reshape(1, D_GROUP), lam=lam[None, :])


def _kernel_states(st):
    s_ssd, s_ssd_conv, s_gdn, s_gdn_conv, s_mc, s_mn, s_mm, s_lru, s_lru_conv = st
    lead = s_mm.shape[:2]
    m_rows = jnp.zeros(lead + (1, LANES), F32).at[:, :, 0, LANE_ML_I:LANE_ML_I + MLSTM_HEADS].set(s_mm)
    return (s_ssd, s_ssd_conv, s_gdn, s_gdn_conv, s_mc, s_mn, m_rows, s_lru[:, :, None, :], s_lru_conv)


def _mixer_group(h, w_wide, w_narrow, w_layer, kst, prm, prev, *, n_seq, seq_len, row0, g, layer):
    s_ssd, s_ssd_conv, s_gdn, s_gdn_conv, s_mc, s_mn, m_rows, s_lru, s_lru_conv = kst
    m = n_seq * seq_len
    wide = _matmul(h, w_wide, w_layer, MM_TN, "in_proj", row0, m).reshape(n_seq, seq_len, WIDE_COLS)
    narrow = _matmul(h, w_narrow, w_layer, NARROW_COLS, "in_proj_gates", row0, m).reshape(
        n_seq, seq_len, NARROW_COLS)
    cs = min(seq_len, CHUNK)
    kw = dict(n_seq=n_seq, seq_len=seq_len, cs=cs, g=g, layer=layer, out_layer=w_layer)
    p_ssd, p_gdn, p_ml, p_lru = (None,) * N_MIXERS if prev is None else prev
    y_a, *o_ssd = _ssd_call(wide, narrow, s_ssd, s_ssd_conv, prm['ssd'], p_ssd, **kw)
    y_b, *o_gdn = _gdn_call(wide, narrow, s_gdn, s_gdn_conv, prm['gdn'], p_gdn, **kw)
    y_c, *o_ml = _mlstm_call(wide, narrow, s_mc, s_mn, m_rows, prm['mlstm'], p_ml, **kw)
    y_d, *o_lru = _lru_call(wide, s_lru, s_lru_conv, prm['lru'], p_lru, **kw)
    return tuple(y.reshape(m, D_GROUP) for y in (y_a, y_b, y_c, y_d)), (o_ssd, o_gdn, o_ml, o_lru)


def _reference_states(stacked):
    (ssd, ssd_cx, ssd_cbc), (gdn, gdn_conv), (mc, mn, m_rows), (lru, lru_conv) = stacked
    return (ssd, jnp.concatenate([ssd_cx, ssd_cbc], axis=-1), gdn, gdn_conv, mc, mn,
            m_rows[:, :, 0, LANE_ML_I:LANE_ML_I + MLSTM_HEADS], lru[:, :, 0, :], lru_conv)


def _state_shapes(n):
    return ((n, SSD_HEADS, SSD_HEAD_DIM, SSD_STATE),
            (n, CONV_PREV, SSD_CONV_DIM),
            (n, GDN_HEADS, GDN_HEAD_DIM, GDN_HEAD_DIM),
            (n, CONV_PREV, GDN_CONV_DIM),
            (n, MLSTM_HEADS, MLSTM_HEAD_DIM, MLSTM_HEAD_DIM),
            (n, MLSTM_HEADS, MLSTM_HEAD_DIM),
            (n, MLSTM_HEADS),
            (n, D_GROUP),
            (n, CONV_PREV, D_GROUP))


def kernel(x_prompt, x_sample, state_ssd, state_ssd_conv, state_gdn, state_gdn_conv, state_mlstm_c, state_mlstm_n, state_mlstm_m, state_rglru, state_rglru_conv, c_prompt, c_sample, w_ada, ada_table, norm1_g, norm2_g, final_g, w_in, w_out, ssd_conv_w, ssd_conv_b, ssd_dt_bias, ssd_a_log, ssd_d, ssd_norm_g, gdn_conv_w, gdn_conv_b, gdn_dt_bias, gdn_a_log, gdn_norm_g, mlstm_i_bias, mlstm_f_bias, mlstm_norm_g, lru_conv_w, lru_conv_b, lru_w_a, lru_b_a, lru_w_x, lru_b_x, lru_lambda, moe_w_group, moe_b_group, moe_w_expert, moe_b_expert, moe_w_gate, moe_w_up, moe_w_down):
    d = D_MODEL

    w_in_wide, w_in_narrow = _w_in_relayout(w_in)
    w_out_b = w_out.astype(BF16)
    w_gate_b = moe_w_gate.astype(BF16)
    w_up_b = moe_w_up.astype(BF16)
    w_down_b = moe_w_down.astype(BF16)
    router_pad = jnp.zeros((DEPTH, d, ROUTER_LANES - N_EXPERT_GROUPS - N_EXPERTS), F32)
    w_router = jnp.concatenate([moe_w_group, moe_w_expert, router_pad], axis=-1)
    w_router_hi = w_router.astype(BF16)
    w_router_lo = (w_router - w_router_hi.astype(F32)).astype(BF16)
    w_router = jnp.concatenate([w_router_hi, w_router_lo], axis=-1)
    b_router = jnp.concatenate([moe_b_group, moe_b_expert, router_pad[:, 0, :]], axis=-1)[:, None, :]

    c_pad = jnp.concatenate([c_prompt, c_sample, jnp.zeros((ADA_ROWS - BATCH - DEC_BATCH, d), F32)], axis=0)
    mod_shared = _ada_matmul(c_pad, w_ada)
    mod_tiles = jnp.concatenate(
        [jnp.broadcast_to(mod_shared[:BATCH, None, :], (BATCH, MOD_ROWS, 6 * d)),
         mod_shared[BATCH:BATCH + DEC_BATCH].reshape(SAMPLE_ROW_TILES, MOD_ROWS, 6 * d)], axis=0)
    ada_rows = ada_table.reshape(DEPTH * 6, 1, d)

    x = jnp.concatenate([x_prompt.reshape(N_PROMPT_TOK, d), x_sample.reshape(N_SAMPLE_TOK, d)], axis=0)

    st_sample = (state_ssd, state_ssd_conv, state_gdn, state_gdn_conv, state_mlstm_c,
                 state_mlstm_n, state_mlstm_m, state_rglru, state_rglru_conv)
    kst_sample = _kernel_states(st_sample)
    kst_prompt = _kernel_states(tuple(jnp.zeros((1,) + s, F32) for s in _state_shapes(BATCH)))
    stacked_p = stacked_s = None

    y_prev = None
    for l in range(DEPTH):
        if l == 0:
            (h,) = _rowwise(x, norm1_g[l][None, :], mod_tiles=mod_tiles, ada_rows=ada_rows, layer=l,
                            mod_chunks=(1, 0), name="norm1")
        else:
            x, h = _rowwise(x, norm1_g[l][None, :], y=y_prev, mod_tiles=mod_tiles, ada_rows=ada_rows,
                            layer=l, gate_chunk=5, gate_layer=l - 1, mod_chunks=(1, 0), name="resid_norm1")
        prm = dict(ssd=_ssd_params(ssd_conv_w[l], ssd_conv_b[l], ssd_dt_bias[l], ssd_a_log[l], ssd_d[l],
                                   ssd_norm_g[l]),
                   gdn=_gdn_params(gdn_conv_w[l], gdn_conv_b[l], gdn_dt_bias[l], gdn_a_log[l], gdn_norm_g[l]),
                   mlstm=_mlstm_params(mlstm_i_bias[l], mlstm_f_bias[l], mlstm_norm_g[l]),
                   lru=_lru_params(lru_conv_w[l], lru_conv_b[l], lru_w_a[l], lru_b_a[l], lru_w_x[l], lru_b_x[l],
                                   lru_lambda[l]))
        ys_p, stacked_p = _mixer_group(h, w_in_wide, w_in_narrow, l, kst_prompt, prm, stacked_p, n_seq=BATCH,
                                       seq_len=SEQ, row0=0, g=PROMPT_SEQS_PER_STEP, layer=0)
        ys_s, stacked_s = _mixer_group(h, w_in_wide, w_in_narrow, l, kst_sample, prm, stacked_s, n_seq=DEC_BATCH,
                                       seq_len=DEC_SEQ, row0=N_PROMPT_TOK, g=SAMPLE_SEQS_PER_STEP, layer=l)
        y_mix = _out_proj(ys_p, ys_s, w_out_b, l)

        x, h2, gates = _rowwise(x, norm2_g[l][None, :], y=y_mix, mod_tiles=mod_tiles, ada_rows=ada_rows,
                                layer=l, gate_chunk=2, mod_chunks=(4, 3),
                                router=(w_router[l], b_router[l]), name="resid_norm2_router")
        src, valid, dest, tile_ea, tile_eb, n_used = _moe_plan(gates)
        x_sorted = _take_rows(h2, src)
        gates_sorted = _take_rows(gates, src) * valid[:, None]
        y_sorted = _moe(x_sorted, gates_sorted, tile_ea, tile_eb, n_used, w_gate_b, w_up_b, w_down_b, l)
        y_prev = _take_rows(y_sorted, dest)

    x, y_fin = _rowwise(x, final_g[None, :], y=y_prev, mod_tiles=mod_tiles, ada_rows=ada_rows,
                        layer=DEPTH - 1, gate_chunk=5, h_dtype=F32, name="resid_final_norm")
    outs = [y_fin[:N_PROMPT_TOK].reshape(BATCH, SEQ, d), y_fin[N_PROMPT_TOK:].reshape(DEC_BATCH, DEC_SEQ, d)]
    for st_p, st_s in zip(_reference_states(stacked_p), _reference_states(stacked_s)):
        outs.append(st_p)
        outs.append(st_s)
    return tuple(outs)
```

```python
import functools

import numpy as np
import jax
import jax.numpy as jnp
from jax import lax
from jax.experimental import pallas as pl
from jax.experimental.pallas import tpu as pltpu

F32 = jnp.float32
BF16 = jnp.bfloat16

D_MODEL = 4096
BATCH = 4
SEQ = 2048
DEPTH = 4
DEC_BATCH = 128
DEC_SEQ = 8
EPS = 1e-6
N_MIXERS = 4
D_GROUP = D_MODEL // N_MIXERS
CONV_W = 4
CHUNK = 64
SSD_HEAD_DIM = 64
SSD_HEADS = D_GROUP // SSD_HEAD_DIM
SSD_NGROUPS = 2
SSD_STATE = 128
SSD_BC_DIM = 2 * SSD_NGROUPS * SSD_STATE
SSD_CONV_DIM = D_GROUP + SSD_BC_DIM
GDN_HEAD_DIM = 128
GDN_HEADS = D_GROUP // GDN_HEAD_DIM
GDN_CONV_DIM = 3 * D_GROUP
MLSTM_HEAD_DIM = 128
MLSTM_HEADS = D_GROUP // MLSTM_HEAD_DIM
LRU_BLOCKS = 8
LRU_BLOCK = D_GROUP // LRU_BLOCKS
LRU_C = 8.0
N_EXPERT_GROUPS = 4
EXPERTS_PER_GROUP = 4
N_EXPERTS = N_EXPERT_GROUPS * EXPERTS_PER_GROUP
D_FF_EXPERT = D_MODEL // 8
IN_SIZES = (D_GROUP, SSD_CONV_DIM, SSD_HEADS,
            GDN_CONV_DIM, GDN_HEADS, GDN_HEADS, D_GROUP,
            3 * D_GROUP, MLSTM_HEADS, MLSTM_HEADS, D_GROUP,
            D_GROUP, D_GROUP)
IN_OFFSETS = tuple(int(o) for o in np.cumsum((0,) + IN_SIZES))
(SEG_SSD_Z, SEG_SSD_XBC, SEG_SSD_DT, SEG_GDN_QKV, SEG_GDN_A, SEG_GDN_B, SEG_GDN_Z,
 SEG_ML_QKV, SEG_ML_I, SEG_ML_F, SEG_ML_O, SEG_LRU_X, SEG_LRU_GATE) = range(len(IN_SIZES))

N_PROMPT_TOK = BATCH * SEQ
N_SAMPLE_TOK = DEC_BATCH * DEC_SEQ
N_TOK = N_PROMPT_TOK + N_SAMPLE_TOK

LANES = 128
SUBLANES = 8
VMEM_LIMIT_BYTES = 56 * 1024 * 1024

ROW_TILE = 256
MOD_REPEAT = DEC_SEQ
MOD_ROWS = ROW_TILE // MOD_REPEAT
N_ROW_TILES = N_TOK // ROW_TILE
PROMPT_TILES_PER_SEQ = SEQ // ROW_TILE
PROMPT_ROW_TILES = N_PROMPT_TOK // ROW_TILE
SAMPLE_ROW_TILES = N_SAMPLE_TOK // ROW_TILE
MM_TM = 1024
MM_TN = 512
ADA_ROWS = 144
ADA_TN = 512
MOE_TM = 512
MOE_FF_TILE = 256
MOE_FF_STEPS = D_FF_EXPERT // MOE_FF_TILE
TOP_K_IN_GROUP = 2
PAIRS = tuple((a, b) for a in range(EXPERTS_PER_GROUP) for b in range(a + 1, EXPERTS_PER_GROUP))
N_PAIRS = len(PAIRS)
N_BUCKETS = N_EXPERT_GROUPS * N_PAIRS
MOE_STEPS_PER_TILE = TOP_K_IN_GROUP * MOE_FF_STEPS
MOE_MAX_TILES = N_TOK // MOE_TM + N_BUCKETS
MOE_SLOTS = MOE_MAX_TILES * MOE_TM
ROUTER_LANES = LANES
EXPERT_LANE0 = N_EXPERT_GROUPS
BUCKET_LANE = 0

_WIDE_SEGS = (SEG_GDN_QKV, SEG_ML_QKV, SEG_SSD_Z, SEG_GDN_Z, SEG_ML_O, SEG_LRU_X, SEG_LRU_GATE, SEG_SSD_XBC)
_NARROW_SEGS = (SEG_SSD_DT, SEG_GDN_A, SEG_GDN_B, SEG_ML_I, SEG_ML_F)
WIDE_OFF = {}
_o = 0
for _s in _WIDE_SEGS:
    WIDE_OFF[_s] = _o
    _o += IN_SIZES[_s]
WIDE_COLS = _o
NARROW_OFF = {}
_o = 0
for _s in _NARROW_SEGS:
    NARROW_OFF[_s] = _o
    _o += IN_SIZES[_s]
NARROW_COLS = LANES
LANE_SSD_DT = NARROW_OFF[SEG_SSD_DT]
LANE_GDN_A = NARROW_OFF[SEG_GDN_A]
LANE_GDN_B = NARROW_OFF[SEG_GDN_B]
LANE_ML_I = NARROW_OFF[SEG_ML_I]
LANE_ML_F = NARROW_OFF[SEG_ML_F]

PROMPT_SEQS_PER_STEP = 4
SAMPLE_SEQS_PER_STEP = 8
CONV_PAD = SUBLANES
CONV_PREV = CONV_W - 1


def _cparams(*sem):
    return pltpu.CompilerParams(dimension_semantics=sem, vmem_limit_bytes=VMEM_LIMIT_BYTES)


def _dot(a, b):
    return jnp.dot(a.astype(BF16), b.astype(BF16), preferred_element_type=F32)


def _dot_nt(a, b):
    return lax.dot_general(a.astype(BF16), b.astype(BF16), (((1,), (1,)), ((), ())), preferred_element_type=F32)


def _dot_tn(a, b):
    return lax.dot_general(a.astype(BF16), b.astype(BF16), (((0,), (0,)), ((), ())), preferred_element_type=F32)


def _dot_f32(a, b):
    return jnp.dot(a, b, preferred_element_type=F32, precision=lax.Precision.HIGHEST)


def _silu(x):
    return x * jax.nn.sigmoid(x)


def _mm_kernel(a_ref, w_ref, o_ref):
    o_ref[...] = jnp.dot(a_ref[...], w_ref[...], preferred_element_type=F32).astype(o_ref.dtype)


def _matmul(a, w, layer, tn, name, row0, m):
    k = a.shape[1]
    n = w.shape[2]
    assert m % MM_TM == 0 and row0 % MM_TM == 0 and n % tn == 0
    tile0 = row0 // MM_TM
    return pl.pallas_call(
        _mm_kernel,
        grid=(m // MM_TM, n // tn),
        in_specs=[pl.BlockSpec((MM_TM, k), lambda i, j: (tile0 + i, 0)),
                  pl.BlockSpec((None, k, tn), lambda i, j: (layer, 0, j))],
        out_specs=pl.BlockSpec((MM_TM, tn), lambda i, j: (i, j)),
        out_shape=jax.ShapeDtypeStruct((m, n), F32),
        compiler_params=_cparams("parallel", "arbitrary"),
        name=name,
    )(a, w)


PROMPT_MM_TILES = N_PROMPT_TOK // MM_TM
assert N_SAMPLE_TOK == MM_TM


def _out_proj_kernel(*refs):
    yp_refs, ys_refs = refs[:N_MIXERS], refs[N_MIXERS:2 * N_MIXERS]
    w_refs, o_ref = refs[2 * N_MIXERS:3 * N_MIXERS], refs[3 * N_MIXERS]

    def project(y_refs):
        acc = jnp.dot(y_refs[0][...], w_refs[0][...], preferred_element_type=F32)
        for m in range(1, N_MIXERS):
            acc += jnp.dot(y_refs[m][...], w_refs[m][...], preferred_element_type=F32)
        o_ref[...] = acc

    is_prompt = pl.program_id(0) < PROMPT_MM_TILES
    pl.when(is_prompt)(lambda: project(yp_refs))
    pl.when(jnp.logical_not(is_prompt))(lambda: project(ys_refs))


def _out_proj(ys_prompt, ys_sample, w, layer):
    n = w.shape[2]
    yp_spec = pl.BlockSpec((MM_TM, D_GROUP), lambda i, j: (jnp.minimum(i, PROMPT_MM_TILES - 1), 0))
    ys_spec = pl.BlockSpec((MM_TM, D_GROUP), lambda i, j: (0, 0))
    w_specs = [pl.BlockSpec((None, D_GROUP, MM_TN), functools.partial(lambda i, j, m: (layer, m, j), m=m))
               for m in range(N_MIXERS)]
    return pl.pallas_call(
        _out_proj_kernel,
        grid=(N_TOK // MM_TM, n // MM_TN),
        in_specs=[yp_spec] * N_MIXERS + [ys_spec] * N_MIXERS + w_specs,
        out_specs=pl.BlockSpec((MM_TM, MM_TN), lambda i, j: (i, j)),
        out_shape=jax.ShapeDtypeStruct((N_TOK, n), F32),
        compiler_params=_cparams("parallel", "arbitrary"),
        name="out_proj",
    )(*ys_prompt, *ys_sample, w, w, w, w)


W_IN_ROW_TILE = 256


def _w_in_relayout_kernel(wt_ref, wide_ref, narrow_ref):
    for s in _WIDE_SEGS:
        wide_ref[:, WIDE_OFF[s]:WIDE_OFF[s] + IN_SIZES[s]] = (
            wt_ref[IN_OFFSETS[s]:IN_OFFSETS[s + 1], :].T.astype(BF16))
    rows = [wt_ref[IN_OFFSETS[s]:IN_OFFSETS[s + 1], :] for s in _NARROW_SEGS]
    used = sum(IN_SIZES[s] for s in _NARROW_SEGS)
    rows.append(jnp.zeros((NARROW_COLS - used, wt_ref.shape[1]), F32))
    narrow_ref[...] = jnp.concatenate(rows, axis=0).T.astype(BF16)


def _w_in_relayout(w_in):
    wt = jnp.swapaxes(w_in, 1, 2)
    depth, n, k = wt.shape
    return pl.pallas_call(
        _w_in_relayout_kernel,
        grid=(depth, k // W_IN_ROW_TILE),
        in_specs=[pl.BlockSpec((None, n, W_IN_ROW_TILE), lambda l, i: (l, 0, i))],
        out_specs=[pl.BlockSpec((None, W_IN_ROW_TILE, WIDE_COLS), lambda l, i: (l, i, 0)),
                   pl.BlockSpec((None, W_IN_ROW_TILE, NARROW_COLS), lambda l, i: (l, i, 0))],
        out_shape=[jax.ShapeDtypeStruct((depth, k, WIDE_COLS), BF16),
                   jax.ShapeDtypeStruct((depth, k, NARROW_COLS), BF16)],
        compiler_params=_cparams("parallel", "parallel"),
        name="w_in_relayout",
    )(wt)


def _ada_kernel(c_ref, w_ref, o_ref):
    o_ref[...] = _dot(_silu(c_ref[...]), w_ref[...])


def _ada_matmul(c_pad, w_ada):
    k, n = w_ada.shape
    return pl.pallas_call(
        _ada_kernel,
        grid=(n // ADA_TN,),
        in_specs=[pl.BlockSpec((ADA_ROWS, k), lambda j: (0, 0)),
                  pl.BlockSpec((k, ADA_TN), lambda j: (0, j))],
        out_specs=pl.BlockSpec((ADA_ROWS, ADA_TN), lambda j: (0, j)),
        out_shape=jax.ShapeDtypeStruct((ADA_ROWS, n), F32),
        compiler_params=_cparams("arbitrary"),
        name="adaln_matmul",
    )(c_pad, w_ada)


def _route(lg):
    lane = lax.broadcasted_iota(jnp.int32, lg.shape, 1).astype(F32)
    neg = -jnp.inf
    big = float(ROUTER_LANES)
    gl = jnp.where(lane < N_EXPERT_GROUPS, lg, neg)
    gmax = jnp.max(gl, axis=-1, keepdims=True)
    gidx = jnp.min(jnp.where(gl == gmax, lane, big), axis=-1, keepdims=True)
    g_w = 1.0 / jnp.sum(jnp.exp(gl - gmax), axis=-1, keepdims=True)
    lo = EXPERT_LANE0 + EXPERTS_PER_GROUP * gidx
    el = jnp.where((lane >= lo) & (lane < lo + EXPERTS_PER_GROUP), lg, neg)
    e1 = jnp.max(el, axis=-1, keepdims=True)
    i1 = jnp.min(jnp.where(el == e1, lane, big), axis=-1, keepdims=True)
    el2 = jnp.where(lane == i1, neg, el)
    e2 = jnp.max(el2, axis=-1, keepdims=True)
    i2 = jnp.min(jnp.where(el2 == e2, lane, big), axis=-1, keepdims=True)
    t = jnp.exp(e2 - e1)
    w1 = g_w / (1.0 + t)
    w2 = g_w * t / (1.0 + t)
    out = jnp.where(lane == i1, w1, 0.0) + jnp.where(lane == i2, w2, 0.0)
    pa = jnp.minimum(i1, i2) - lo
    pb = jnp.maximum(i1, i2) - lo
    pair = pa * (2 * EXPERTS_PER_GROUP - 1 - pa) * 0.5 + (pb - pa - 1.0)
    return out + jnp.where(lane == float(BUCKET_LANE), gidx * N_PAIRS + pair, 0.0)


def _rowwise_kernel(*refs, has_resid, has_mod, has_router):
    it = iter(refs)
    x_ref = next(it)
    if has_resid:
        y_ref, gate_m, gate_t = next(it), next(it), next(it)
    g_ref = next(it)
    if has_mod:
        sc_m, sc_t, sh_m, sh_t = next(it), next(it), next(it), next(it)
    if has_router:
        wr_ref, br_ref = next(it), next(it)
    if has_resid:
        xo_ref = next(it)
    h_ref = next(it)
    if has_router:
        gt_ref = next(it)

    def mod(m_ref, t_ref):
        m = m_ref[...] + t_ref[...]
        return jnp.broadcast_to(m[:, None, :], (MOD_ROWS, MOD_REPEAT, m.shape[-1])).reshape(ROW_TILE, m.shape[-1])

    x = x_ref[...]
    if has_resid:
        x = x + mod(gate_m, gate_t) * y_ref[...]
        xo_ref[...] = x
    y = x * lax.rsqrt(jnp.mean(x * x, axis=-1, keepdims=True) + EPS)
    h = y * g_ref[...]
    if has_mod:
        h = h * (1.0 + mod(sc_m, sc_t)) + mod(sh_m, sh_t)
    h_ref[...] = h.astype(h_ref.dtype)
    if has_router:
        h_hi = h.astype(BF16)
        h_lo = (h - h_hi.astype(F32)).astype(BF16)
        hi_terms = jnp.dot(h_hi, wr_ref[...], preferred_element_type=F32)
        lg = (hi_terms[:, :ROUTER_LANES] + hi_terms[:, ROUTER_LANES:]
              + jnp.dot(h_lo, wr_ref[:, :ROUTER_LANES], preferred_element_type=F32))
        gt_ref[...] = _route(lg + br_ref[...])


def _mod_tile_index(i):
    return jnp.where(i < PROMPT_ROW_TILES, i // PROMPT_TILES_PER_SEQ, BATCH + i - PROMPT_ROW_TILES)


def _rowwise(x, g_row, *, y=None, mod_tiles=None, ada_rows=None, layer=0, gate_chunk=None, gate_layer=None,
             mod_chunks=None, router=None, h_dtype=BF16, name="rowwise"):
    has_resid = y is not None
    has_mod = mod_chunks is not None
    has_router = router is not None
    d = D_MODEL
    row_spec = pl.BlockSpec((ROW_TILE, d), lambda i: (i, 0))
    vec_spec = pl.BlockSpec((1, d), lambda i: (0, 0))

    def mod_specs(chunk, lyr):
        return [pl.BlockSpec((None, MOD_ROWS, d), lambda i: (_mod_tile_index(i), 0, chunk)),
                pl.BlockSpec((None, 1, d), lambda i: (lyr * 6 + chunk, 0, 0))]

    args, specs = [x], [row_spec]
    if has_resid:
        args += [y, mod_tiles, ada_rows]
        specs += [row_spec] + mod_specs(gate_chunk, layer if gate_layer is None else gate_layer)
    args.append(g_row)
    specs.append(vec_spec)
    if has_mod:
        sc_chunk, sh_chunk = mod_chunks
        args += [mod_tiles, ada_rows, mod_tiles, ada_rows]
        specs += mod_specs(sc_chunk, layer) + mod_specs(sh_chunk, layer)
    if has_router:
        w_r, b_r = router
        args += [w_r, b_r]
        specs += [pl.BlockSpec((d, 2 * ROUTER_LANES), lambda i: (0, 0)),
                  pl.BlockSpec((1, ROUTER_LANES), lambda i: (0, 0))]
    out_shape, out_specs = [], []
    if has_resid:
        out_shape.append(jax.ShapeDtypeStruct((N_TOK, d), F32))
        out_specs.append(row_spec)
    out_shape.append(jax.ShapeDtypeStruct((N_TOK, d), h_dtype))
    out_specs.append(row_spec)
    if has_router:
        out_shape.append(jax.ShapeDtypeStruct((N_TOK, ROUTER_LANES), F32))
        out_specs.append(pl.BlockSpec((ROW_TILE, ROUTER_LANES), lambda i: (i, 0)))
    return pl.pallas_call(
        functools.partial(_rowwise_kernel, has_resid=has_resid, has_mod=has_mod, has_router=has_router),
        grid=(N_ROW_TILES,),
        in_specs=specs,
        out_specs=out_specs,
        out_shape=out_shape,
        compiler_params=_cparams("parallel"),
        name=name,
    )(*args)


def _moe_kernel(ea_ref, eb_ref, nu_ref, x_ref, gt_ref, wg_ref, wu_ref, wd_ref, o_ref):
    i = pl.program_id(0)
    j = pl.program_id(1)
    used = i < nu_ref[0]

    @pl.when(j == 0)
    def _():
        o_ref[...] = jnp.zeros_like(o_ref)

    @pl.when(used)
    def _():
        x = x_ref[...]
        a = jnp.dot(x, wg_ref[...], preferred_element_type=F32)
        b = jnp.dot(x, wu_ref[...], preferred_element_type=F32)
        he = _silu(a) * b
        gates = gt_ref[...]
        lane = lax.broadcasted_iota(jnp.int32, gates.shape, 1)
        col = EXPERT_LANE0 + jnp.where(j < MOE_FF_STEPS, ea_ref[i], eb_ref[i])
        gcol = jnp.sum(jnp.where(lane == col, gates, 0.0), axis=-1, keepdims=True)
        o_ref[...] += gcol * jnp.dot(he.astype(BF16), wd_ref[...].astype(BF16), preferred_element_type=F32)


def _moe(x_sorted, gates_sorted, tile_ea, tile_eb, n_used, w_gate, w_up, w_down, layer):
    d = D_MODEL

    def expert_of(i, j, ea, eb, nu):
        jj = jnp.where(i < nu[0], j, MOE_STEPS_PER_TILE - 1)
        return jnp.where(jj < MOE_FF_STEPS, ea[i], eb[i]), jj % MOE_FF_STEPS

    def in_row_map(i, j, ea, eb, nu):
        return (jnp.minimum(i, jnp.minimum(nu[0], MOE_MAX_TILES - 1)), 0)

    def wgu_map(i, j, ea, eb, nu):
        e, f = expert_of(i, j, ea, eb, nu)
        return (layer, e, 0, f)

    def wd_map(i, j, ea, eb, nu):
        e, f = expert_of(i, j, ea, eb, nu)
        return (layer, e, f, 0)

    grid_spec = pltpu.PrefetchScalarGridSpec(
        num_scalar_prefetch=3,
        grid=(MOE_MAX_TILES, MOE_STEPS_PER_TILE),
        in_specs=[pl.BlockSpec((MOE_TM, d), in_row_map),
                  pl.BlockSpec((MOE_TM, ROUTER_LANES), in_row_map),
                  pl.BlockSpec((None, None, d, MOE_FF_TILE), wgu_map),
                  pl.BlockSpec((None, None, d, MOE_FF_TILE), wgu_map),
                  pl.BlockSpec((None, None, MOE_FF_TILE, d), wd_map)],
        out_specs=pl.BlockSpec((MOE_TM, d), lambda i, j, ea, eb, nu: (i, 0)),
    )
    return pl.pallas_call(
        _moe_kernel,
        grid_spec=grid_spec,
        out_shape=jax.ShapeDtypeStruct((MOE_SLOTS, d), F32),
        compiler_params=_cparams("arbitrary", "arbitrary"),
        name="moe_experts",
    )(tile_ea, tile_eb, n_used, x_sorted, gates_sorted, w_gate, w_up, w_down)


def _take_rows(a, idx):
    return a.at[idx].get(mode="promise_in_bounds")


def _moe_plan(gates):
    bid = gates[:, BUCKET_LANE].astype(jnp.int32)
    onehot = (bid[:, None] == jnp.arange(N_BUCKETS, dtype=jnp.int32)[None, :]).astype(jnp.int32)
    rank = jnp.take_along_axis(jnp.cumsum(onehot, axis=0), bid[:, None], axis=1)[:, 0] - 1
    counts = jnp.sum(onehot, axis=0)
    tiles = (counts + MOE_TM - 1) // MOE_TM
    tile_end = jnp.cumsum(tiles)
    tile_start = tile_end - tiles
    dest = tile_start[bid] * MOE_TM + rank
    src = (jnp.arange(MOE_SLOTS, dtype=jnp.int32) % N_TOK).at[dest].set(jnp.arange(N_TOK, dtype=jnp.int32))
    n_used = tile_end[-1]
    t = jnp.arange(MOE_MAX_TILES, dtype=jnp.int32)
    tile_bucket = jnp.sum((jnp.minimum(t, n_used - 1)[:, None] >= tile_end[None, :]).astype(jnp.int32), axis=1)
    slot = jnp.arange(MOE_SLOTS, dtype=jnp.int32)
    slot_bucket = jnp.repeat(tile_bucket, MOE_TM)
    valid = ((slot - tile_start[slot_bucket] * MOE_TM < counts[slot_bucket])
             & (slot < n_used * MOE_TM)).astype(F32)
    group0 = (tile_bucket // N_PAIRS) * EXPERTS_PER_GROUP
    pair = tile_bucket % N_PAIRS
    tile_ea = group0 + jnp.asarray([p[0] for p in PAIRS], jnp.int32)[pair]
    tile_eb = group0 + jnp.asarray([p[1] for p in PAIRS], jnp.int32)[pair]
    return src, valid, dest, tile_ea, tile_eb, n_used.reshape(1).astype(jnp.int32)


def _conv_init(win_scr, c0_ref):
    win_scr[0:CONV_PAD, :] = jnp.zeros((CONV_PAD, win_scr.shape[1]), F32)
    win_scr[CONV_PAD - CONV_PREV:CONV_PAD, :] = c0_ref[...]


def _conv_step(u_ref, win_scr, w_ref, b_ref, cs):
    u = u_ref[...]
    prev = win_scr[0:CONV_PAD, :]
    row = lax.broadcasted_iota(jnp.int32, (CONV_PAD, 1), 0)
    out = b_ref[...]
    for j in range(CONV_W):
        shift = CONV_PREV - j
        if shift == 0:
            tap = u
        else:
            rolled = pltpu.roll(u, shift, 0)
            head = jnp.where(row >= shift, rolled[0:CONV_PAD, :], pltpu.roll(prev, shift, 0))
            tap = head if cs == CONV_PAD else jnp.concatenate([head, rolled[CONV_PAD:, :]], axis=0)
        out = out + tap * w_ref[j:j + 1, :]
    win_scr[CONV_PAD:2 * CONV_PAD, :] = u[cs - CONV_PAD:cs, :]
    return out


def _conv_advance(win_scr, cs):
    win_scr[0:CONV_PAD, :] = win_scr[CONV_PAD:2 * CONV_PAD, :]


def _causal_masks(cs):
    row = lax.broadcasted_iota(jnp.int32, (cs, cs), 0)
    col = lax.broadcasted_iota(jnp.int32, (cs, cs), 1)
    return row >= col, row > col


def _cumsum_rows(x):
    cs = x.shape[0]
    row = lax.broadcasted_iota(jnp.int32, x.shape, 0)
    shift = 1
    while shift < cs:
        x = x + jnp.where(row >= shift, pltpu.roll(x, shift, 0), 0.0)
        shift *= 2
    return x


def _rms(x, width):
    return x * lax.rsqrt(jnp.sum(x * x, axis=-1, keepdims=True) * (1.0 / width) + EPS)


def _seq_specs(cs, layer, g):
    def rows(width, col):
        return pl.BlockSpec((g, cs, width), lambda b, c: (b, c, col))

    def const(shape):
        nd = len(shape)
        return pl.BlockSpec(shape, lambda b, c: (0,) * nd)

    def per_seq(shape, last=0):
        nd = len(shape)
        return pl.BlockSpec((g,) + shape, lambda b, c: (b,) + (0,) * (nd - 1) + (last,))

    def state_in(shape, last=0):
        nd = len(shape)
        return pl.BlockSpec((None, g) + shape, lambda b, c: (layer, b) + (0,) * (nd - 1) + (last,))

    return rows, const, per_seq, state_in


def _seq_batched_kernel(*refs, body, per_seq_refs, alias_range, g, cs, nc):
    refs = refs[:alias_range[0]] + refs[alias_range[1]:]
    c = pl.program_id(1)
    views = [tuple(r.at[s] if flag else r for r, flag in zip(refs, per_seq_refs)) for s in range(g)]

    @pl.when(c == 0)
    def _():
        for v in views:
            body(*v, cs=cs, phase="init")

    for v in views:
        body(*v, cs=cs, phase="main")

    @pl.when(c == nc - 1)
    def _():
        for v in views:
            body(*v, cs=cs, phase="final")


def _seq_call(body, name, in_arrays, in_specs, n_const, state_shapes, scratch, prev, *,
              n_seq, seq_len, cs, g, out_layer):
    assert n_seq % g == 0
    nc = seq_len // cs
    n_out = 1 + len(state_shapes)
    flags = ([True] * (len(in_arrays) - n_const) + [False] * n_const + [True] * (n_out + len(scratch)))
    out_specs = [pl.BlockSpec((g, cs, D_GROUP), lambda b, c: (b, c, 0))]
    out_shape = [jax.ShapeDtypeStruct((n_seq, seq_len, D_GROUP), BF16)]
    for shape in state_shapes:
        nd = len(shape)
        out_specs.append(pl.BlockSpec((None, g) + shape, lambda b, c, nd=nd: (out_layer, b) + (0,) * nd))
        out_shape.append(jax.ShapeDtypeStruct((DEPTH, n_seq) + shape, F32))
    n_in = len(in_arrays)
    aliases = {}
    if prev is not None:
        in_arrays = tuple(in_arrays) + tuple(prev)
        in_specs = list(in_specs) + [pl.BlockSpec(memory_space=pl.ANY)] * len(prev)
        aliases = {n_in + k: 1 + k for k in range(len(prev))}
    return pl.pallas_call(
        functools.partial(_seq_batched_kernel, body=body, per_seq_refs=tuple(flags),
                          alias_range=(n_in, len(in_arrays)), g=g, cs=cs, nc=nc),
        grid=(n_seq // g, nc),
        in_specs=in_specs,
        out_specs=out_specs,
        out_shape=out_shape,
        scratch_shapes=[pltpu.VMEM((g,) + shape, F32) for shape in scratch],
        input_output_aliases=aliases,
        compiler_params=_cparams("parallel", "arbitrary"),
        name=name,
    )(*in_arrays)


def _ssd_kernel(z_ref, x_ref, bc_ref, nar_ref, s0_ref, c0x_ref, c0bc_ref,
                cwx_ref, cbx_ref, cwbc_ref, cbbc_ref, bias_ref, alog_ref, dsk_ref, ng_ref,
                y_ref, s_out_ref, cx_out_ref, cbc_out_ref,
                st_scr, wx_scr, wbc_scr, ycat_scr, *, cs, phase):
    lo = CONV_PAD - CONV_PREV
    if phase == "init":
        st_scr[...] = s0_ref[...]
        _conv_init(wx_scr, c0x_ref)
        _conv_init(wbc_scr, c0bc_ref)
        return
    if phase == "final":
        s_out_ref[...] = st_scr[...]
        cx_out_ref[...] = wx_scr[lo:CONV_PAD, :]
        cbc_out_ref[...] = wbc_scr[lo:CONV_PAD, :]
        return

    xs = _silu(_conv_step(x_ref, wx_scr, cwx_ref, cbx_ref, cs))
    bcs = _silu(_conv_step(bc_ref, wbc_scr, cwbc_ref, cbbc_ref, cs))
    incl, _ = _causal_masks(cs)
    dt = jax.nn.softplus(nar_ref[...] + bias_ref[...])
    cum = _cumsum_rows(dt * (-jnp.exp(alog_ref[...])))
    cum_t = cum.T
    exp_cum = jnp.exp(cum)
    cum_last = cum[cs - 1:cs, :]
    w_end = jnp.exp(cum_last - cum)
    chunk_decay = jnp.exp(cum_last)
    heads_per_group = SSD_HEADS // SSD_NGROUPS
    heads = range(SSD_HEADS)
    hp = SSD_HEAD_DIM
    s_old = [st_scr[h] for h in heads]
    col = lambda arr, h: arr[:, LANE_SSD_DT + h:LANE_SSD_DT + h + 1]
    b_gs = [bcs[:, g * SSD_STATE:(g + 1) * SSD_STATE] for g in range(SSD_NGROUPS)]
    c_gs = [bcs[:, (SSD_NGROUPS + g) * SSD_STATE:(SSD_NGROUPS + g + 1) * SSD_STATE] for g in range(SSD_NGROUPS)]
    cbs = [_dot_nt(c_gs[g], b_gs[g]) for g in range(SSD_NGROUPS)]
    grp = [h // heads_per_group for h in heads]
    scores = [cbs[grp[h]] * jnp.exp(jnp.where(incl, col(cum, h) - cum_t[LANE_SSD_DT + h:LANE_SSD_DT + h + 1, :],
                                              -jnp.inf)) for h in heads]
    xdts = [xs[:, h * hp:(h + 1) * hp] * col(dt, h) for h in heads]
    ys = [_dot(scores[h], xdts[h]) + _dot_nt(c_gs[grp[h]] * col(exp_cum, h), s_old[h]) for h in heads]
    s_new = [s_old[h] * col(chunk_decay, h) + _dot_tn(xdts[h] * col(w_end, h), b_gs[grp[h]]) for h in heads]
    for h in heads:
        st_scr[h] = s_new[h]
        ycat_scr[:, h * hp:(h + 1) * hp] = ys[h]
    y = (ycat_scr[...] + dsk_ref[...] * xs) * _silu(z_ref[...])
    gw = D_GROUP // SSD_NGROUPS
    for g in range(SSD_NGROUPS):
        y_g = _rms(y[:, g * gw:(g + 1) * gw], gw) * ng_ref[:, g * gw:(g + 1) * gw]
        y_ref[:, g * gw:(g + 1) * gw] = y_g.astype(y_ref.dtype)
    _conv_advance(wx_scr, cs)
    _conv_advance(wbc_scr, cs)


def _ssd_call(wide, narrow, s0, conv0, prm, prev, *, n_seq, seq_len, cs, g, layer, out_layer):
    rows, const, per_seq, state_in = _seq_specs(cs, layer, g)
    xw, bcw = D_GROUP, SSD_BC_DIM
    x_off = WIDE_OFF[SEG_SSD_XBC]
    in_specs = [rows(xw, WIDE_OFF[SEG_SSD_Z] // xw), rows(xw, x_off // xw), rows(bcw, (x_off + xw) // bcw),
                rows(NARROW_COLS, 0),
                state_in((SSD_HEADS, SSD_HEAD_DIM, SSD_STATE)),
                state_in((CONV_PREV, xw)), state_in((CONV_PREV, bcw), last=xw // bcw),
                const((CONV_W, xw)), const((1, xw)), const((CONV_W, bcw)), const((1, bcw)),
                const((1, LANES)), const((1, LANES)), const((1, xw)), const((1, xw))]
    state_shapes = [(SSD_HEADS, SSD_HEAD_DIM, SSD_STATE), (CONV_PREV, xw), (CONV_PREV, bcw)]
    scratch = [(SSD_HEADS, SSD_HEAD_DIM, SSD_STATE), (CONV_PAD + cs, xw), (CONV_PAD + cs, bcw), (cs, xw)]
    in_arrays = (wide, wide, wide, narrow, s0, conv0, conv0,
                 prm['cw'][:, :xw], prm['cb'][:, :xw], prm['cw'][:, xw:], prm['cb'][:, xw:],
                 prm['bias'], prm['alog'], prm['dskip'], prm['ng'])
    return _seq_call(_ssd_kernel, "ssd_mixer", in_arrays, in_specs, 8, state_shapes, scratch, prev,
                     n_seq=n_seq, seq_len=seq_len, cs=cs, g=g, out_layer=out_layer)


def _ssd_params(conv_w, conv_b, dt_bias, a_log, d_skip, norm_g):
    pad = jnp.zeros((LANES - SSD_HEADS,), F32)
    return dict(cw=conv_w, cb=conv_b[None, :],
                bias=jnp.concatenate([dt_bias, pad])[None, :],
                alog=jnp.concatenate([a_log, pad])[None, :],
                dskip=jnp.repeat(d_skip, SSD_HEAD_DIM)[None, :], ng=norm_g[None, :])


def _split_hi_lo(x):
    hi = x.astype(BF16).astype(F32)
    lo = (x - hi).astype(BF16).astype(F32)
    return hi, lo


def _lhs3(a):
    hi, lo = _split_hi_lo(a)
    return jnp.concatenate([hi, lo, hi], axis=1).astype(BF16)


def _rhs3(b):
    hi, lo = _split_hi_lo(b)
    return jnp.concatenate([hi, hi, lo], axis=0).astype(BF16)


def _dot3(lhs3, rhs3):
    return jnp.dot(lhs3, rhs3, preferred_element_type=F32)


def _inv_unit_lower(a_list, cs):
    row = lax.broadcasted_iota(jnp.int32, (cs, cs), 0)
    col = lax.broadcasted_iota(jnp.int32, (cs, cs), 1)
    eye = jnp.where(row == col, 1.0, 0.0)
    ps = [-a for a in a_list]
    ts = [eye + p for p in ps]
    forms = [(_lhs3(p), _rhs3(p)) for p in ps]
    n = 1
    while 2 * n < cs:
        ps = [_dot3(lhs, rhs) for lhs, rhs in forms]
        forms = [(_lhs3(p), _rhs3(p)) for p in ps]
        ts = [t + _dot3(_lhs3(t), rhs) for t, (_, rhs) in zip(ts, forms)]
        n *= 2
    return ts


def _gdn_kernel(qkv_ref, z_ref, nar_ref, s0_ref, c0_ref, cw_ref, cb_ref, bias_ref, alog_ref, ng_ref,
                y_ref, s_out_ref, c_out_ref, st_scr, win_scr, *, cs, phase):
    if phase == "init":
        st_scr[...] = s0_ref[...]
        _conv_init(win_scr, c0_ref)
        return
    if phase == "final":
        s_out_ref[...] = st_scr[...]
        c_out_ref[...] = win_scr[CONV_PAD - CONV_PREV:CONV_PAD, :]
        return

    qkv = _silu(_conv_step(qkv_ref, win_scr, cw_ref, cb_ref, cs))
    incl, strict = _causal_masks(cs)
    nar = nar_ref[...]
    g_log = -jnp.exp(alog_ref[...]) * jax.nn.softplus(nar + bias_ref[...])
    beta_all = jax.nn.sigmoid(nar)
    gc = _cumsum_rows(g_log)
    gc_t = gc.T
    exp_gc = jnp.exp(gc)
    gc_last = gc[cs - 1:cs, :]
    exp_to_end = jnp.exp(gc_last - gc)
    g_last = jnp.exp(gc_last)
    hd = GDN_HEAD_DIM
    heads = range(GDN_HEADS)
    s_old = [st_scr[h] for h in heads]
    col = lambda arr, lane: arr[:, lane:lane + 1]
    qs = [qkv[:, h * hd:(h + 1) * hd] for h in heads]
    ks = [qkv[:, D_GROUP + h * hd:D_GROUP + (h + 1) * hd] for h in heads]
    vs = [qkv[:, 2 * D_GROUP + h * hd:2 * D_GROUP + (h + 1) * hd] for h in heads]
    qs = [q * lax.rsqrt(jnp.sum(q * q, axis=-1, keepdims=True) + EPS) * (hd ** -0.5) for q in qs]
    ks = [k * lax.rsqrt(jnp.sum(k * k, axis=-1, keepdims=True) + EPS) for k in ks]
    decays = [jnp.exp(jnp.where(incl, col(gc, LANE_GDN_A + h) - gc_t[LANE_GDN_A + h:LANE_GDN_A + h + 1, :],
                                -jnp.inf)) for h in heads]
    betas = [col(beta_all, LANE_GDN_B + h) for h in heads]
    a_mats = [jnp.where(strict, betas[h] * _dot_nt(ks[h], ks[h]) * decays[h], 0.0) for h in heads]
    t_invs = _inv_unit_lower(a_mats, cs)
    rhs = [jnp.concatenate([vs[h] * betas[h], ks[h] * (betas[h] * col(exp_gc, LANE_GDN_A + h))], axis=1)
           for h in heads]
    uw = [_dot3(_lhs3(t_invs[h]), _rhs3(rhs[h])) for h in heads]
    qks = [_dot_nt(qs[h], ks[h]) * decays[h] for h in heads]
    v_new = [uw[h][:, :hd] - _dot(uw[h][:, hd:], s_old[h]) for h in heads]
    outs = [_dot(qs[h] * col(exp_gc, LANE_GDN_A + h), s_old[h]) + _dot(qks[h], v_new[h]) for h in heads]
    s_new = [s_old[h] * col(g_last, LANE_GDN_A + h) + _dot_tn(ks[h] * col(exp_to_end, LANE_GDN_A + h), v_new[h])
             for h in heads]
    for h in heads:
        st_scr[h] = s_new[h]
    y = jnp.concatenate([_rms(o, hd) for o in outs], axis=1) * ng_ref[...] * _silu(z_ref[...])
    y_ref[...] = y.astype(y_ref.dtype)
    _conv_advance(win_scr, cs)


def _gdn_call(wide, narrow, s0, conv0, prm, prev, *, n_seq, seq_len, cs, g, layer, out_layer):
    rows, const, per_seq, state_in = _seq_specs(cs, layer, g)
    hd = GDN_HEAD_DIM
    in_specs = [rows(GDN_CONV_DIM, WIDE_OFF[SEG_GDN_QKV] // GDN_CONV_DIM),
                rows(D_GROUP, WIDE_OFF[SEG_GDN_Z] // D_GROUP),
                rows(NARROW_COLS, 0),
                state_in((GDN_HEADS, hd, hd)), state_in((CONV_PREV, GDN_CONV_DIM)),
                const((CONV_W, GDN_CONV_DIM)), const((1, GDN_CONV_DIM)),
                const((1, LANES)), const((1, LANES)), const((1, D_GROUP))]
    state_shapes = [(GDN_HEADS, hd, hd), (CONV_PREV, GDN_CONV_DIM)]
    scratch = [(GDN_HEADS, hd, hd), (CONV_PAD + cs, GDN_CONV_DIM)]
    in_arrays = (wide, wide, narrow, s0, conv0, prm['cw'], prm['cb'], prm['bias'], prm['alog'], prm['ng'])
    return _seq_call(_gdn_kernel, "gdn_mixer", in_arrays, in_specs, 5, state_shapes, scratch, prev,
                     n_seq=n_seq, seq_len=seq_len, cs=cs, g=g, out_layer=out_layer)


def _lane_row(vec, lane0):
    return jnp.zeros((LANES,), F32).at[lane0:lane0 + vec.shape[0]].set(vec)[None, :]


def _gdn_params(conv_w, conv_b, dt_bias, a_log, norm_g):
    return dict(cw=conv_w, cb=conv_b[None, :], bias=_lane_row(dt_bias, LANE_GDN_A),
                alog=_lane_row(a_log, LANE_GDN_A), ng=norm_g[None, :])


def _mlstm_kernel(qkv_ref, o_ref, nar_ref, c0_ref, n0_ref, m0_ref, bias_ref, ng_ref,
                  y_ref, c_out_ref, n_out_ref, m_out_ref, c_scr, n_scr, m_scr, *, cs, phase):
    if phase == "init":
        c_scr[...] = c0_ref[...]
        n_scr[...] = n0_ref[...]
        m_scr[...] = m0_ref[...]
        return
    if phase == "final":
        c_out_ref[...] = c_scr[...]
        n_out_ref[...] = n_scr[...]
        m_out_ref[...] = m_scr[...]
        return

    incl, _ = _causal_masks(cs)
    pre = nar_ref[...] + bias_ref[...]
    f_cum = _cumsum_rows(jax.nn.log_sigmoid(pre))
    f_cum_t = f_cum.T
    pre_t = pre.T
    hd = MLSTM_HEAD_DIM
    heads = range(MLSTM_HEADS)
    c_old = [c_scr[h] for h in heads]
    n_all = n_scr[...]
    m_all = m_scr[...]
    qkv = qkv_ref[...]
    col = lambda arr, lane: arr[:, lane:lane + 1]
    qs = [qkv[:, h * hd:(h + 1) * hd] for h in heads]
    ks = [qkv[:, D_GROUP + h * hd:D_GROUP + (h + 1) * hd] * (hd ** -0.5) for h in heads]
    vs = [qkv[:, 2 * D_GROUP + h * hd:2 * D_GROUP + (h + 1) * hd] for h in heads]
    f_cols = [col(f_cum, LANE_ML_F + h) for h in heads]
    d_logs = [jnp.where(incl, f_cols[h] - f_cum_t[LANE_ML_F + h:LANE_ML_F + h + 1, :]
                        + pre_t[LANE_ML_I + h:LANE_ML_I + h + 1, :], -jnp.inf) for h in heads]
    m_prev = [col(m_all, LANE_ML_I + h) for h in heads]
    m_t = [jnp.maximum(f_cols[h] + m_prev[h], jnp.max(d_logs[h], axis=-1, keepdims=True)) for h in heads]
    w_carry = [jnp.exp(f_cols[h] + m_prev[h] - m_t[h]) for h in heads]
    ps = [jnp.exp(d_logs[h] - m_t[h]) * _dot_nt(qs[h], ks[h]) for h in heads]
    num = [w_carry[h] * _dot(qs[h], c_old[h]) + _dot(ps[h], vs[h]) for h in heads]
    den = [w_carry[h] * jnp.sum(qs[h] * n_all[h:h + 1, :], axis=-1, keepdims=True)
           + jnp.sum(ps[h], axis=-1, keepdims=True) for h in heads]
    hs = [num[h] / jnp.maximum(jnp.abs(den[h]), jnp.exp(-m_t[h])) for h in heads]
    m_end = [m_t[h][cs - 1:cs, :] for h in heads]
    f_last = [f_cols[h][cs - 1:cs, :] for h in heads]
    w_prev = [jnp.exp(f_last[h] + m_prev[h] - m_end[h]) for h in heads]
    kws = [ks[h] * jnp.exp(f_last[h] - f_cols[h] + col(pre, LANE_ML_I + h) - m_end[h]) for h in heads]
    c_new = [w_prev[h] * c_old[h] + _dot_tn(kws[h], vs[h]) for h in heads]
    n_new = [w_prev[h] * n_all[h:h + 1, :] + jnp.sum(kws[h], axis=0, keepdims=True) for h in heads]
    lane = lax.broadcasted_iota(jnp.int32, m_all.shape, 1)
    m_new = m_all
    for h in heads:
        c_scr[h] = c_new[h]
        m_new = jnp.where(lane == LANE_ML_I + h, m_end[h], m_new)
    n_scr[...] = jnp.concatenate(n_new, axis=0)
    m_scr[...] = m_new
    y = jnp.concatenate([_rms(hh, hd) for hh in hs], axis=1) * ng_ref[...] * jax.nn.sigmoid(o_ref[...])
    y_ref[...] = y.astype(y_ref.dtype)


def _mlstm_call(wide, narrow, c0, n0, m0, prm, prev, *, n_seq, seq_len, cs, g, layer, out_layer):
    rows, const, per_seq, state_in = _seq_specs(cs, layer, g)
    hd = MLSTM_HEAD_DIM
    qkv_w = 3 * D_GROUP
    in_specs = [rows(qkv_w, WIDE_OFF[SEG_ML_QKV] // qkv_w),
                rows(D_GROUP, WIDE_OFF[SEG_ML_O] // D_GROUP),
                rows(NARROW_COLS, 0),
                state_in((MLSTM_HEADS, hd, hd)), state_in((MLSTM_HEADS, hd)), state_in((1, LANES)),
                const((1, LANES)), const((1, D_GROUP))]
    state_shapes = [(MLSTM_HEADS, hd, hd), (MLSTM_HEADS, hd), (1, LANES)]
    scratch = [(MLSTM_HEADS, hd, hd), (MLSTM_HEADS, hd), (1, LANES)]
    in_arrays = (wide, wide, narrow, c0, n0, m0, prm['bias'], prm['ng'])
    return _seq_call(_mlstm_kernel, "mlstm_mixer", in_arrays, in_specs, 2, state_shapes, scratch, prev,
                     n_seq=n_seq, seq_len=seq_len, cs=cs, g=g, out_layer=out_layer)


def _mlstm_params(i_bias, f_bias, norm_g):
    return dict(bias=_lane_row(i_bias, LANE_ML_I) + _lane_row(f_bias, LANE_ML_F), ng=norm_g[None, :])


def _lru_kernel(x_ref, gate_ref, h0_ref, c0_ref, cw_ref, cb_ref, wa_ref, ba_ref, wx_ref, bx_ref, lam_ref,
                y_ref, h_out_ref, c_out_ref, h_scr, win_scr, *, cs, phase):
    if phase == "init":
        h_scr[...] = h0_ref[...]
        _conv_init(win_scr, c0_ref)
        return
    if phase == "final":
        h_out_ref[...] = h_scr[...]
        c_out_ref[...] = win_scr[CONV_PAD - CONV_PREV:CONV_PAD, :]
        return

    xc = _conv_step(x_ref, win_scr, cw_ref, cb_ref, cs)
    r_parts, i_parts = [], []
    for g in range(LRU_BLOCKS):
        x_g = xc[:, g * LRU_BLOCK:(g + 1) * LRU_BLOCK].astype(BF16)
        r_parts.append(jnp.dot(x_g, wa_ref[g], preferred_element_type=F32))
        i_parts.append(jnp.dot(x_g, wx_ref[g], preferred_element_type=F32))
    r = jax.nn.sigmoid(jnp.concatenate(r_parts, axis=-1) + ba_ref[...])
    i = jax.nn.sigmoid(jnp.concatenate(i_parts, axis=-1) + bx_ref[...])
    log_a = -LRU_C * r * jax.nn.softplus(-lam_ref[...])
    a = jnp.exp(log_a)
    u = jnp.sqrt(jnp.tanh(-log_a) * (a * a + 1.0)) * (i * xc)
    row = lax.broadcasted_iota(jnp.int32, (cs, D_GROUP), 0)
    u = u + jnp.where(row == 0, a * h_scr[...], 0.0)
    shift = 1
    while shift < cs:
        a_sh = pltpu.roll(a, shift, 0)
        u_sh = pltpu.roll(u, shift, 0)
        live = row >= shift
        u = jnp.where(live, a * u_sh + u, u)
        a = jnp.where(live, a * a_sh, a)
        shift *= 2
    h_scr[...] = u[cs - 1:cs, :]
    y_ref[...] = (u * jax.nn.gelu(gate_ref[...])).astype(y_ref.dtype)
    _conv_advance(win_scr, cs)


def _lru_call(wide, h0, conv0, prm, prev, *, n_seq, seq_len, cs, g, layer, out_layer):
    rows, const, per_seq, state_in = _seq_specs(cs, layer, g)
    in_specs = [rows(D_GROUP, WIDE_OFF[SEG_LRU_X] // D_GROUP), rows(D_GROUP, WIDE_OFF[SEG_LRU_GATE] // D_GROUP),
                state_in((1, D_GROUP)), state_in((CONV_PREV, D_GROUP)),
                const((CONV_W, D_GROUP)), const((1, D_GROUP)),
                const((LRU_BLOCKS, LRU_BLOCK, LRU_BLOCK)), const((1, D_GROUP)),
                const((LRU_BLOCKS, LRU_BLOCK, LRU_BLOCK)), const((1, D_GROUP)), const((1, D_GROUP))]
    state_shapes = [(1, D_GROUP), (CONV_PREV, D_GROUP)]
    scratch = [(1, D_GROUP), (CONV_PAD + cs, D_GROUP)]
    in_arrays = (wide, wide, h0, conv0, prm['cw'], prm['cb'], prm['wa'], prm['ba'], prm['wx'], prm['bx'], prm['lam'])
    return _seq_call(_lru_kernel, "rglru_mixer", in_arrays, in_specs, 7, state_shapes, scratch, prev,
                     n_seq=n_seq, seq_len=seq_len, cs=cs, g=g, out_layer=out_layer)


def _lru_params(conv_w, conv_b, w_a, b_a, w_x, b_x, lam):
    return dict(cw=conv_w, cb=conv_b[None, :], wa=w_a.astype(BF16), ba=b_a.reshape(1, D_GROUP),
                wx=w_x.astype(BF16), bx=b_x.reshape(1, D_GROUP), lam=lam[None, :])


def _kernel_states(st):
    s_ssd, s_ssd_conv, s_gdn, s_gdn_conv, s_mc, s_mn, s_mm, s_lru, s_lru_conv = st
    lead = s_mm.shape[:2]
    m_rows = jnp.zeros(lead + (1, LANES), F32).at[:, :, 0, LANE_ML_I:LANE_ML_I + MLSTM_HEADS].set(s_mm)
    return (s_ssd, s_ssd_conv, s_gdn, s_gdn_conv, s_mc, s_mn, m_rows, s_lru[:, :, None, :], s_lru_conv)


def _mixer_group(h, w_wide, w_narrow, w_layer, kst, prm, prev, *, n_seq, seq_len, row0, g, layer):
    s_ssd, s_ssd_conv, s_gdn, s_gdn_conv, s_mc, s_mn, m_rows, s_lru, s_lru_conv = kst
    m = n_seq * seq_len
    wide = _matmul(h, w_wide, w_layer, MM_TN, "in_proj", row0, m).reshape(n_seq, seq_len, WIDE_COLS)
    narrow = _matmul(h, w_narrow, w_layer, NARROW_COLS, "in_proj_gates", row0, m).reshape(
        n_seq, seq_len, NARROW_COLS)
    cs = min(seq_len, CHUNK)
    kw = dict(n_seq=n_seq, seq_len=seq_len, cs=cs, g=g, layer=layer, out_layer=w_layer)
    p_ssd, p_gdn, p_ml, p_lru = (None,) * N_MIXERS if prev is None else prev
    y_a, *o_ssd = _ssd_call(wide, narrow, s_ssd, s_ssd_conv, prm['ssd'], p_ssd, **kw)
    y_b, *o_gdn = _gdn_call(wide, narrow, s_gdn, s_gdn_conv, prm['gdn'], p_gdn, **kw)
    y_c, *o_ml = _mlstm_call(wide, narrow, s_mc, s_mn, m_rows, prm['mlstm'], p_ml, **kw)
    y_d, *o_lru = _lru_call(wide, s_lru, s_lru_conv, prm['lru'], p_lru, **kw)
    return tuple(y.reshape(m, D_GROUP) for y in (y_a, y_b, y_c, y_d)), (o_ssd, o_gdn, o_ml, o_lru)


def _reference_states(stacked):
    (ssd, ssd_cx, ssd_cbc), (gdn, gdn_conv), (mc, mn, m_rows), (lru, lru_conv) = stacked
    return (ssd, jnp.concatenate([ssd_cx, ssd_cbc], axis=-1), gdn, gdn_conv, mc, mn,
            m_rows[:, :, 0, LANE_ML_I:LANE_ML_I + MLSTM_HEADS], lru[:, :, 0, :], lru_conv)


def _state_shapes(n):
    return ((n, SSD_HEADS, SSD_HEAD_DIM, SSD_STATE),
            (n, CONV_PREV, SSD_CONV_DIM),
            (n, GDN_HEADS, GDN_HEAD_DIM, GDN_HEAD_DIM),
            (n, CONV_PREV, GDN_CONV_DIM),
            (n, MLSTM_HEADS, MLSTM_HEAD_DIM, MLSTM_HEAD_DIM),
            (n, MLSTM_HEADS, MLSTM_HEAD_DIM),
            (n, MLSTM_HEADS),
            (n, D_GROUP),
            (n, CONV_PREV, D_GROUP))


def kernel(x_prompt, x_sample, state_ssd, state_ssd_conv, state_gdn, state_gdn_conv, state_mlstm_c, state_mlstm_n, state_mlstm_m, state_rglru, state_rglru_conv, c_prompt, c_sample, w_ada, ada_table, norm1_g, norm2_g, final_g, w_in, w_out, ssd_conv_w, ssd_conv_b, ssd_dt_bias, ssd_a_log, ssd_d, ssd_norm_g, gdn_conv_w, gdn_conv_b, gdn_dt_bias, gdn_a_log, gdn_norm_g, mlstm_i_bias, mlstm_f_bias, mlstm_norm_g, lru_conv_w, lru_conv_b, lru_w_a, lru_b_a, lru_w_x, lru_b_x, lru_lambda, moe_w_group, moe_b_group, moe_w_expert, moe_b_expert, moe_w_gate, moe_w_up, moe_w_down):
    d = D_MODEL

    w_in_wide, w_in_narrow = _w_in_relayout(w_in)
    w_out_b = w_out.astype(BF16)
    w_gate_b = moe_w_gate.astype(BF16)
    w_up_b = moe_w_up.astype(BF16)
    router_pad = jnp.zeros((DEPTH, d, ROUTER_LANES - N_EXPERT_GROUPS - N_EXPERTS), F32)
    w_router = jnp.concatenate([moe_w_group, moe_w_expert, router_pad], axis=-1)
    w_router_hi = w_router.astype(BF16)
    w_router_lo = (w_router - w_router_hi.astype(F32)).astype(BF16)
    w_router = jnp.concatenate([w_router_hi, w_router_lo], axis=-1)
    b_router = jnp.concatenate([moe_b_group, moe_b_expert, router_pad[:, 0, :]], axis=-1)[:, None, :]

    c_pad = jnp.concatenate([c_prompt, c_sample, jnp.zeros((ADA_ROWS - BATCH - DEC_BATCH, d), F32)], axis=0)
    mod_shared = _ada_matmul(c_pad, w_ada)
    mod_tiles = jnp.concatenate(
        [jnp.broadcast_to(mod_shared[:BATCH, None, :], (BATCH, MOD_ROWS, 6 * d)),
         mod_shared[BATCH:BATCH + DEC_BATCH].reshape(SAMPLE_ROW_TILES, MOD_ROWS, 6 * d)], axis=0)
    ada_rows = ada_table.reshape(DEPTH * 6, 1, d)

    x = jnp.concatenate([x_prompt.reshape(N_PROMPT_TOK, d), x_sample.reshape(N_SAMPLE_TOK, d)], axis=0)

    st_sample = (state_ssd, state_ssd_conv, state_gdn, state_gdn_conv, state_mlstm_c,
                 state_mlstm_n, state_mlstm_m, state_rglru, state_rglru_conv)
    kst_sample = _kernel_states(st_sample)
    kst_prompt = _kernel_states(tuple(jnp.zeros((1,) + s, F32) for s in _state_shapes(BATCH)))
    stacked_p = stacked_s = None

    y_prev = None
    for l in range(DEPTH):
        if l == 0:
            (h,) = _rowwise(x, norm1_g[l][None, :], mod_tiles=mod_tiles, ada_rows=ada_rows, layer=l,
                            mod_chunks=(1, 0), name="norm1")
        else:
            x, h = _rowwise(x, norm1_g[l][None, :], y=y_prev, mod_tiles=mod_tiles, ada_rows=ada_rows,
                            layer=l, gate_chunk=5, gate_layer=l - 1, mod_chunks=(1, 0), name="resid_norm1")
        prm = dict(ssd=_ssd_params(ssd_conv_w[l], ssd_conv_b[l], ssd_dt_bias[l], ssd_a_log[l], ssd_d[l],
                                   ssd_norm_g[l]),
                   gdn=_gdn_params(gdn_conv_w[l], gdn_conv_b[l], gdn_dt_bias[l], gdn_a_log[l], gdn_norm_g[l]),
                   mlstm=_mlstm_params(mlstm_i_bias[l], mlstm_f_bias[l], mlstm_norm_g[l]),
                   lru=_lru_params(lru_conv_w[l], lru_conv_b[l], lru_w_a[l], lru_b_a[l], lru_w_x[l], lru_b_x[l],
                                   lru_lambda[l]))
        ys_p, stacked_p = _mixer_group(h, w_in_wide, w_in_narrow, l, kst_prompt, prm, stacked_p, n_seq=BATCH,
                                       seq_len=SEQ, row0=0, g=PROMPT_SEQS_PER_STEP, layer=0)
        ys_s, stacked_s = _mixer_group(h, w_in_wide, w_in_narrow, l, kst_sample, prm, stacked_s, n_seq=DEC_BATCH,
                                       seq_len=DEC_SEQ, row0=N_PROMPT_TOK, g=SAMPLE_SEQS_PER_STEP, layer=l)
        y_mix = _out_proj(ys_p, ys_s, w_out_b, l)

        x, h2, gates = _rowwise(x, norm2_g[l][None, :], y=y_mix, mod_tiles=mod_tiles, ada_rows=ada_rows,
                                layer=l, gate_chunk=2, mod_chunks=(4, 3),
                                router=(w_router[l], b_router[l]), name="resid_norm2_router")
        src, valid, dest, tile_ea, tile_eb, n_used = _moe_plan(gates)
        x_sorted = _take_rows(h2, src)
        gates_sorted = _take_rows(gates, src) * valid[:, None]
        y_sorted = _moe(x_sorted, gates_sorted, tile_ea, tile_eb, n_used, w_gate_b, w_up_b, moe_w_down, l)
        y_prev = _take_rows(y_sorted, dest)

    x, y_fin = _rowwise(x, final_g[None, :], y=y_prev, mod_tiles=mod_tiles, ada_rows=ada_rows,
                        layer=DEPTH - 1, gate_chunk=5, h_dtype=F32, name="resid_final_norm")
    outs = [y_fin[:N_PROMPT_TOK].reshape(BATCH, SEQ, d), y_fin[N_PROMPT_TOK:].reshape(DEC_BATCH, DEC_SEQ, d)]
    for st_p, st_s in zip(_reference_states(stacked_p), _reference_states(stacked_s)):
        outs.append(st_p)
        outs.append(st_s)
    return tuple(outs)
```

```python
import functools

import numpy as np
import jax
import jax.numpy as jnp
from jax import lax
from jax.experimental import pallas as pl
from jax.experimental.pallas import tpu as pltpu

F32 = jnp.float32
BF16 = jnp.bfloat16

D_MODEL = 4096
BATCH = 4
SEQ = 2048
DEPTH = 4
DEC_BATCH = 128
DEC_SEQ = 8
EPS = 1e-6
N_MIXERS = 4
D_GROUP = D_MODEL // N_MIXERS
CONV_W = 4
CHUNK = 64
SSD_HEAD_DIM = 64
SSD_HEADS = D_GROUP // SSD_HEAD_DIM
SSD_NGROUPS = 2
SSD_STATE = 128
SSD_BC_DIM = 2 * SSD_NGROUPS * SSD_STATE
SSD_CONV_DIM = D_GROUP + SSD_BC_DIM
GDN_HEAD_DIM = 128
GDN_HEADS = D_GROUP // GDN_HEAD_DIM
GDN_CONV_DIM = 3 * D_GROUP
MLSTM_HEAD_DIM = 128
MLSTM_HEADS = D_GROUP // MLSTM_HEAD_DIM
LRU_BLOCKS = 8
LRU_BLOCK = D_GROUP // LRU_BLOCKS
LRU_C = 8.0
N_EXPERT_GROUPS = 4
EXPERTS_PER_GROUP = 4
N_EXPERTS = N_EXPERT_GROUPS * EXPERTS_PER_GROUP
D_FF_EXPERT = D_MODEL // 8
IN_SIZES = (D_GROUP, SSD_CONV_DIM, SSD_HEADS,
            GDN_CONV_DIM, GDN_HEADS, GDN_HEADS, D_GROUP,
            3 * D_GROUP, MLSTM_HEADS, MLSTM_HEADS, D_GROUP,
            D_GROUP, D_GROUP)
IN_OFFSETS = tuple(int(o) for o in np.cumsum((0,) + IN_SIZES))
(SEG_SSD_Z, SEG_SSD_XBC, SEG_SSD_DT, SEG_GDN_QKV, SEG_GDN_A, SEG_GDN_B, SEG_GDN_Z,
 SEG_ML_QKV, SEG_ML_I, SEG_ML_F, SEG_ML_O, SEG_LRU_X, SEG_LRU_GATE) = range(len(IN_SIZES))

N_PROMPT_TOK = BATCH * SEQ
N_SAMPLE_TOK = DEC_BATCH * DEC_SEQ
N_TOK = N_PROMPT_TOK + N_SAMPLE_TOK

LANES = 128
SUBLANES = 8
VMEM_LIMIT_BYTES = 56 * 1024 * 1024

ROW_TILE = 256
MOD_REPEAT = DEC_SEQ
MOD_ROWS = ROW_TILE // MOD_REPEAT
N_ROW_TILES = N_TOK // ROW_TILE
PROMPT_TILES_PER_SEQ = SEQ // ROW_TILE
PROMPT_ROW_TILES = N_PROMPT_TOK // ROW_TILE
SAMPLE_ROW_TILES = N_SAMPLE_TOK // ROW_TILE
MM_TM = 1024
MM_TN = 512
ADA_ROWS = 144
ADA_TN = 512
MOE_TM = 512
MOE_FF_TILE = 256
MOE_FF_STEPS = D_FF_EXPERT // MOE_FF_TILE
TOP_K_IN_GROUP = 2
PAIRS = tuple((a, b) for a in range(EXPERTS_PER_GROUP) for b in range(a + 1, EXPERTS_PER_GROUP))
N_PAIRS = len(PAIRS)
N_BUCKETS = N_EXPERT_GROUPS * N_PAIRS
MOE_STEPS_PER_TILE = TOP_K_IN_GROUP * MOE_FF_STEPS
MOE_MAX_TILES = N_TOK // MOE_TM + N_BUCKETS
MOE_SLOTS = MOE_MAX_TILES * MOE_TM
ROUTER_LANES = LANES
EXPERT_LANE0 = N_EXPERT_GROUPS
BUCKET_LANE = 0

_WIDE_SEGS = (SEG_GDN_QKV, SEG_ML_QKV, SEG_SSD_Z, SEG_GDN_Z, SEG_ML_O, SEG_LRU_X, SEG_LRU_GATE, SEG_SSD_XBC)
_NARROW_SEGS = (SEG_SSD_DT, SEG_GDN_A, SEG_GDN_B, SEG_ML_I, SEG_ML_F)
WIDE_OFF = {}
_o = 0
for _s in _WIDE_SEGS:
    WIDE_OFF[_s] = _o
    _o += IN_SIZES[_s]
WIDE_COLS = _o
NARROW_OFF = {}
_o = 0
for _s in _NARROW_SEGS:
    NARROW_OFF[_s] = _o
    _o += IN_SIZES[_s]
NARROW_COLS = LANES
LANE_SSD_DT = NARROW_OFF[SEG_SSD_DT]
LANE_GDN_A = NARROW_OFF[SEG_GDN_A]
LANE_GDN_B = NARROW_OFF[SEG_GDN_B]
LANE_ML_I = NARROW_OFF[SEG_ML_I]
LANE_ML_F = NARROW_OFF[SEG_ML_F]

PROMPT_SEQS_PER_STEP = 4
SAMPLE_SEQS_PER_STEP = 8
CONV_PAD = SUBLANES
CONV_PREV = CONV_W - 1


def _cparams(*sem):
    return pltpu.CompilerParams(dimension_semantics=sem, vmem_limit_bytes=VMEM_LIMIT_BYTES)


def _dot(a, b):
    return jnp.dot(a.astype(BF16), b.astype(BF16), preferred_element_type=F32)


def _dot_nt(a, b):
    return lax.dot_general(a.astype(BF16), b.astype(BF16), (((1,), (1,)), ((), ())), preferred_element_type=F32)


def _dot_tn(a, b):
    return lax.dot_general(a.astype(BF16), b.astype(BF16), (((0,), (0,)), ((), ())), preferred_element_type=F32)


def _dot_f32(a, b):
    return jnp.dot(a, b, preferred_element_type=F32, precision=lax.Precision.HIGHEST)


def _silu(x):
    return x * jax.nn.sigmoid(x)


def _mm_kernel(a_ref, w_ref, o_ref):
    o_ref[...] = jnp.dot(a_ref[...], w_ref[...], preferred_element_type=F32).astype(o_ref.dtype)


def _matmul(a, w, layer, tn, name, row0, m):
    k = a.shape[1]
    n = w.shape[2]
    assert m % MM_TM == 0 and row0 % MM_TM == 0 and n % tn == 0
    tile0 = row0 // MM_TM
    return pl.pallas_call(
        _mm_kernel,
        grid=(m // MM_TM, n // tn),
        in_specs=[pl.BlockSpec((MM_TM, k), lambda i, j: (tile0 + i, 0)),
                  pl.BlockSpec((None, k, tn), lambda i, j: (layer, 0, j))],
        out_specs=pl.BlockSpec((MM_TM, tn), lambda i, j: (i, j)),
        out_shape=jax.ShapeDtypeStruct((m, n), F32),
        compiler_params=_cparams("parallel", "arbitrary"),
        name=name,
    )(a, w)


PROMPT_MM_TILES = N_PROMPT_TOK // MM_TM
assert N_SAMPLE_TOK == MM_TM


def _out_proj_kernel(*refs):
    yp_refs, ys_refs = refs[:N_MIXERS], refs[N_MIXERS:2 * N_MIXERS]
    w_refs, o_ref = refs[2 * N_MIXERS:3 * N_MIXERS], refs[3 * N_MIXERS]

    def project(y_refs):
        acc = jnp.dot(y_refs[0][...], w_refs[0][...], preferred_element_type=F32)
        for m in range(1, N_MIXERS):
            acc += jnp.dot(y_refs[m][...], w_refs[m][...], preferred_element_type=F32)
        o_ref[...] = acc

    is_prompt = pl.program_id(0) < PROMPT_MM_TILES
    pl.when(is_prompt)(lambda: project(yp_refs))
    pl.when(jnp.logical_not(is_prompt))(lambda: project(ys_refs))


def _out_proj(ys_prompt, ys_sample, w, layer):
    n = w.shape[2]
    yp_spec = pl.BlockSpec((MM_TM, D_GROUP), lambda i, j: (jnp.minimum(i, PROMPT_MM_TILES - 1), 0))
    ys_spec = pl.BlockSpec((MM_TM, D_GROUP), lambda i, j: (0, 0))
    w_specs = [pl.BlockSpec((None, D_GROUP, MM_TN), functools.partial(lambda i, j, m: (layer, m, j), m=m))
               for m in range(N_MIXERS)]
    return pl.pallas_call(
        _out_proj_kernel,
        grid=(N_TOK // MM_TM, n // MM_TN),
        in_specs=[yp_spec] * N_MIXERS + [ys_spec] * N_MIXERS + w_specs,
        out_specs=pl.BlockSpec((MM_TM, MM_TN), lambda i, j: (i, j)),
        out_shape=jax.ShapeDtypeStruct((N_TOK, n), F32),
        compiler_params=_cparams("parallel", "arbitrary"),
        name="out_proj",
    )(*ys_prompt, *ys_sample, w, w, w, w)


W_IN_ROW_TILE = 256


def _w_in_relayout_kernel(wt_ref, wide_ref, narrow_ref):
    for s in _WIDE_SEGS:
        wide_ref[:, WIDE_OFF[s]:WIDE_OFF[s] + IN_SIZES[s]] = (
            wt_ref[IN_OFFSETS[s]:IN_OFFSETS[s + 1], :].T.astype(BF16))
    rows = [wt_ref[IN_OFFSETS[s]:IN_OFFSETS[s + 1], :] for s in _NARROW_SEGS]
    used = sum(IN_SIZES[s] for s in _NARROW_SEGS)
    rows.append(jnp.zeros((NARROW_COLS - used, wt_ref.shape[1]), F32))
    narrow_ref[...] = jnp.concatenate(rows, axis=0).T.astype(BF16)


def _w_in_relayout(w_in):
    wt = jnp.swapaxes(w_in, 1, 2)
    depth, n, k = wt.shape
    return pl.pallas_call(
        _w_in_relayout_kernel,
        grid=(depth, k // W_IN_ROW_TILE),
        in_specs=[pl.BlockSpec((None, n, W_IN_ROW_TILE), lambda l, i: (l, 0, i))],
        out_specs=[pl.BlockSpec((None, W_IN_ROW_TILE, WIDE_COLS), lambda l, i: (l, i, 0)),
                   pl.BlockSpec((None, W_IN_ROW_TILE, NARROW_COLS), lambda l, i: (l, i, 0))],
        out_shape=[jax.ShapeDtypeStruct((depth, k, WIDE_COLS), BF16),
                   jax.ShapeDtypeStruct((depth, k, NARROW_COLS), BF16)],
        compiler_params=_cparams("parallel", "parallel"),
        name="w_in_relayout",
    )(wt)


def _ada_kernel(c_ref, w_ref, o_ref):
    o_ref[...] = _dot(_silu(c_ref[...]), w_ref[...])


def _ada_matmul(c_pad, w_ada):
    k, n = w_ada.shape
    return pl.pallas_call(
        _ada_kernel,
        grid=(n // ADA_TN,),
        in_specs=[pl.BlockSpec((ADA_ROWS, k), lambda j: (0, 0)),
                  pl.BlockSpec((k, ADA_TN), lambda j: (0, j))],
        out_specs=pl.BlockSpec((ADA_ROWS, ADA_TN), lambda j: (0, j)),
        out_shape=jax.ShapeDtypeStruct((ADA_ROWS, n), F32),
        compiler_params=_cparams("arbitrary"),
        name="adaln_matmul",
    )(c_pad, w_ada)


def _route(lg):
    lane = lax.broadcasted_iota(jnp.int32, lg.shape, 1).astype(F32)
    neg = -jnp.inf
    big = float(ROUTER_LANES)
    gl = jnp.where(lane < N_EXPERT_GROUPS, lg, neg)
    gmax = jnp.max(gl, axis=-1, keepdims=True)
    gidx = jnp.min(jnp.where(gl == gmax, lane, big), axis=-1, keepdims=True)
    g_w = 1.0 / jnp.sum(jnp.exp(gl - gmax), axis=-1, keepdims=True)
    lo = EXPERT_LANE0 + EXPERTS_PER_GROUP * gidx
    el = jnp.where((lane >= lo) & (lane < lo + EXPERTS_PER_GROUP), lg, neg)
    e1 = jnp.max(el, axis=-1, keepdims=True)
    i1 = jnp.min(jnp.where(el == e1, lane, big), axis=-1, keepdims=True)
    el2 = jnp.where(lane == i1, neg, el)
    e2 = jnp.max(el2, axis=-1, keepdims=True)
    i2 = jnp.min(jnp.where(el2 == e2, lane, big), axis=-1, keepdims=True)
    t = jnp.exp(e2 - e1)
    w1 = g_w / (1.0 + t)
    w2 = g_w * t / (1.0 + t)
    out = jnp.where(lane == i1, w1, 0.0) + jnp.where(lane == i2, w2, 0.0)
    pa = jnp.minimum(i1, i2) - lo
    pb = jnp.maximum(i1, i2) - lo
    pair = pa * (2 * EXPERTS_PER_GROUP - 1 - pa) * 0.5 + (pb - pa - 1.0)
    return out + jnp.where(lane == float(BUCKET_LANE), gidx * N_PAIRS + pair, 0.0)


def _rowwise_kernel(*refs, has_resid, has_mod, has_router):
    it = iter(refs)
    x_ref = next(it)
    if has_resid:
        y_ref, gate_m, gate_t = next(it), next(it), next(it)
    g_ref = next(it)
    if has_mod:
        sc_m, sc_t, sh_m, sh_t = next(it), next(it), next(it), next(it)
    if has_router:
        wr_ref, br_ref = next(it), next(it)
    if has_resid:
        xo_ref = next(it)
    h_ref = next(it)
    if has_router:
        gt_ref = next(it)

    def mod(m_ref, t_ref):
        m = m_ref[...] + t_ref[...]
        return jnp.broadcast_to(m[:, None, :], (MOD_ROWS, MOD_REPEAT, m.shape[-1])).reshape(ROW_TILE, m.shape[-1])

    x = x_ref[...]
    if has_resid:
        x = x + mod(gate_m, gate_t) * y_ref[...]
        xo_ref[...] = x
    y = x * lax.rsqrt(jnp.mean(x * x, axis=-1, keepdims=True) + EPS)
    h = y * g_ref[...]
    if has_mod:
        h = h * (1.0 + mod(sc_m, sc_t)) + mod(sh_m, sh_t)
    h_ref[...] = h.astype(h_ref.dtype)
    if has_router:
        h_hi = h.astype(BF16)
        h_lo = (h - h_hi.astype(F32)).astype(BF16)
        hi_terms = jnp.dot(h_hi, wr_ref[...], preferred_element_type=F32)
        lg = (hi_terms[:, :ROUTER_LANES] + hi_terms[:, ROUTER_LANES:]
              + jnp.dot(h_lo, wr_ref[:, :ROUTER_LANES], preferred_element_type=F32))
        gt_ref[...] = _route(lg + br_ref[...])


def _mod_tile_index(i):
    return jnp.where(i < PROMPT_ROW_TILES, i // PROMPT_TILES_PER_SEQ, BATCH + i - PROMPT_ROW_TILES)


def _rowwise(x, g_row, *, y=None, mod_tiles=None, ada_rows=None, layer=0, gate_chunk=None, gate_layer=None,
             mod_chunks=None, router=None, h_dtype=BF16, name="rowwise"):
    has_resid = y is not None
    has_mod = mod_chunks is not None
    has_router = router is not None
    d = D_MODEL
    row_spec = pl.BlockSpec((ROW_TILE, d), lambda i: (i, 0))
    vec_spec = pl.BlockSpec((1, d), lambda i: (0, 0))

    def mod_specs(chunk, lyr):
        return [pl.BlockSpec((None, MOD_ROWS, d), lambda i: (_mod_tile_index(i), 0, chunk)),
                pl.BlockSpec((None, 1, d), lambda i: (lyr * 6 + chunk, 0, 0))]

    args, specs = [x], [row_spec]
    if has_resid:
        args += [y, mod_tiles, ada_rows]
        specs += [row_spec] + mod_specs(gate_chunk, layer if gate_layer is None else gate_layer)
    args.append(g_row)
    specs.append(vec_spec)
    if has_mod:
        sc_chunk, sh_chunk = mod_chunks
        args += [mod_tiles, ada_rows, mod_tiles, ada_rows]
        specs += mod_specs(sc_chunk, layer) + mod_specs(sh_chunk, layer)
    if has_router:
        w_r, b_r = router
        args += [w_r, b_r]
        specs += [pl.BlockSpec((d, 2 * ROUTER_LANES), lambda i: (0, 0)),
                  pl.BlockSpec((1, ROUTER_LANES), lambda i: (0, 0))]
    out_shape, out_specs = [], []
    if has_resid:
        out_shape.append(jax.ShapeDtypeStruct((N_TOK, d), F32))
        out_specs.append(row_spec)
    out_shape.append(jax.ShapeDtypeStruct((N_TOK, d), h_dtype))
    out_specs.append(row_spec)
    if has_router:
        out_shape.append(jax.ShapeDtypeStruct((N_TOK, ROUTER_LANES), F32))
        out_specs.append(pl.BlockSpec((ROW_TILE, ROUTER_LANES), lambda i: (i, 0)))
    return pl.pallas_call(
        functools.partial(_rowwise_kernel, has_resid=has_resid, has_mod=has_mod, has_router=has_router),
        grid=(N_ROW_TILES,),
        in_specs=specs,
        out_specs=out_specs,
        out_shape=out_shape,
        compiler_params=_cparams("parallel"),
        name=name,
    )(*args)


def _moe_kernel(ea_ref, eb_ref, nu_ref, x_ref, gt_ref, wg_ref, wu_ref, wd_ref, o_ref):
    i = pl.program_id(0)
    j = pl.program_id(1)
    used = i < nu_ref[0]

    @pl.when(j == 0)
    def _():
        o_ref[...] = jnp.zeros_like(o_ref)

    @pl.when(used)
    def _():
        x = x_ref[...]
        a = jnp.dot(x, wg_ref[...], preferred_element_type=F32)
        b = jnp.dot(x, wu_ref[...], preferred_element_type=F32)
        he = _silu(a) * b
        gates = gt_ref[...]
        lane = lax.broadcasted_iota(jnp.int32, gates.shape, 1)
        col = EXPERT_LANE0 + jnp.where(j < MOE_FF_STEPS, ea_ref[i], eb_ref[i])
        gcol = jnp.sum(jnp.where(lane == col, gates, 0.0), axis=-1, keepdims=True)
        o_ref[...] += gcol * jnp.dot(he.astype(BF16), wd_ref[...].astype(BF16), preferred_element_type=F32)


def _moe(x_sorted, gates_sorted, tile_ea, tile_eb, n_used, w_gate, w_up, w_down, layer):
    d = D_MODEL

    def expert_of(i, j, ea, eb, nu):
        jj = jnp.where(i < nu[0], j, MOE_STEPS_PER_TILE - 1)
        return jnp.where(jj < MOE_FF_STEPS, ea[i], eb[i]), jj % MOE_FF_STEPS

    def in_row_map(i, j, ea, eb, nu):
        return (jnp.minimum(i, jnp.minimum(nu[0], MOE_MAX_TILES - 1)), 0)

    def wgu_map(i, j, ea, eb, nu):
        e, f = expert_of(i, j, ea, eb, nu)
        return (layer, e, 0, f)

    def wd_map(i, j, ea, eb, nu):
        e, f = expert_of(i, j, ea, eb, nu)
        return (layer, e, f, 0)

    grid_spec = pltpu.PrefetchScalarGridSpec(
        num_scalar_prefetch=3,
        grid=(MOE_MAX_TILES, MOE_STEPS_PER_TILE),
        in_specs=[pl.BlockSpec((MOE_TM, d), in_row_map),
                  pl.BlockSpec((MOE_TM, ROUTER_LANES), in_row_map),
                  pl.BlockSpec((None, None, d, MOE_FF_TILE), wgu_map),
                  pl.BlockSpec((None, None, d, MOE_FF_TILE), wgu_map),
                  pl.BlockSpec((None, None, MOE_FF_TILE, d), wd_map)],
        out_specs=pl.BlockSpec((MOE_TM, d), lambda i, j, ea, eb, nu: (i, 0)),
    )
    return pl.pallas_call(
        _moe_kernel,
        grid_spec=grid_spec,
        out_shape=jax.ShapeDtypeStruct((MOE_SLOTS, d), F32),
        compiler_params=_cparams("arbitrary", "arbitrary"),
        name="moe_experts",
    )(tile_ea, tile_eb, n_used, x_sorted, gates_sorted, w_gate, w_up, w_down)


def _take_rows(a, idx):
    return a.at[idx].get(mode="promise_in_bounds")


def _moe_plan(gates):
    bid = gates[:, BUCKET_LANE].astype(jnp.int32)
    onehot = (bid[:, None] == jnp.arange(N_BUCKETS, dtype=jnp.int32)[None, :]).astype(jnp.int32)
    rank = jnp.take_along_axis(jnp.cumsum(onehot, axis=0), bid[:, None], axis=1)[:, 0] - 1
    counts = jnp.sum(onehot, axis=0)
    tiles = (counts + MOE_TM - 1) // MOE_TM
    tile_end = jnp.cumsum(tiles)
    tile_start = tile_end - tiles
    dest = tile_start[bid] * MOE_TM + rank
    src = (jnp.arange(MOE_SLOTS, dtype=jnp.int32) % N_TOK).at[dest].set(jnp.arange(N_TOK, dtype=jnp.int32))
    n_used = tile_end[-1]
    t = jnp.arange(MOE_MAX_TILES, dtype=jnp.int32)
    tile_bucket = jnp.sum((jnp.minimum(t, n_used - 1)[:, None] >= tile_end[None, :]).astype(jnp.int32), axis=1)
    slot = jnp.arange(MOE_SLOTS, dtype=jnp.int32)
    slot_bucket = jnp.repeat(tile_bucket, MOE_TM)
    valid = ((slot - tile_start[slot_bucket] * MOE_TM < counts[slot_bucket])
             & (slot < n_used * MOE_TM)).astype(F32)
    group0 = (tile_bucket // N_PAIRS) * EXPERTS_PER_GROUP
    pair = tile_bucket % N_PAIRS
    tile_ea = group0 + jnp.asarray([p[0] for p in PAIRS], jnp.int32)[pair]
    tile_eb = group0 + jnp.asarray([p[1] for p in PAIRS], jnp.int32)[pair]
    return src, valid, dest, tile_ea, tile_eb, n_used.reshape(1).astype(jnp.int32)


def _conv_init(win_scr, c0_ref):
    win_scr[0:CONV_PAD, :] = jnp.zeros((CONV_PAD, win_scr.shape[1]), F32)
    win_scr[CONV_PAD - CONV_PREV:CONV_PAD, :] = c0_ref[...]


def _conv_step(u_ref, win_scr, w_ref, b_ref, cs):
    u = u_ref[...]
    prev = win_scr[0:CONV_PAD, :]
    row = lax.broadcasted_iota(jnp.int32, (CONV_PAD, 1), 0)
    out = b_ref[...]
    for j in range(CONV_W):
        shift = CONV_PREV - j
        if shift == 0:
            tap = u
        else:
            rolled = pltpu.roll(u, shift, 0)
            head = jnp.where(row >= shift, rolled[0:CONV_PAD, :], pltpu.roll(prev, shift, 0))
            tap = head if cs == CONV_PAD else jnp.concatenate([head, rolled[CONV_PAD:, :]], axis=0)
        out = out + tap * w_ref[j:j + 1, :]
    win_scr[CONV_PAD:2 * CONV_PAD, :] = u[cs - CONV_PAD:cs, :]
    return out


def _conv_advance(win_scr, cs):
    win_scr[0:CONV_PAD, :] = win_scr[CONV_PAD:2 * CONV_PAD, :]


def _causal_masks(cs):
    row = lax.broadcasted_iota(jnp.int32, (cs, cs), 0)
    col = lax.broadcasted_iota(jnp.int32, (cs, cs), 1)
    return row >= col, row > col


def _cumsum_rows(x):
    cs = x.shape[0]
    row = lax.broadcasted_iota(jnp.int32, x.shape, 0)
    shift = 1
    while shift < cs:
        x = x + jnp.where(row >= shift, pltpu.roll(x, shift, 0), 0.0)
        shift *= 2
    return x


def _rms(x, width):
    return x * lax.rsqrt(jnp.sum(x * x, axis=-1, keepdims=True) * (1.0 / width) + EPS)


def _seq_specs(cs, layer, g):
    def rows(width, col):
        return pl.BlockSpec((g, cs, width), lambda b, c: (b, c, col))

    def const(shape):
        nd = len(shape)
        return pl.BlockSpec(shape, lambda b, c: (0,) * nd)

    def per_seq(shape, last=0):
        nd = len(shape)
        return pl.BlockSpec((g,) + shape, lambda b, c: (b,) + (0,) * (nd - 1) + (last,))

    def state_in(shape, last=0):
        nd = len(shape)
        return pl.BlockSpec((None, g) + shape, lambda b, c: (layer, b) + (0,) * (nd - 1) + (last,))

    return rows, const, per_seq, state_in


def _seq_batched_kernel(*refs, body, per_seq_refs, alias_range, g, cs, nc):
    refs = refs[:alias_range[0]] + refs[alias_range[1]:]
    c = pl.program_id(1)
    views = [tuple(r.at[s] if flag else r for r, flag in zip(refs, per_seq_refs)) for s in range(g)]

    @pl.when(c == 0)
    def _():
        for v in views:
            body(*v, cs=cs, phase="init")

    for v in views:
        body(*v, cs=cs, phase="main")

    @pl.when(c == nc - 1)
    def _():
        for v in views:
            body(*v, cs=cs, phase="final")


def _seq_call(body, name, in_arrays, in_specs, n_const, state_shapes, scratch, prev, *,
              n_seq, seq_len, cs, g, out_layer):
    assert n_seq % g == 0
    nc = seq_len // cs
    n_out = 1 + len(state_shapes)
    flags = ([True] * (len(in_arrays) - n_const) + [False] * n_const + [True] * (n_out + len(scratch)))
    out_specs = [pl.BlockSpec((g, cs, D_GROUP), lambda b, c: (b, c, 0))]
    out_shape = [jax.ShapeDtypeStruct((n_seq, seq_len, D_GROUP), BF16)]
    for shape in state_shapes:
        nd = len(shape)
        out_specs.append(pl.BlockSpec((None, g) + shape, lambda b, c, nd=nd: (out_layer, b) + (0,) * nd))
        out_shape.append(jax.ShapeDtypeStruct((DEPTH, n_seq) + shape, F32))
    n_in = len(in_arrays)
    if prev is None:
        prev = tuple(jnp.zeros(s.shape, s.dtype) for s in out_shape[1:])
    in_arrays = tuple(in_arrays) + tuple(prev)
    in_specs = list(in_specs) + [pl.BlockSpec(memory_space=pl.ANY)] * len(prev)
    aliases = {n_in + k: 1 + k for k in range(len(prev))}
    return pl.pallas_call(
        functools.partial(_seq_batched_kernel, body=body, per_seq_refs=tuple(flags),
                          alias_range=(n_in, len(in_arrays)), g=g, cs=cs, nc=nc),
        grid=(n_seq // g, nc),
        in_specs=in_specs,
        out_specs=out_specs,
        out_shape=out_shape,
        scratch_shapes=[pltpu.VMEM((g,) + shape, F32) for shape in scratch],
        input_output_aliases=aliases,
        compiler_params=_cparams("parallel", "arbitrary"),
        name=name,
    )(*in_arrays)


def _ssd_kernel(z_ref, x_ref, bc_ref, nar_ref, s0_ref, c0x_ref, c0bc_ref,
                cwx_ref, cbx_ref, cwbc_ref, cbbc_ref, bias_ref, alog_ref, dsk_ref, ng_ref,
                y_ref, s_out_ref, cx_out_ref, cbc_out_ref,
                st_scr, wx_scr, wbc_scr, ycat_scr, *, cs, phase):
    lo = CONV_PAD - CONV_PREV
    if phase == "init":
        st_scr[...] = s0_ref[...]
        _conv_init(wx_scr, c0x_ref)
        _conv_init(wbc_scr, c0bc_ref)
        return
    if phase == "final":
        s_out_ref[...] = st_scr[...]
        cx_out_ref[...] = wx_scr[lo:CONV_PAD, :]
        cbc_out_ref[...] = wbc_scr[lo:CONV_PAD, :]
        return

    xs = _silu(_conv_step(x_ref, wx_scr, cwx_ref, cbx_ref, cs))
    bcs = _silu(_conv_step(bc_ref, wbc_scr, cwbc_ref, cbbc_ref, cs))
    incl, _ = _causal_masks(cs)
    dt = jax.nn.softplus(nar_ref[...] + bias_ref[...])
    cum = _cumsum_rows(dt * (-jnp.exp(alog_ref[...])))
    cum_t = cum.T
    exp_cum = jnp.exp(cum)
    cum_last = cum[cs - 1:cs, :]
    w_end = jnp.exp(cum_last - cum)
    chunk_decay = jnp.exp(cum_last)
    heads_per_group = SSD_HEADS // SSD_NGROUPS
    heads = range(SSD_HEADS)
    hp = SSD_HEAD_DIM
    s_old = [st_scr[h] for h in heads]
    col = lambda arr, h: arr[:, LANE_SSD_DT + h:LANE_SSD_DT + h + 1]
    b_gs = [bcs[:, g * SSD_STATE:(g + 1) * SSD_STATE] for g in range(SSD_NGROUPS)]
    c_gs = [bcs[:, (SSD_NGROUPS + g) * SSD_STATE:(SSD_NGROUPS + g + 1) * SSD_STATE] for g in range(SSD_NGROUPS)]
    cbs = [_dot_nt(c_gs[g], b_gs[g]) for g in range(SSD_NGROUPS)]
    grp = [h // heads_per_group for h in heads]
    scores = [cbs[grp[h]] * jnp.exp(jnp.where(incl, col(cum, h) - cum_t[LANE_SSD_DT + h:LANE_SSD_DT + h + 1, :],
                                              -jnp.inf)) for h in heads]
    xdts = [xs[:, h * hp:(h + 1) * hp] * col(dt, h) for h in heads]
    ys = [_dot(scores[h], xdts[h]) + _dot_nt(c_gs[grp[h]] * col(exp_cum, h), s_old[h]) for h in heads]
    s_new = [s_old[h] * col(chunk_decay, h) + _dot_tn(xdts[h] * col(w_end, h), b_gs[grp[h]]) for h in heads]
    for h in heads:
        st_scr[h] = s_new[h]
        ycat_scr[:, h * hp:(h + 1) * hp] = ys[h]
    y = (ycat_scr[...] + dsk_ref[...] * xs) * _silu(z_ref[...])
    gw = D_GROUP // SSD_NGROUPS
    for g in range(SSD_NGROUPS):
        y_g = _rms(y[:, g * gw:(g + 1) * gw], gw) * ng_ref[:, g * gw:(g + 1) * gw]
        y_ref[:, g * gw:(g + 1) * gw] = y_g.astype(y_ref.dtype)
    _conv_advance(wx_scr, cs)
    _conv_advance(wbc_scr, cs)


def _ssd_call(wide, narrow, s0, conv0, prm, prev, *, n_seq, seq_len, cs, g, layer, out_layer):
    rows, const, per_seq, state_in = _seq_specs(cs, layer, g)
    xw, bcw = D_GROUP, SSD_BC_DIM
    x_off = WIDE_OFF[SEG_SSD_XBC]
    in_specs = [rows(xw, WIDE_OFF[SEG_SSD_Z] // xw), rows(xw, x_off // xw), rows(bcw, (x_off + xw) // bcw),
                rows(NARROW_COLS, 0),
                state_in((SSD_HEADS, SSD_HEAD_DIM, SSD_STATE)),
                state_in((CONV_PREV, xw)), state_in((CONV_PREV, bcw), last=xw // bcw),
                const((CONV_W, xw)), const((1, xw)), const((CONV_W, bcw)), const((1, bcw)),
                const((1, LANES)), const((1, LANES)), const((1, xw)), const((1, xw))]
    state_shapes = [(SSD_HEADS, SSD_HEAD_DIM, SSD_STATE), (CONV_PREV, xw), (CONV_PREV, bcw)]
    scratch = [(SSD_HEADS, SSD_HEAD_DIM, SSD_STATE), (CONV_PAD + cs, xw), (CONV_PAD + cs, bcw), (cs, xw)]
    in_arrays = (wide, wide, wide, narrow, s0, conv0, conv0,
                 prm['cw'][:, :xw], prm['cb'][:, :xw], prm['cw'][:, xw:], prm['cb'][:, xw:],
                 prm['bias'], prm['alog'], prm['dskip'], prm['ng'])
    return _seq_call(_ssd_kernel, "ssd_mixer", in_arrays, in_specs, 8, state_shapes, scratch, prev,
                     n_seq=n_seq, seq_len=seq_len, cs=cs, g=g, out_layer=out_layer)


def _ssd_params(conv_w, conv_b, dt_bias, a_log, d_skip, norm_g):
    pad = jnp.zeros((LANES - SSD_HEADS,), F32)
    return dict(cw=conv_w, cb=conv_b[None, :],
                bias=jnp.concatenate([dt_bias, pad])[None, :],
                alog=jnp.concatenate([a_log, pad])[None, :],
                dskip=jnp.repeat(d_skip, SSD_HEAD_DIM)[None, :], ng=norm_g[None, :])


def _split_hi_lo(x):
    hi = x.astype(BF16).astype(F32)
    lo = (x - hi).astype(BF16).astype(F32)
    return hi, lo


def _lhs3(a):
    hi, lo = _split_hi_lo(a)
    return jnp.concatenate([hi, lo, hi], axis=1).astype(BF16)


def _rhs3(b):
    hi, lo = _split_hi_lo(b)
    return jnp.concatenate([hi, hi, lo], axis=0).astype(BF16)


def _dot3(lhs3, rhs3):
    return jnp.dot(lhs3, rhs3, preferred_element_type=F32)


def _inv_unit_lower(a_list, cs):
    row = lax.broadcasted_iota(jnp.int32, (cs, cs), 0)
    col = lax.broadcasted_iota(jnp.int32, (cs, cs), 1)
    eye = jnp.where(row == col, 1.0, 0.0)
    ps = [-a for a in a_list]
    ts = [eye + p for p in ps]
    forms = [(_lhs3(p), _rhs3(p)) for p in ps]
    n = 1
    while 2 * n < cs:
        ps = [_dot3(lhs, rhs) for lhs, rhs in forms]
        forms = [(_lhs3(p), _rhs3(p)) for p in ps]
        ts = [t + _dot3(_lhs3(t), rhs) for t, (_, rhs) in zip(ts, forms)]
        n *= 2
    return ts


def _gdn_kernel(qkv_ref, z_ref, nar_ref, s0_ref, c0_ref, cw_ref, cb_ref, bias_ref, alog_ref, ng_ref,
                y_ref, s_out_ref, c_out_ref, st_scr, win_scr, *, cs, phase):
    if phase == "init":
        st_scr[...] = s0_ref[...]
        _conv_init(win_scr, c0_ref)
        return
    if phase == "final":
        s_out_ref[...] = st_scr[...]
        c_out_ref[...] = win_scr[CONV_PAD - CONV_PREV:CONV_PAD, :]
        return

    qkv = _silu(_conv_step(qkv_ref, win_scr, cw_ref, cb_ref, cs))
    incl, strict = _causal_masks(cs)
    nar = nar_ref[...]
    g_log = -jnp.exp(alog_ref[...]) * jax.nn.softplus(nar + bias_ref[...])
    beta_all = jax.nn.sigmoid(nar)
    gc = _cumsum_rows(g_log)
    gc_t = gc.T
    exp_gc = jnp.exp(gc)
    gc_last = gc[cs - 1:cs, :]
    exp_to_end = jnp.exp(gc_last - gc)
    g_last = jnp.exp(gc_last)
    hd = GDN_HEAD_DIM
    heads = range(GDN_HEADS)
    s_old = [st_scr[h] for h in heads]
    col = lambda arr, lane: arr[:, lane:lane + 1]
    qs = [qkv[:, h * hd:(h + 1) * hd] for h in heads]
    ks = [qkv[:, D_GROUP + h * hd:D_GROUP + (h + 1) * hd] for h in heads]
    vs = [qkv[:, 2 * D_GROUP + h * hd:2 * D_GROUP + (h + 1) * hd] for h in heads]
    qs = [q * lax.rsqrt(jnp.sum(q * q, axis=-1, keepdims=True) + EPS) * (hd ** -0.5) for q in qs]
    ks = [k * lax.rsqrt(jnp.sum(k * k, axis=-1, keepdims=True) + EPS) for k in ks]
    decays = [jnp.exp(jnp.where(incl, col(gc, LANE_GDN_A + h) - gc_t[LANE_GDN_A + h:LANE_GDN_A + h + 1, :],
                                -jnp.inf)) for h in heads]
    betas = [col(beta_all, LANE_GDN_B + h) for h in heads]
    a_mats = [jnp.where(strict, betas[h] * _dot_nt(ks[h], ks[h]) * decays[h], 0.0) for h in heads]
    t_invs = _inv_unit_lower(a_mats, cs)
    rhs = [jnp.concatenate([vs[h] * betas[h], ks[h] * (betas[h] * col(exp_gc, LANE_GDN_A + h))], axis=1)
           for h in heads]
    uw = [_dot3(_lhs3(t_invs[h]), _rhs3(rhs[h])) for h in heads]
    qks = [_dot_nt(qs[h], ks[h]) * decays[h] for h in heads]
    v_new = [uw[h][:, :hd] - _dot(uw[h][:, hd:], s_old[h]) for h in heads]
    outs = [_dot(qs[h] * col(exp_gc, LANE_GDN_A + h), s_old[h]) + _dot(qks[h], v_new[h]) for h in heads]
    s_new = [s_old[h] * col(g_last, LANE_GDN_A + h) + _dot_tn(ks[h] * col(exp_to_end, LANE_GDN_A + h), v_new[h])
             for h in heads]
    for h in heads:
        st_scr[h] = s_new[h]
    y = jnp.concatenate([_rms(o, hd) for o in outs], axis=1) * ng_ref[...] * _silu(z_ref[...])
    y_ref[...] = y.astype(y_ref.dtype)
    _conv_advance(win_scr, cs)


def _gdn_call(wide, narrow, s0, conv0, prm, prev, *, n_seq, seq_len, cs, g, layer, out_layer):
    rows, const, per_seq, state_in = _seq_specs(cs, layer, g)
    hd = GDN_HEAD_DIM
    in_specs = [rows(GDN_CONV_DIM, WIDE_OFF[SEG_GDN_QKV] // GDN_CONV_DIM),
                rows(D_GROUP, WIDE_OFF[SEG_GDN_Z] // D_GROUP),
                rows(NARROW_COLS, 0),
                state_in((GDN_HEADS, hd, hd)), state_in((CONV_PREV, GDN_CONV_DIM)),
                const((CONV_W, GDN_CONV_DIM)), const((1, GDN_CONV_DIM)),
                const((1, LANES)), const((1, LANES)), const((1, D_GROUP))]
    state_shapes = [(GDN_HEADS, hd, hd), (CONV_PREV, GDN_CONV_DIM)]
    scratch = [(GDN_HEADS, hd, hd), (CONV_PAD + cs, GDN_CONV_DIM)]
    in_arrays = (wide, wide, narrow, s0, conv0, prm['cw'], prm['cb'], prm['bias'], prm['alog'], prm['ng'])
    return _seq_call(_gdn_kernel, "gdn_mixer", in_arrays, in_specs, 5, state_shapes, scratch, prev,
                     n_seq=n_seq, seq_len=seq_len, cs=cs, g=g, out_layer=out_layer)


def _lane_row(vec, lane0):
    return jnp.zeros((LANES,), F32).at[lane0:lane0 + vec.shape[0]].set(vec)[None, :]


def _gdn_params(conv_w, conv_b, dt_bias, a_log, norm_g):
    return dict(cw=conv_w, cb=conv_b[None, :], bias=_lane_row(dt_bias, LANE_GDN_A),
                alog=_lane_row(a_log, LANE_GDN_A), ng=norm_g[None, :])


def _mlstm_kernel(qkv_ref, o_ref, nar_ref, c0_ref, n0_ref, m0_ref, bias_ref, ng_ref,
                  y_ref, c_out_ref, n_out_ref, m_out_ref, c_scr, n_scr, m_scr, *, cs, phase):
    if phase == "init":
        c_scr[...] = c0_ref[...]
        n_scr[...] = n0_ref[...]
        m_scr[...] = m0_ref[...]
        return
    if phase == "final":
        c_out_ref[...] = c_scr[...]
        n_out_ref[...] = n_scr[...]
        m_out_ref[...] = m_scr[...]
        return

    incl, _ = _causal_masks(cs)
    pre = nar_ref[...] + bias_ref[...]
    f_cum = _cumsum_rows(jax.nn.log_sigmoid(pre))
    f_cum_t = f_cum.T
    pre_t = pre.T
    hd = MLSTM_HEAD_DIM
    heads = range(MLSTM_HEADS)
    c_old = [c_scr[h] for h in heads]
    n_all = n_scr[...]
    m_all = m_scr[...]
    qkv = qkv_ref[...]
    col = lambda arr, lane: arr[:, lane:lane + 1]
    qs = [qkv[:, h * hd:(h + 1) * hd] for h in heads]
    ks = [qkv[:, D_GROUP + h * hd:D_GROUP + (h + 1) * hd] * (hd ** -0.5) for h in heads]
    vs = [qkv[:, 2 * D_GROUP + h * hd:2 * D_GROUP + (h + 1) * hd] for h in heads]
    f_cols = [col(f_cum, LANE_ML_F + h) for h in heads]
    d_logs = [jnp.where(incl, f_cols[h] - f_cum_t[LANE_ML_F + h:LANE_ML_F + h + 1, :]
                        + pre_t[LANE_ML_I + h:LANE_ML_I + h + 1, :], -jnp.inf) for h in heads]
    m_prev = [col(m_all, LANE_ML_I + h) for h in heads]
    m_t = [jnp.maximum(f_cols[h] + m_prev[h], jnp.max(d_logs[h], axis=-1, keepdims=True)) for h in heads]
    w_carry = [jnp.exp(f_cols[h] + m_prev[h] - m_t[h]) for h in heads]
    ps = [jnp.exp(d_logs[h] - m_t[h]) * _dot_nt(qs[h], ks[h]) for h in heads]
    num = [w_carry[h] * _dot(qs[h], c_old[h]) + _dot(ps[h], vs[h]) for h in heads]
    den = [w_carry[h] * jnp.sum(qs[h] * n_all[h:h + 1, :], axis=-1, keepdims=True)
           + jnp.sum(ps[h], axis=-1, keepdims=True) for h in heads]
    hs = [num[h] / jnp.maximum(jnp.abs(den[h]), jnp.exp(-m_t[h])) for h in heads]
    m_end = [m_t[h][cs - 1:cs, :] for h in heads]
    f_last = [f_cols[h][cs - 1:cs, :] for h in heads]
    w_prev = [jnp.exp(f_last[h] + m_prev[h] - m_end[h]) for h in heads]
    kws = [ks[h] * jnp.exp(f_last[h] - f_cols[h] + col(pre, LANE_ML_I + h) - m_end[h]) for h in heads]
    c_new = [w_prev[h] * c_old[h] + _dot_tn(kws[h], vs[h]) for h in heads]
    n_new = [w_prev[h] * n_all[h:h + 1, :] + jnp.sum(kws[h], axis=0, keepdims=True) for h in heads]
    lane = lax.broadcasted_iota(jnp.int32, m_all.shape, 1)
    m_new = m_all
    for h in heads:
        c_scr[h] = c_new[h]
        m_new = jnp.where(lane == LANE_ML_I + h, m_end[h], m_new)
    n_scr[...] = jnp.concatenate(n_new, axis=0)
    m_scr[...] = m_new
    y = jnp.concatenate([_rms(hh, hd) for hh in hs], axis=1) * ng_ref[...] * jax.nn.sigmoid(o_ref[...])
    y_ref[...] = y.astype(y_ref.dtype)


def _mlstm_call(wide, narrow, c0, n0, m0, prm, prev, *, n_seq, seq_len, cs, g, layer, out_layer):
    rows, const, per_seq, state_in = _seq_specs(cs, layer, g)
    hd = MLSTM_HEAD_DIM
    qkv_w = 3 * D_GROUP
    in_specs = [rows(qkv_w, WIDE_OFF[SEG_ML_QKV] // qkv_w),
                rows(D_GROUP, WIDE_OFF[SEG_ML_O] // D_GROUP),
                rows(NARROW_COLS, 0),
                state_in((MLSTM_HEADS, hd, hd)), state_in((MLSTM_HEADS, hd)), state_in((1, LANES)),
                const((1, LANES)), const((1, D_GROUP))]
    state_shapes = [(MLSTM_HEADS, hd, hd), (MLSTM_HEADS, hd), (1, LANES)]
    scratch = [(MLSTM_HEADS, hd, hd), (MLSTM_HEADS, hd), (1, LANES)]
    in_arrays = (wide, wide, narrow, c0, n0, m0, prm['bias'], prm['ng'])
    return _seq_call(_mlstm_kernel, "mlstm_mixer", in_arrays, in_specs, 2, state_shapes, scratch, prev,
                     n_seq=n_seq, seq_len=seq_len, cs=cs, g=g, out_layer=out_layer)


def _mlstm_params(i_bias, f_bias, norm_g):
    return dict(bias=_lane_row(i_bias, LANE_ML_I) + _lane_row(f_bias, LANE_ML_F), ng=norm_g[None, :])


def _lru_kernel(x_ref, gate_ref, h0_ref, c0_ref, cw_ref, cb_ref, wa_ref, ba_ref, wx_ref, bx_ref, lam_ref,
                y_ref, h_out_ref, c_out_ref, h_scr, win_scr, *, cs, phase):
    if phase == "init":
        h_scr[...] = h0_ref[...]
        _conv_init(win_scr, c0_ref)
        return
    if phase == "final":
        h_out_ref[...] = h_scr[...]
        c_out_ref[...] = win_scr[CONV_PAD - CONV_PREV:CONV_PAD, :]
        return

    xc = _conv_step(x_ref, win_scr, cw_ref, cb_ref, cs)
    r_parts, i_parts = [], []
    for g in range(LRU_BLOCKS):
        x_g = xc[:, g * LRU_BLOCK:(g + 1) * LRU_BLOCK].astype(BF16)
        r_parts.append(jnp.dot(x_g, wa_ref[g], preferred_element_type=F32))
        i_parts.append(jnp.dot(x_g, wx_ref[g], preferred_element_type=F32))
    r = jax.nn.sigmoid(jnp.concatenate(r_parts, axis=-1) + ba_ref[...])
    i = jax.nn.sigmoid(jnp.concatenate(i_parts, axis=-1) + bx_ref[...])
    log_a = -LRU_C * r * jax.nn.softplus(-lam_ref[...])
    a = jnp.exp(log_a)
    u = jnp.sqrt(jnp.tanh(-log_a) * (a * a + 1.0)) * (i * xc)
    row = lax.broadcasted_iota(jnp.int32, (cs, D_GROUP), 0)
    u = u + jnp.where(row == 0, a * h_scr[...], 0.0)
    shift = 1
    while shift < cs:
        a_sh = pltpu.roll(a, shift, 0)
        u_sh = pltpu.roll(u, shift, 0)
        live = row >= shift
        u = jnp.where(live, a * u_sh + u, u)
        a = jnp.where(live, a * a_sh, a)
        shift *= 2
    h_scr[...] = u[cs - 1:cs, :]
    y_ref[...] = (u * jax.nn.gelu(gate_ref[...])).astype(y_ref.dtype)
    _conv_advance(win_scr, cs)


def _lru_call(wide, h0, conv0, prm, prev, *, n_seq, seq_len, cs, g, layer, out_layer):
    rows, const, per_seq, state_in = _seq_specs(cs, layer, g)
    in_specs = [rows(D_GROUP, WIDE_OFF[SEG_LRU_X] // D_GROUP), rows(D_GROUP, WIDE_OFF[SEG_LRU_GATE] // D_GROUP),
                state_in((1, D_GROUP)), state_in((CONV_PREV, D_GROUP)),
                const((CONV_W, D_GROUP)), const((1, D_GROUP)),
                const((LRU_BLOCKS, LRU_BLOCK, LRU_BLOCK)), const((1, D_GROUP)),
                const((LRU_BLOCKS, LRU_BLOCK, LRU_BLOCK)), const((1, D_GROUP)), const((1, D_GROUP))]
    state_shapes = [(1, D_GROUP), (CONV_PREV, D_GROUP)]
    scratch = [(1, D_GROUP), (CONV_PAD + cs, D_GROUP)]
    in_arrays = (wide, wide, h0, conv0, prm['cw'], prm['cb'], prm['wa'], prm['ba'], prm['wx'], prm['bx'], prm['lam'])
    return _seq_call(_lru_kernel, "rglru_mixer", in_arrays, in_specs, 7, state_shapes, scratch, prev,
                     n_seq=n_seq, seq_len=seq_len, cs=cs, g=g, out_layer=out_layer)


def _lru_params(conv_w, conv_b, w_a, b_a, w_x, b_x, lam):
    return dict(cw=conv_w, cb=conv_b[None, :], wa=w_a.astype(BF16), ba=b_a.reshape(1, D_GROUP),
                wx=w_x.astype(BF16), bx=b_x.reshape(1, D_GROUP), lam=lam[None, :])


def _kernel_states(st):
    s_ssd, s_ssd_conv, s_gdn, s_gdn_conv, s_mc, s_mn, s_mm, s_lru, s_lru_conv = st
    lead = s_mm.shape[:2]
    m_rows = jnp.zeros(lead + (1, LANES), F32).at[:, :, 0, LANE_ML_I:LANE_ML_I + MLSTM_HEADS].set(s_mm)
    return (s_ssd, s_ssd_conv, s_gdn, s_gdn_conv, s_mc, s_mn, m_rows, s_lru[:, :, None, :], s_lru_conv)


def _mixer_group(h, w_wide, w_narrow, w_layer, kst, prm, prev, *, n_seq, seq_len, row0, g, layer):
    s_ssd, s_ssd_conv, s_gdn, s_gdn_conv, s_mc, s_mn, m_rows, s_lru, s_lru_conv = kst
    m = n_seq * seq_len
    wide = _matmul(h, w_wide, w_layer, MM_TN, "in_proj", row0, m).reshape(n_seq, seq_len, WIDE_COLS)
    narrow = _matmul(h, w_narrow, w_layer, NARROW_COLS, "in_proj_gates", row0, m).reshape(
        n_seq, seq_len, NARROW_COLS)
    cs = min(seq_len, CHUNK)
    kw = dict(n_seq=n_seq, seq_len=seq_len, cs=cs, g=g, layer=layer, out_layer=w_layer)
    p_ssd, p_gdn, p_ml, p_lru = (None,) * N_MIXERS if prev is None else prev
    y_a, *o_ssd = _ssd_call(wide, narrow, s_ssd, s_ssd_conv, prm['ssd'], p_ssd, **kw)
    y_b, *o_gdn = _gdn_call(wide, narrow, s_gdn, s_gdn_conv, prm['gdn'], p_gdn, **kw)
    y_c, *o_ml = _mlstm_call(wide, narrow, s_mc, s_mn, m_rows, prm['mlstm'], p_ml, **kw)
    y_d, *o_lru = _lru_call(wide, s_lru, s_lru_conv, prm['lru'], p_lru, **kw)
    return tuple(y.reshape(m, D_GROUP) for y in (y_a, y_b, y_c, y_d)), (o_ssd, o_gdn, o_ml, o_lru)


def _reference_states(stacked):
    (ssd, ssd_cx, ssd_cbc), (gdn, gdn_conv), (mc, mn, m_rows), (lru, lru_conv) = stacked
    return (ssd, jnp.concatenate([ssd_cx, ssd_cbc], axis=-1), gdn, gdn_conv, mc, mn,
            m_rows[:, :, 0, LANE_ML_I:LANE_ML_I + MLSTM_HEADS], lru[:, :, 0, :], lru_conv)


def _state_shapes(n):
    return ((n, SSD_HEADS, SSD_HEAD_DIM, SSD_STATE),
            (n, CONV_PREV, SSD_CONV_DIM),
            (n, GDN_HEADS, GDN_HEAD_DIM, GDN_HEAD_DIM),
            (n, CONV_PREV, GDN_CONV_DIM),
            (n, MLSTM_HEADS, MLSTM_HEAD_DIM, MLSTM_HEAD_DIM),
            (n, MLSTM_HEADS, MLSTM_HEAD_DIM),
            (n, MLSTM_HEADS),
            (n, D_GROUP),
            (n, CONV_PREV, D_GROUP))


def kernel(x_prompt, x_sample, state_ssd, state_ssd_conv, state_gdn, state_gdn_conv, state_mlstm_c, state_mlstm_n, state_mlstm_m, state_rglru, state_rglru_conv, c_prompt, c_sample, w_ada, ada_table, norm1_g, norm2_g, final_g, w_in, w_out, ssd_conv_w, ssd_conv_b, ssd_dt_bias, ssd_a_log, ssd_d, ssd_norm_g, gdn_conv_w, gdn_conv_b, gdn_dt_bias, gdn_a_log, gdn_norm_g, mlstm_i_bias, mlstm_f_bias, mlstm_norm_g, lru_conv_w, lru_conv_b, lru_w_a, lru_b_a, lru_w_x, lru_b_x, lru_lambda, moe_w_group, moe_b_group, moe_w_expert, moe_b_expert, moe_w_gate, moe_w_up, moe_w_down):
    d = D_MODEL

    w_in_wide, w_in_narrow = _w_in_relayout(w_in)
    w_out_b = w_out.astype(BF16)
    w_gate_b = moe_w_gate.astype(BF16)
    w_up_b = moe_w_up.astype(BF16)
    router_pad = jnp.zeros((DEPTH, d, ROUTER_LANES - N_EXPERT_GROUPS - N_EXPERTS), F32)
    w_router = jnp.concatenate([moe_w_group, moe_w_expert, router_pad], axis=-1)
    w_router_hi = w_router.astype(BF16)
    w_router_lo = (w_router - w_router_hi.astype(F32)).astype(BF16)
    w_router = jnp.concatenate([w_router_hi, w_router_lo], axis=-1)
    b_router = jnp.concatenate([moe_b_group, moe_b_expert, router_pad[:, 0, :]], axis=-1)[:, None, :]

    c_pad = jnp.concatenate([c_prompt, c_sample, jnp.zeros((ADA_ROWS - BATCH - DEC_BATCH, d), F32)], axis=0)
    mod_shared = _ada_matmul(c_pad, w_ada)
    mod_tiles = jnp.concatenate(
        [jnp.broadcast_to(mod_shared[:BATCH, None, :], (BATCH, MOD_ROWS, 6 * d)),
         mod_shared[BATCH:BATCH + DEC_BATCH].reshape(SAMPLE_ROW_TILES, MOD_ROWS, 6 * d)], axis=0)
    ada_rows = ada_table.reshape(DEPTH * 6, 1, d)

    x = jnp.concatenate([x_prompt.reshape(N_PROMPT_TOK, d), x_sample.reshape(N_SAMPLE_TOK, d)], axis=0)

    st_sample = (state_ssd, state_ssd_conv, state_gdn, state_gdn_conv, state_mlstm_c,
                 state_mlstm_n, state_mlstm_m, state_rglru, state_rglru_conv)
    kst_sample = _kernel_states(st_sample)
    kst_prompt = _kernel_states(tuple(jnp.zeros((1,) + s, F32) for s in _state_shapes(BATCH)))
    stacked_p = stacked_s = None

    y_prev = None
    for l in range(DEPTH):
        if l == 0:
            (h,) = _rowwise(x, norm1_g[l][None, :], mod_tiles=mod_tiles, ada_rows=ada_rows, layer=l,
                            mod_chunks=(1, 0), name="norm1")
        else:
            x, h = _rowwise(x, norm1_g[l][None, :], y=y_prev, mod_tiles=mod_tiles, ada_rows=ada_rows,
                            layer=l, gate_chunk=5, gate_layer=l - 1, mod_chunks=(1, 0), name="resid_norm1")
        prm = dict(ssd=_ssd_params(ssd_conv_w[l], ssd_conv_b[l], ssd_dt_bias[l], ssd_a_log[l], ssd_d[l],
                                   ssd_norm_g[l]),
                   gdn=_gdn_params(gdn_conv_w[l], gdn_conv_b[l], gdn_dt_bias[l], gdn_a_log[l], gdn_norm_g[l]),
                   mlstm=_mlstm_params(mlstm_i_bias[l], mlstm_f_bias[l], mlstm_norm_g[l]),
                   lru=_lru_params(lru_conv_w[l], lru_conv_b[l], lru_w_a[l], lru_b_a[l], lru_w_x[l], lru_b_x[l],
                                   lru_lambda[l]))
        ys_p, stacked_p = _mixer_group(h, w_in_wide, w_in_narrow, l, kst_prompt, prm, stacked_p, n_seq=BATCH,
                                       seq_len=SEQ, row0=0, g=PROMPT_SEQS_PER_STEP, layer=0)
        ys_s, stacked_s = _mixer_group(h, w_in_wide, w_in_narrow, l, kst_sample, prm, stacked_s, n_seq=DEC_BATCH,
                                       seq_len=DEC_SEQ, row0=N_PROMPT_TOK, g=SAMPLE_SEQS_PER_STEP, layer=l)
        y_mix = _out_proj(ys_p, ys_s, w_out_b, l)

        x, h2, gates = _rowwise(x, norm2_g[l][None, :], y=y_mix, mod_tiles=mod_tiles, ada_rows=ada_rows,
                                layer=l, gate_chunk=2, mod_chunks=(4, 3),
                                router=(w_router[l], b_router[l]), name="resid_norm2_router")
        src, valid, dest, tile_ea, tile_eb, n_used = _moe_plan(gates)
        x_sorted = _take_rows(h2, src)
        gates_sorted = _take_rows(gates, src) * valid[:, None]
        y_sorted = _moe(x_sorted, gates_sorted, tile_ea, tile_eb, n_used, w_gate_b, w_up_b, moe_w_down, l)
        y_prev = _take_rows(y_sorted, dest)

    x, y_fin = _rowwise(x, final_g[None, :], y=y_prev, mod_tiles=mod_tiles, ada_rows=ada_rows,
                        layer=DEPTH - 1, gate_chunk=5, h_dtype=F32, name="resid_final_norm")
    outs = [y_fin[:N_PROMPT_TOK].reshape(BATCH, SEQ, d), y_fin[N_PROMPT_TOK:].reshape(DEC_BATCH, DEC_SEQ, d)]
    for st_p, st_s in zip(_reference_states(stacked_p), _reference_states(stacked_s)):
        outs.append(st_p)
        outs.append(st_s)
    return tuple(outs)
```
